```python
import jax
import jax.numpy as jnp
from jax import lax
import numpy as np

D_MODEL = 1024
BATCH = 8
SEQ = 2048
DEPTH = 2
DEC_BATCH = 128
DEC_SEQ = 4
PAST_LEN = 16384
PAGE_SIZE = 128

N_A_LAYERS = DEPTH // 2
N_B_LAYERS = DEPTH - N_A_LAYERS
MLA_HEADS = 16
MLA_Q_LORA = 512
MLA_KV_LORA = 256
MLA_NOPE = 64
MLA_ROPE = 32
MLA_V = 64
MLA_SCALE = (MLA_NOPE + MLA_ROPE) ** -0.5
ROPE_THETA = 10000.0
Q_BLOCK = 128
SWA_HEADS = 16
SWA_KV_HEADS = 4
SWA_GROUP = SWA_HEADS // SWA_KV_HEADS
SWA_HEAD_DIM = 64
SWA_SCALE = SWA_HEAD_DIM ** -0.5
WINDOW = 128
N_EXPERTS = 32
TOP_K = 4
EXPERT_FF = D_MODEL
SWIGLU_ALPHA = 1.702
SWIGLU_LIMIT = 7.0
MOE_BLOCK = 128
RMS_EPS = 1e-6
NEG_INF = -1e30

kernel_name = 'yoco_mla_swa_sink_moe_adaln_step'


def rms_norm(x, g):
    xf = x.astype(jnp.float32)
    y = xf * lax.rsqrt(jnp.mean(xf * xf, axis=-1, keepdims=True) + RMS_EPS)
    return (y * g.astype(jnp.float32)).astype(x.dtype)


def adaln_params(c, w, b, n):
    m = jax.nn.silu(c) @ w + b
    return m.reshape(c.shape[0], n, D_MODEL)


def modulate(h, shift, scale):
    return h * (1 + scale[:, None, :]) + shift[:, None, :]


def alibi_slopes(n):
    return jnp.exp2(-8.0 * jnp.arange(1, n + 1, dtype=jnp.float32) / n)


def rope(x, pos):
    half = MLA_ROPE // 2
    inv = ROPE_THETA ** (-jnp.arange(half, dtype=jnp.float32) / half)
    ang = pos.astype(jnp.float32)[:, None] * inv[None, :]
    ang = ang.reshape((ang.shape[0],) + (1,) * (x.ndim - 3) + (half,))
    cos, sin = jnp.cos(ang), jnp.sin(ang)
    xf = x.astype(jnp.float32)
    x1, x2 = xf[..., :half], xf[..., half:]
    return jnp.concatenate([x1 * cos - x2 * sin, x2 * cos + x1 * sin], axis=-1).astype(x.dtype)


def mla_project(u, pos, w_down, g_ql, g_kvl, w_uq, g_qn, g_qr, g_kr):
    a = u @ w_down
    cq = rms_norm(a[..., :MLA_Q_LORA], g_ql)
    ckv = rms_norm(a[..., MLA_Q_LORA:MLA_Q_LORA + MLA_KV_LORA], g_kvl)
    kr = rope(rms_norm(a[..., MLA_Q_LORA + MLA_KV_LORA:], g_kr), pos)
    q = jnp.einsum('ntc,chd->nthd', cq, w_uq)
    qn = rms_norm(q[..., :MLA_NOPE], g_qn)
    qr = rope(rms_norm(q[..., MLA_NOPE:], g_qr), pos)
    return qn, qr, ckv, kr


def mla_prompt(qn, qr, ckv, kr, w_uk, g_kn, w_uv):
    n, s_len = qn.shape[:2]
    kn = rms_norm(jnp.einsum('nsc,chd->nshd', ckv, w_uk), g_kn)
    v = jnp.einsum('nsc,chd->nshd', ckv, w_uv)
    kpos = jnp.arange(s_len)

    def block(start):
        qn_b = lax.dynamic_slice_in_dim(qn, start, Q_BLOCK, axis=1)
        qr_b = lax.dynamic_slice_in_dim(qr, start, Q_BLOCK, axis=1)
        sc = (jnp.einsum('nthd,nshd->nhts', qn_b, kn)
              + jnp.einsum('nthr,nsr->nhts', qr_b, kr)).astype(jnp.float32) * MLA_SCALE
        qpos = start + jnp.arange(Q_BLOCK)
        sc = jnp.where(kpos[None, :] <= qpos[:, None], sc, NEG_INF)
        p = jax.nn.softmax(sc, axis=-1).astype(v.dtype)
        return jnp.einsum('nhts,nshd->nthd', p, v)

    o = lax.map(block, jnp.arange(0, s_len, Q_BLOCK))
    return jnp.moveaxis(o, 0, 1).reshape(n, s_len, MLA_HEADS * MLA_V)


def mla_sample(qn, qr, ckv, kr, pool_c, pool_r, page_table, w_uk, g_kn, w_uv):
    n, t = qn.shape[:2]
    past = page_table.shape[1] * PAGE_SIZE
    causal = jnp.arange(t)[None, :] <= jnp.arange(t)[:, None]
    valid = jnp.concatenate([jnp.ones((t, past), bool), causal], axis=1)

    def one(args):
        qn1, qr1, c1, r1, pages = args
        c_all = jnp.concatenate([pool_c[pages].reshape(past, MLA_KV_LORA), c1], axis=0)
        r_all = jnp.concatenate([pool_r[pages].reshape(past, MLA_ROPE), r1], axis=0)
        kn = rms_norm(jnp.einsum('sc,chd->shd', c_all, w_uk), g_kn)
        sc = (jnp.einsum('thd,shd->hts', qn1, kn)
              + jnp.einsum('thr,sr->hts', qr1, r_all)).astype(jnp.float32) * MLA_SCALE
        p = jax.nn.softmax(jnp.where(valid, sc, NEG_INF), axis=-1).astype(c_all.dtype)
        return jnp.einsum('thc,chd->thd', jnp.einsum('hts,sc->thc', p, c_all), w_uv)

    o = lax.map(one, (qn, qr, ckv, kr, page_table))
    return o.reshape(n, t, MLA_HEADS * MLA_V)


def shared_kv(h, c, w_kvmod, b_kvmod, g_kv_norm, w_kv, g_k):
    m = adaln_params(c, w_kvmod, b_kvmod, 2)
    u = modulate(rms_norm(h, g_kv_norm), m[:, 0], m[:, 1])
    kv = (u @ w_kv).reshape(h.shape[0], h.shape[1], 2, SWA_KV_HEADS, SWA_HEAD_DIM)
    return rms_norm(kv[:, :, 0], g_k), kv[:, :, 1]


def sink_softmax(sc, sink):
    m = jnp.maximum(jnp.max(sc, axis=-1, keepdims=True), sink)
    e = jnp.exp(sc - m)
    return e / (jnp.sum(e, axis=-1, keepdims=True) + jnp.exp(sink - m))


def swa_prompt(q, k, v, sinks):
    n, s_len = q.shape[:2]
    nb = s_len // WINDOW
    qb = q.reshape(n, nb, WINDOW, SWA_KV_HEADS, SWA_GROUP, SWA_HEAD_DIM)

    def band(x):
        xp = jnp.concatenate([jnp.zeros_like(x[:, :WINDOW]), x], axis=1)
        xp = xp.reshape(n, nb + 1, WINDOW, SWA_KV_HEADS, SWA_HEAD_DIM)
        return jnp.concatenate([xp[:, :-1], xp[:, 1:]], axis=2)

    kb, vb = band(k), band(v)
    sc = jnp.einsum('nbqkgd,nbskd->nbkgqs', qb, kb).astype(jnp.float32) * SWA_SCALE
    dist = jnp.arange(WINDOW)[:, None] + WINDOW - jnp.arange(2 * WINDOW)[None, :]
    key_pos = (jnp.arange(nb)[:, None] - 1) * WINDOW + jnp.arange(2 * WINDOW)[None, :]
    valid = (dist >= 0) & (dist < WINDOW) & (key_pos[:, None, :] >= 0)
    slopes = alibi_slopes(SWA_HEADS).reshape(SWA_KV_HEADS, SWA_GROUP, 1, 1)
    sc = jnp.where(valid[:, None, None], sc - slopes * dist.astype(jnp.float32), NEG_INF)
    sink = sinks.astype(jnp.float32).reshape(SWA_KV_HEADS, SWA_GROUP, 1, 1)
    p = sink_softmax(sc, sink).astype(v.dtype)
    o = jnp.einsum('nbkgqs,nbskd->nbqkgd', p, vb)
    return o.reshape(n, s_len, SWA_HEADS * SWA_HEAD_DIM)


def swa_sample(q, k_all, v_all, sinks):
    n, t = q.shape[:2]
    dist = (WINDOW + jnp.arange(t))[:, None] - jnp.arange(WINDOW + t)[None, :]
    valid = (dist >= 0) & (dist < WINDOW)
    qg = q.reshape(n, t, SWA_KV_HEADS, SWA_GROUP, SWA_HEAD_DIM)
    sc = jnp.einsum('ntkgd,nskd->nkgts', qg, k_all).astype(jnp.float32) * SWA_SCALE
    slopes = alibi_slopes(SWA_HEADS).reshape(SWA_KV_HEADS, SWA_GROUP, 1, 1)
    sc = jnp.where(valid, sc - slopes * dist.astype(jnp.float32), NEG_INF)
    sink = sinks.astype(jnp.float32).reshape(SWA_KV_HEADS, SWA_GROUP, 1, 1)
    p = sink_softmax(sc, sink).astype(v_all.dtype)
    o = jnp.einsum('nkgts,nskd->ntkgd', p, v_all)
    return o.reshape(n, t, SWA_HEADS * SWA_HEAD_DIM)


def moe(u, w_router, b_router, w_up, b_up, w_down, b_down):
    n, t, d = u.shape
    x = u.reshape(n * t, d)
    m = x.shape[0]
    a = m * TOP_K
    logits = (x @ w_router + b_router).astype(jnp.float32)
    top_val, top_idx = lax.top_k(logits, TOP_K)
    gates = jax.nn.softmax(top_val, axis=-1).astype(x.dtype)
    e_flat = top_idx.reshape(a)
    order = jnp.argsort(e_flat)
    e_sorted = e_flat[order]
    counts = jnp.bincount(e_flat, length=N_EXPERTS)
    padded = (counts + MOE_BLOCK - 1) // MOE_BLOCK * MOE_BLOCK
    pad_end = jnp.cumsum(padded)
    slot = (pad_end - padded)[e_sorted] + jnp.arange(a) - (jnp.cumsum(counts) - counts)[e_sorted]
    n_blk = -(-a // MOE_BLOCK) + N_EXPERTS
    slot_assign = jnp.full((n_blk * MOE_BLOCK,), a, jnp.int32).at[slot].set(order.astype(jnp.int32))
    blk_expert = jnp.minimum(jnp.searchsorted(pad_end, jnp.arange(n_blk) * MOE_BLOCK, side='right'), N_EXPERTS - 1)
    tok = slot_assign // TOP_K
    x_pad = jnp.concatenate([x, jnp.zeros((1, d), x.dtype)], axis=0)
    g_pad = jnp.concatenate([gates.reshape(a), jnp.zeros((1,), x.dtype)], axis=0)

    def expert_block(args):
        xb, e = args
        h = xb @ w_up[e] + b_up[e]
        glu = jnp.minimum(h[:, :EXPERT_FF], SWIGLU_LIMIT)
        lin = jnp.clip(h[:, EXPERT_FF:], -SWIGLU_LIMIT, SWIGLU_LIMIT)
        return (glu * jax.nn.sigmoid(SWIGLU_ALPHA * glu) * (lin + 1)) @ w_down[e] + b_down[e]

    yb = lax.map(expert_block, (x_pad[tok].reshape(n_blk, MOE_BLOCK, d), blk_expert))
    y = jax.ops.segment_sum(yb.reshape(-1, d) * g_pad[slot_assign][:, None], tok, num_segments=m + 1)[:m]
    return y.reshape(n, t, d)


def setup_inputs(seed: int = 0) -> dict:
    key = jax.random.key(seed)
    ks = iter(jax.random.split(key, 64))

    def nrm(shape, scale):
        return jax.random.normal(next(ks), shape, jnp.float32) * scale

    def gain(shape):
        return 1.0 + nrm(shape, 0.05)

    d = D_MODEL
    na, nbl = N_A_LAYERS, N_B_LAYERS
    n_pages = PAST_LEN // PAGE_SIZE
    n_phys = (DEC_BATCH * n_pages * 5) // 4
    page_table = jax.random.permutation(next(ks), n_phys)[:DEC_BATCH * n_pages]
    page_table = page_table.reshape(DEC_BATCH, n_pages).astype(jnp.int32)
    return {
        'x_prompt': nrm((BATCH, SEQ, d), 1.0),
        'x_sample': nrm((DEC_BATCH, DEC_SEQ, d), 1.0),
        'c_prompt': nrm((BATCH, d), 1.0),
        'c_sample': nrm((DEC_BATCH, d), 1.0),
        'cache_mla_latent': nrm((na, n_phys, PAGE_SIZE, MLA_KV_LORA), 1.0),
        'cache_mla_krope': nrm((na, n_phys, PAGE_SIZE, MLA_ROPE), 1.0),
        'state_win_k': nrm((DEC_BATCH, WINDOW, SWA_KV_HEADS, SWA_HEAD_DIM), 1.0),
        'state_win_v': nrm((DEC_BATCH, WINDOW, SWA_KV_HEADS, SWA_HEAD_DIM), 1.0),
        'page_table': page_table,
        'w_mod': nrm((DEPTH, d, 6 * d), 0.5 * d ** -0.5),
        'b_mod': nrm((DEPTH, 6 * d), 0.02),
        'g_attn': gain((DEPTH, d)),
        'g_ffn': gain((DEPTH, d)),
        'w_mla_down': nrm((na, d, MLA_Q_LORA + MLA_KV_LORA + MLA_ROPE), d ** -0.5),
        'g_mla_q_lora': gain((na, MLA_Q_LORA)),
        'g_mla_kv_lora': gain((na, MLA_KV_LORA)),
        'w_mla_uq': nrm((na, MLA_Q_LORA, MLA_HEADS, MLA_NOPE + MLA_ROPE), MLA_Q_LORA ** -0.5),
        'w_mla_uk': nrm((na, MLA_KV_LORA, MLA_HEADS, MLA_NOPE), MLA_KV_LORA ** -0.5),
        'w_mla_uv': nrm((na, MLA_KV_LORA, MLA_HEADS, MLA_V), MLA_KV_LORA ** -0.5),
        'g_mla_qn': gain((na, MLA_NOPE)),
        'g_mla_qr': gain((na, MLA_ROPE)),
        'g_mla_kn': gain((na, MLA_NOPE)),
        'g_mla_kr': gain((na, MLA_ROPE)),
        'w_mla_o': nrm((na, MLA_HEADS * MLA_V, d), (MLA_HEADS * MLA_V) ** -0.5),
        'w_kvmod': nrm((d, 2 * d), 0.5 * d ** -0.5),
        'b_kvmod': nrm((2 * d,), 0.02),
        'g_kv_norm': gain((d,)),
        'w_kv': nrm((d, 2 * SWA_KV_HEADS * SWA_HEAD_DIM), d ** -0.5),
        'g_swa_k': gain((SWA_HEAD_DIM,)),
        'w_swa_q': nrm((nbl, d, SWA_HEADS * SWA_HEAD_DIM), d ** -0.5),
        'g_swa_q': gain((nbl, SWA_HEAD_DIM)),
        'swa_sinks': nrm((nbl, SWA_HEADS), 1.0),
        'w_swa_o': nrm((nbl, SWA_HEADS * SWA_HEAD_DIM, d), (SWA_HEADS * SWA_HEAD_DIM) ** -0.5),
        'w_router': nrm((DEPTH, d, N_EXPERTS), d ** -0.5),
        'b_router': nrm((DEPTH, N_EXPERTS), 0.01),
        'w_up': nrm((DEPTH, N_EXPERTS, d, 2 * EXPERT_FF), d ** -0.5),
        'b_up': nrm((DEPTH, N_EXPERTS, 2 * EXPERT_FF), 0.02),
        'w_down': nrm((DEPTH, N_EXPERTS, EXPERT_FF, d), EXPERT_FF ** -0.5),
        'b_down': nrm((DEPTH, N_EXPERTS, d), 0.02),
    }


def reference(x_prompt, x_sample, c_prompt, c_sample, cache_mla_latent, cache_mla_krope,
              state_win_k, state_win_v, page_table, w_mod, b_mod, g_attn, g_ffn,
              w_mla_down, g_mla_q_lora, g_mla_kv_lora, w_mla_uq, w_mla_uk, w_mla_uv,
              g_mla_qn, g_mla_qr, g_mla_kn, g_mla_kr, w_mla_o, w_kvmod, b_kvmod, g_kv_norm,
              w_kv, g_swa_k, w_swa_q, g_swa_q, swa_sinks, w_swa_o, w_router, b_router,
              w_up, b_up, w_down, b_down):
    nb_p, s_len = x_prompt.shape[:2]
    nb_s, t_len = x_sample.shape[:2]
    past = page_table.shape[1] * PAGE_SIZE
    pos_p = jnp.arange(s_len)
    pos_s = past + jnp.arange(t_len)
    hp, hs = x_prompt, x_sample
    lat_p, kr_p, lat_s, kr_s = [], [], [], []
    for layer in range(DEPTH):
        mp = adaln_params(c_prompt, w_mod[layer], b_mod[layer], 6)
        ms = adaln_params(c_sample, w_mod[layer], b_mod[layer], 6)
        up = modulate(rms_norm(hp, g_attn[layer]), mp[:, 0], mp[:, 1])
        us = modulate(rms_norm(hs, g_attn[layer]), ms[:, 0], ms[:, 1])
        if layer < N_A_LAYERS:
            i = layer
            pw = (w_mla_down[i], g_mla_q_lora[i], g_mla_kv_lora[i], w_mla_uq[i], g_mla_qn[i], g_mla_qr[i], g_mla_kr[i])
            qn, qr, ckv_p, kro_p = mla_project(up, pos_p, *pw)
            att_p = mla_prompt(qn, qr, ckv_p, kro_p, w_mla_uk[i], g_mla_kn[i], w_mla_uv[i]) @ w_mla_o[i]
            qn, qr, ckv_s, kro_s = mla_project(us, pos_s, *pw)
            att_s = mla_sample(qn, qr, ckv_s, kro_s, cache_mla_latent[i], cache_mla_krope[i], page_table,
                               w_mla_uk[i], g_mla_kn[i], w_mla_uv[i]) @ w_mla_o[i]
            lat_p.append(ckv_p)
            kr_p.append(kro_p)
            lat_s.append(ckv_s)
            kr_s.append(kro_s)
        else:
            j = layer - N_A_LAYERS
            if j == 0:
                k_p, v_p = shared_kv(hp, c_prompt, w_kvmod, b_kvmod, g_kv_norm, w_kv, g_swa_k)
                k_n, v_n = shared_kv(hs, c_sample, w_kvmod, b_kvmod, g_kv_norm, w_kv, g_swa_k)
                k_all_s = jnp.concatenate([state_win_k, k_n], axis=1)
                v_all_s = jnp.concatenate([state_win_v, v_n], axis=1)
                win_k_p, win_v_p = k_p[:, -WINDOW:], v_p[:, -WINDOW:]
                win_k_s, win_v_s = k_all_s[:, -WINDOW:], v_all_s[:, -WINDOW:]
            q_p = rms_norm((up @ w_swa_q[j]).reshape(nb_p, s_len, SWA_HEADS, SWA_HEAD_DIM), g_swa_q[j])
            q_s = rms_norm((us @ w_swa_q[j]).reshape(nb_s, t_len, SWA_HEADS, SWA_HEAD_DIM), g_swa_q[j])
            att_p = swa_prompt(q_p, k_p, v_p, swa_sinks[j]) @ w_swa_o[j]
            att_s = swa_sample(q_s, k_all_s, v_all_s, swa_sinks[j]) @ w_swa_o[j]
        hp = hp + mp[:, 2][:, None] * att_p
        hs = hs + ms[:, 2][:, None] * att_s
        mw = (w_router[layer], b_router[layer], w_up[layer], b_up[layer], w_down[layer], b_down[layer])
        hp = hp + mp[:, 5][:, None] * moe(modulate(rms_norm(hp, g_ffn[layer]), mp[:, 3], mp[:, 4]), *mw)
        hs = hs + ms[:, 5][:, None] * moe(modulate(rms_norm(hs, g_ffn[layer]), ms[:, 3], ms[:, 4]), *mw)
    return (hp, hs, jnp.stack(lat_p), jnp.stack(kr_p), jnp.stack(lat_s), jnp.stack(kr_s),
            win_k_p, win_v_p, win_k_s, win_v_s)
```

```python
import functools

import jax
import jax.numpy as jnp
from jax import lax
from jax.experimental import pallas as pl
from jax.experimental.pallas import tpu as pltpu

F32 = jnp.float32
BF16 = jnp.bfloat16

RMS_EPS = 1e-6
NEG_INF = -1e30
ROPE_THETA = 10000.0
PAGE = 128
WINDOW = 128
TOP_K = 4
SWIGLU_ALPHA = 1.702
SWIGLU_LIMIT = 7.0
LANES = 128
VMEM_LIMIT = 56 * 1024 * 1024

TM = 512
MOE_TM = 256
SAMPLE_PAGES = 8
SAMPLE_SUB = 256


def _nn(a, b):
    return jnp.dot(a, b, preferred_element_type=F32)


def _nt(a, b):
    return lax.dot_general(a, b, (((1,), (1,)), ((), ())), preferred_element_type=F32)


def _params(*sem):
    return pltpu.CompilerParams(dimension_semantics=sem, vmem_limit_bytes=VMEM_LIMIT)


def _norm_mod(h, g, shift, scale):
    tm, d = h.shape
    nb = shift.shape[0]
    y = h * lax.rsqrt(jnp.mean(h * h, axis=-1, keepdims=True) + RMS_EPS) * g
    y = y.reshape(tm // nb, nb, d) * (1.0 + scale[None]) + shift[None]
    return y.reshape(tm, d)


def _group64_rscale(x):
    lane = lax.broadcasted_iota(jnp.int32, x.shape, 1)
    x2 = x * x
    lo = jnp.sum(jnp.where(lane < 64, x2, 0.0), axis=-1, keepdims=True)
    hi = jnp.sum(jnp.where(lane >= 64, x2, 0.0), axis=-1, keepdims=True)
    return jnp.where(lane < 64, lax.rsqrt(lo / 64.0 + RMS_EPS), lax.rsqrt(hi / 64.0 + RMS_EPS))


def _adaln_kernel(c_ref, w_ref, b_ref, o_ref):
    c = c_ref[...]
    a = (c * jax.nn.sigmoid(c)).astype(BF16)
    o_ref[...] = _nn(a, w_ref[...].astype(BF16)) + b_ref[...]


def _adaln(c, w, b):
    n, d = c.shape
    nout = w.shape[1]
    tn = 1024
    return pl.pallas_call(
        _adaln_kernel,
        grid=(nout // tn,),
        in_specs=[pl.BlockSpec((n, d), lambda j: (0, 0)),
                  pl.BlockSpec((d, tn), lambda j: (0, j)),
                  pl.BlockSpec((1, tn), lambda j: (0, j))],
        out_specs=pl.BlockSpec((n, tn), lambda j: (0, j)),
        out_shape=jax.ShapeDtypeStruct((n, nout), F32),
        compiler_params=_params("parallel"),
        name="adaln",
    )(c, w, b.reshape(1, nout))


def _mod_table(m, nb_prompt):
    nb = m.shape[0] - nb_prompt
    mp = jnp.broadcast_to(m[:nb_prompt, None, :], (nb_prompt, nb, m.shape[1]))
    return jnp.concatenate([mp, m[nb_prompt:][None]], axis=0)


def _mla_down_kernel(h_ref, g_ref, sh_ref, sc_ref, w_ref, gql_ref, gkvl_ref, gkr_ref, gkrp_ref,
                     cos_ref, sin_ref, cq_ref, ckv_ref, kr_ref, *, q_lora, kv_lora, rope):
    u = _norm_mod(h_ref[...], g_ref[...], sh_ref[...], sc_ref[...]).astype(BF16)
    a = _nn(u, w_ref[...])
    q = a[:, :q_lora]
    cq_ref[...] = (q * lax.rsqrt(jnp.mean(q * q, axis=-1, keepdims=True) + RMS_EPS)
                   * gql_ref[...]).astype(BF16)
    c = a[:, q_lora:q_lora + kv_lora]
    ckv_ref[...] = c * lax.rsqrt(jnp.mean(c * c, axis=-1, keepdims=True) + RMS_EPS) * gkvl_ref[...]
    o = q_lora + kv_lora
    raw = a[:, o:o + rope]
    rot = a[:, o + rope:o + 2 * rope]
    r = lax.rsqrt(jnp.mean(raw * raw, axis=-1, keepdims=True) + RMS_EPS)
    kr_ref[...] = r * (raw * gkr_ref[...] * cos_ref[...] + rot * gkrp_ref[...] * sin_ref[...])


def _q_up_kernel(cq_ref, w_ref, gq_ref, tab_ref, o_ref, *, heads, nope, rope, scale):
    a = _nn(cq_ref[...], w_ref[...])
    tab = tab_ref[...]
    gq = gq_ref[...]
    for h in range(heads):
        x = a[:, h * LANES:(h + 1) * LANES]
        lane = lax.broadcasted_iota(jnp.int32, x.shape, 1)
        x2 = x * x
        ssn = jnp.sum(jnp.where(lane < nope, x2, 0.0), axis=-1, keepdims=True)
        ssr = jnp.sum(jnp.where((lane >= nope) & (lane < nope + rope), x2, 0.0), axis=-1, keepdims=True)
        r = jnp.where(lane < nope, lax.rsqrt(ssn / nope + RMS_EPS), lax.rsqrt(ssr / rope + RMS_EPS))
        o_ref[h] = (x * r * gq * tab * scale).astype(BF16)


def _kv_up_kernel(ckv_ref, kr_ref, wukt_ref, gkn_ref, wuv_ref, eye_ref, kt_ref, v_ref, *, heads, nope, rope):
    c = ckv_ref[...].astype(BF16)
    tm = c.shape[0]
    knt = _nt(wukt_ref[...], c).reshape(heads, nope, tm)
    ss = jnp.sum(knt * knt, axis=1, keepdims=True)
    kn = knt * lax.rsqrt(ss / nope + RMS_EPS) * gkn_ref[...][None]
    krt = _nt(eye_ref[...], kr_ref[...].astype(BF16)).astype(BF16)
    krt = jnp.broadcast_to(krt[None], (heads, rope, tm))
    kt_ref[0, :, 0, 0:nope, :] = kn.astype(BF16)
    kt_ref[0, :, 0, nope:nope + rope, :] = krt
    kt_ref[0, :, 0, nope + rope:nope + 2 * rope, :] = krt
    v_ref[...] = _nn(c, wuv_ref[...]).astype(BF16)


def _mla_flash_kernel(q_ref, kt_ref, v_ref, o_ref, *, tq, tk, vdim):
    qi = pl.program_id(2)
    n_chunks = ((qi + 1) * tq + tk - 1) // tk
    outs = []
    for hh in range(2):
        q = q_ref[hh]

        def body(j, carry, hh=hh, q=q):
            m, l, acc = carry
            s = _nn(q, kt_ref[0, hh, j])
            row = qi * tq + lax.broadcasted_iota(jnp.int32, s.shape, 0)
            col = j * tk + lax.broadcasted_iota(jnp.int32, s.shape, 1)
            s = jnp.where(col <= row, s, NEG_INF)
            m_new = jnp.maximum(m, jnp.max(s, axis=-1, keepdims=True))
            alpha = jnp.exp(m - m_new)
            p = jnp.exp(s - m_new)
            l = alpha * l + jnp.sum(p, axis=-1, keepdims=True)
            start = pl.multiple_of(j * tk, tk)
            acc = alpha * acc + _nn(p.astype(BF16), v_ref[pl.ds(start, tk), :])
            return m_new, l, acc

        init = (jnp.full((tq, 1), NEG_INF, F32), jnp.zeros((tq, 1), F32), jnp.zeros((tq, LANES), F32))
        m, l, acc = lax.fori_loop(0, n_chunks, body, init)
        outs.append(acc / l)
    lane = lax.broadcasted_iota(jnp.int32, (tq, LANES), 1)
    o_ref[...] = jnp.where(lane < vdim, outs[0], outs[1]).astype(BF16)


def _q_absorb_kernel(q_ref, wuk_ref, gkn_ref, qa_ref, qr_ref, *, heads, nope, rope):
    for h in range(heads):
        q = q_ref[h].astype(F32)
        qg = (q[:, :nope] * gkn_ref[...]).astype(BF16)
        qa_ref[h] = _nn(qg, wuk_ref[h]).astype(BF16)
        qr_ref[h] = (q[:, nope:nope + rope] + q[:, nope + rope:nope + 2 * rope]).astype(BF16)


def _mla_sample_kernel(pt_ref, qa_ref, qr_ref, wukt_ref, cnew_ref, rnew_ref, poolc_ref, poolr_ref,
                       o_ref, cbuf, rbuf, sems, *, heads, nope, n_chunks, t_new):
    n = pl.program_id(0)
    n_batch = pl.num_programs(0)
    cp = cbuf.shape[1]
    rows = qa_ref.shape[1]
    lat = qa_ref.shape[2]

    def copies(b, ci, slot):
        out = []
        for p in range(cp):
            page = pt_ref[b, ci * cp + p]
            out.append(pltpu.make_async_copy(poolc_ref.at[page], cbuf.at[slot, p], sems.at[0, slot]))
            out.append(pltpu.make_async_copy(poolr_ref.at[page], rbuf.at[slot, p], sems.at[1, slot]))
        return out

    @pl.when(n == 0)
    def _():
        for cpy in copies(0, 0, 0):
            cpy.start()

    qa = qa_ref[0]
    qr = qr_ref[0]
    wukt = wukt_ref[...]

    def tile(c, kr, mask, carry):
        m, l, acc = carry
        tk = c.shape[0]
        knt = _nt(wukt, c).reshape(heads, nope, tk)
        r = lax.rsqrt(jnp.sum(knt * knt, axis=1) / nope + RMS_EPS)
        s = _nt(qa, c) * jnp.concatenate([r] * t_new, axis=0) + _nt(qr, kr)
        if mask is not None:
            s = jnp.where(mask, s, NEG_INF)
        m_new = jnp.maximum(m, jnp.max(s, axis=-1, keepdims=True))
        alpha = jnp.exp(m - m_new)
        p = jnp.exp(s - m_new)
        l = alpha * l + jnp.sum(p, axis=-1, keepdims=True)
        acc = alpha * acc + _nn(p.astype(BF16), c)
        return m_new, l, acc

    def chunk(ci, carry):
        g = n * n_chunks + ci
        slot = g % 2
        for cpy in copies(n, ci, slot):
            cpy.wait()

        @pl.when(ci + 1 < n_chunks)
        def _():
            for cpy in copies(n, ci + 1, 1 - slot):
                cpy.start()

        @pl.when((ci + 1 == n_chunks) & (n + 1 < n_batch))
        def _():
            for cpy in copies(n + 1, 0, 1 - slot):
                cpy.start()

        for p0 in range(0, cp, SAMPLE_SUB // PAGE):
            pp = SAMPLE_SUB // PAGE
            c = cbuf[slot, p0:p0 + pp].reshape(SAMPLE_SUB, lat).astype(BF16)
            kr = rbuf[slot, p0:p0 + pp].reshape(SAMPLE_SUB, rbuf.shape[3]).astype(BF16)
            carry = tile(c, kr, None, carry)
        return carry

    init = (jnp.full((rows, 1), NEG_INF, F32), jnp.zeros((rows, 1), F32), jnp.zeros((rows, lat), F32))
    carry = lax.fori_loop(0, n_chunks, chunk, init)
    c = cnew_ref[0].astype(BF16)
    kr = rnew_ref[0].astype(BF16)
    tk = c.shape[0]
    t_of_row = lax.broadcasted_iota(jnp.int32, (rows, tk), 0) // heads
    col = lax.broadcasted_iota(jnp.int32, (rows, tk), 1)
    m, l, acc = tile(c, kr, col <= t_of_row, carry)
    o_ref[0] = acc / l


def _unabsorb_kernel(o_ref, wuv_ref, out_ref, *, heads):
    out_ref[...] = jnp.concatenate(
        [_nn(o_ref[h], wuv_ref[h]) for h in range(heads)], axis=-1).astype(BF16)


def _oproj_kernel(att_ref, w_ref, h_ref, gate_ref, o_ref):
    y = _nn(att_ref[...], w_ref[...])
    tm, d = y.shape
    gate = gate_ref[...]
    nb = gate.shape[0]
    o_ref[...] = h_ref[...] + (y.reshape(tm // nb, nb, d) * gate[None]).reshape(tm, d)


def _router_kernel(h_ref, g_ref, sh_ref, sc_ref, whi_ref, wlo_ref, b_ref, u_ref, idx_ref, gate_ref):
    u = _norm_mod(h_ref[...], g_ref[...], sh_ref[...], sc_ref[...])
    uhi = u.astype(BF16)
    ulo = (u - uhi.astype(F32)).astype(BF16)
    u_ref[...] = uhi
    whi = whi_ref[...]
    logits = _nt(whi, uhi) + _nt(whi, ulo) + _nt(wlo_ref[...], uhi) + b_ref[...]
    ne = logits.shape[0]
    eid = lax.broadcasted_iota(jnp.int32, logits.shape, 0)
    work = logits
    vals, idxs = [], []
    for _ in range(TOP_K):
        m = jnp.max(work, axis=0, keepdims=True)
        idx = jnp.min(jnp.where(work == m, eid, ne), axis=0, keepdims=True)
        vals.append(m)
        idxs.append(idx)
        work = jnp.where(eid == idx, -jnp.inf, work)
    es = [jnp.exp(v - vals[0]) for v in vals]
    den = es[0] + es[1] + es[2] + es[3]
    idx_ref[...] = jnp.concatenate(idxs, axis=0)
    gate_ref[...] = jnp.concatenate([e / den for e in es], axis=0)


def _expert_kernel(be_ref, nused_ref, x_ref, wup_ref, bup_ref, wdn_ref, bdn_ref, y_ref, wup_bf, wdn_bf, *, ff):
    i = pl.program_id(0)
    prev = be_ref[jnp.maximum(i - 1, 0)]
    new_expert = (i == 0) | (be_ref[i] != prev)

    @pl.when(new_expert & (i < nused_ref[0]))
    def _():
        wup_bf[...] = wup_ref[0].astype(BF16)
        wdn_bf[...] = wdn_ref[0].astype(BF16)

    @pl.when(i < nused_ref[0])
    def _():
        x = x_ref[...]
        fc = 512
        acc = jnp.zeros(y_ref.shape, F32)
        for c0 in range(0, ff, fc):
            glu = _nn(x, wup_bf[:, c0:c0 + fc]) + bup_ref[0, :, c0:c0 + fc]
            lin = _nn(x, wup_bf[:, ff + c0:ff + c0 + fc]) + bup_ref[0, :, ff + c0:ff + c0 + fc]
            glu = jnp.minimum(glu, SWIGLU_LIMIT)
            lin = jnp.clip(lin, -SWIGLU_LIMIT, SWIGLU_LIMIT)
            act = glu * jax.nn.sigmoid(SWIGLU_ALPHA * glu) * (lin + 1.0)
            acc = acc + _nn(act.astype(BF16), wdn_bf[c0:c0 + fc, :])
        y_ref[...] = acc + bdn_ref[0]

    @pl.when(i >= nused_ref[0])
    def _():
        y_ref[...] = jnp.zeros(y_ref.shape, F32)


def _combine_kernel(h_ref, y_ref, gk_ref, gate_ref, o_ref):
    gk = gk_ref[...]
    y = y_ref[0] * gk[:, 0:1]
    for k in range(1, TOP_K):
        y = y + y_ref[k] * gk[:, k:k + 1]
    tm, d = y.shape
    gate = gate_ref[...]
    nb = gate.shape[0]
    o_ref[...] = h_ref[...] + (y.reshape(tm // nb, nb, d) * gate[None]).reshape(tm, d)


def _shared_kv_kernel(h_ref, g_ref, sh_ref, sc_ref, w_ref, gk_ref, k_ref, v_ref, khm_ref, vhm_ref, *, kvh, hd):
    u = _norm_mod(h_ref[...], g_ref[...], sh_ref[...], sc_ref[...]).astype(BF16)
    a = _nn(u, w_ref[...])
    kw = kvh * hd
    gk = gk_ref[...]
    ks = []
    for j in range(kw // LANES):
        x = a[:, j * LANES:(j + 1) * LANES]
        ks.append(x * _group64_rscale(x) * gk[:, j * LANES:(j + 1) * LANES])
    k = jnp.concatenate(ks, axis=-1)
    v = a[:, kw:2 * kw]
    k_ref[...] = k
    v_ref[...] = v
    for hh in range(kvh):
        khm_ref[hh] = k[:, hh * hd:(hh + 1) * hd].astype(BF16)
        vhm_ref[hh] = v[:, hh * hd:(hh + 1) * hd].astype(BF16)


def _swa_q_kernel(h_ref, g_ref, sh_ref, sc_ref, w_ref, gq_ref, q_ref, *, heads, hd, scale):
    u = _norm_mod(h_ref[...], g_ref[...], sh_ref[...], sc_ref[...]).astype(BF16)
    a = _nn(u, w_ref[...])
    gq = gq_ref[...]
    for j in range(heads * hd // LANES):
        x = a[:, j * LANES:(j + 1) * LANES]
        y = x * _group64_rscale(x) * gq[:, j * LANES:(j + 1) * LANES] * scale
        q_ref[2 * j] = y[:, :hd].astype(BF16)
        q_ref[2 * j + 1] = y[:, hd:].astype(BF16)


def _swa_kernel(sink_ref, q_ref, kp_ref, kc_ref, vp_ref, vc_ref, o_ref, *, heads, kvh, tq, first_block_has_no_prev):
    group = heads // kvh
    nk = WINDOW + tq
    rows = group * tq
    rowi = lax.broadcasted_iota(jnp.int32, (rows, nk), 0)
    col = lax.broadcasted_iota(jnp.int32, (rows, nk), 1)
    t = rowi % tq
    dist = t + WINDOW - col
    valid = (dist >= 0) & (dist < WINDOW)
    if first_block_has_no_prev:
        valid = valid & ((col >= WINDOW) | (pl.program_id(1) > 0))
    distf = dist.astype(F32)
    hrow = lax.broadcasted_iota(jnp.int32, (rows, 1), 0) // tq
    outs = []
    for g in range(kvh):
        k = jnp.concatenate([kp_ref[g], kc_ref[g]], axis=0)
        v = jnp.concatenate([vp_ref[g], vc_ref[g]], axis=0)
        q = q_ref[g * group:(g + 1) * group].reshape(rows, q_ref.shape[2])
        slope = jnp.zeros((rows, 1), F32)
        sink = jnp.zeros((rows, 1), F32)
        for j in range(group):
            h = g * group + j
            slope = jnp.where(hrow == j, 2.0 ** (-8.0 * (h + 1) / heads), slope)
            sink = jnp.where(hrow == j, sink_ref[h], sink)
        s = jnp.where(valid, _nt(q, k) - slope * distf, NEG_INF)
        m = jnp.maximum(jnp.max(s, axis=-1, keepdims=True), sink)
        e = jnp.exp(s - m)
        p = e / (jnp.sum(e, axis=-1, keepdims=True) + jnp.exp(sink - m))
        o = _nn(p.astype(BF16), v)
        for j in range(group):
            outs.append(o[j * tq:(j + 1) * tq])
    o_ref[...] = jnp.concatenate(outs, axis=-1).astype(BF16)


def _rope_tables(pos, rope):
    half = rope // 2
    inv = ROPE_THETA ** (-jnp.arange(half, dtype=F32) / half)
    ang = pos.astype(F32)[:, None] * inv[None, :]
    cos, sin = jnp.cos(ang), jnp.sin(ang)
    return jnp.concatenate([cos, cos], axis=-1), jnp.concatenate([sin, sin], axis=-1)


def _rot_cols(w):
    half = w.shape[-1] // 2
    return jnp.concatenate([-w[..., half:], w[..., :half]], axis=-1)


def _swap_halves(g):
    half = g.shape[-1] // 2
    return jnp.concatenate([g[..., half:], g[..., :half]], axis=-1)


def _moe(h, tab, col0, g_ffn, w_router, b_router, w_up, b_up, w_down, b_down, *, seq, nb_prompt):
    m_rows, d = h.shape
    ne = w_router.shape[1]
    ff = w_down.shape[1]
    nb = tab.shape[1]
    n_tok_blocks = m_rows // TM

    def tab_idx(i):
        return jnp.minimum(i * TM // seq, nb_prompt)

    wr_t = w_router.T
    wr_hi = wr_t.astype(BF16)
    wr_lo = (wr_t - wr_hi.astype(F32)).astype(BF16)
    u, idx_t, gate_t = pl.pallas_call(
        _router_kernel,
        grid=(n_tok_blocks,),
        in_specs=[pl.BlockSpec((TM, d), lambda i: (i, 0)),
                  pl.BlockSpec((1, d), lambda i: (0, 0)),
                  pl.BlockSpec((None, nb, d), lambda i: (tab_idx(i), 0, col0)),
                  pl.BlockSpec((None, nb, d), lambda i: (tab_idx(i), 0, col0 + 1)),
                  pl.BlockSpec((ne, d), lambda i: (0, 0)),
                  pl.BlockSpec((ne, d), lambda i: (0, 0)),
                  pl.BlockSpec((ne, 1), lambda i: (0, 0))],
        out_specs=[pl.BlockSpec((TM, d), lambda i: (i, 0)),
                   pl.BlockSpec((TOP_K, TM), lambda i: (0, i)),
                   pl.BlockSpec((TOP_K, TM), lambda i: (0, i))],
        out_shape=[jax.ShapeDtypeStruct((m_rows, d), BF16),
                   jax.ShapeDtypeStruct((TOP_K, m_rows), jnp.int32),
                   jax.ShapeDtypeStruct((TOP_K, m_rows), F32)],
        compiler_params=_params("parallel"),
        name="router",
    )(h, g_ffn.reshape(1, d), tab, tab, wr_hi, wr_lo, b_router.reshape(ne, 1))

    a = m_rows * TOP_K
    e_flat = idx_t.T.reshape(a)
    order = jnp.argsort(e_flat)
    e_sorted = e_flat[order]
    counts = jnp.bincount(e_flat, length=ne)
    padded = (counts + MOE_TM - 1) // MOE_TM * MOE_TM
    pad_end = jnp.cumsum(padded)
    slot = (pad_end - padded)[e_sorted] + jnp.arange(a) - (jnp.cumsum(counts) - counts)[e_sorted]
    n_blk = -(-a // MOE_TM) + ne
    n_slots = n_blk * MOE_TM
    slot_tok = jnp.zeros((n_slots,), jnp.int32).at[slot].set((order // TOP_K).astype(jnp.int32))
    pos = jnp.zeros((a,), jnp.int32).at[order].set(slot.astype(jnp.int32))
    blk_expert = jnp.minimum(jnp.searchsorted(pad_end, jnp.arange(n_blk) * MOE_TM, side='right'),
                             ne - 1).astype(jnp.int32)
    n_used = (pad_end[-1] // MOE_TM).astype(jnp.int32).reshape(1)
    xs = jnp.take(u, slot_tok, axis=0)

    def live(i, nused):
        return jnp.minimum(i, nused[0] - 1)

    yb = pl.pallas_call(
        functools.partial(_expert_kernel, ff=ff),
        grid_spec=pltpu.PrefetchScalarGridSpec(
            num_scalar_prefetch=2,
            grid=(n_blk,),
            in_specs=[pl.BlockSpec((MOE_TM, d), lambda i, be, nu: (live(i, nu), 0)),
                      pl.BlockSpec((1, d, 2 * ff), lambda i, be, nu: (be[live(i, nu)], 0, 0)),
                      pl.BlockSpec((1, 1, 2 * ff), lambda i, be, nu: (be[live(i, nu)], 0, 0)),
                      pl.BlockSpec((1, ff, d), lambda i, be, nu: (be[live(i, nu)], 0, 0)),
                      pl.BlockSpec((1, 1, d), lambda i, be, nu: (be[live(i, nu)], 0, 0))],
            out_specs=pl.BlockSpec((MOE_TM, d), lambda i, be, nu: (i, 0)),
            scratch_shapes=[pltpu.VMEM((d, 2 * ff), BF16), pltpu.VMEM((ff, d), BF16)]),
        out_shape=jax.ShapeDtypeStruct((n_slots, d), F32),
        compiler_params=_params("arbitrary"),
        name="experts",
    )(blk_expert, n_used, xs, w_up, b_up.reshape(ne, 1, 2 * ff), w_down, b_down.reshape(ne, 1, d))

    ysel = jnp.take(yb, pos.reshape(m_rows, TOP_K).T, axis=0)
    return pl.pallas_call(
        _combine_kernel,
        grid=(n_tok_blocks,),
        in_specs=[pl.BlockSpec((TM, d), lambda i: (i, 0)),
                  pl.BlockSpec((TOP_K, TM, d), lambda i: (0, i, 0)),
                  pl.BlockSpec((TM, TOP_K), lambda i: (i, 0)),
                  pl.BlockSpec((None, nb, d), lambda i: (tab_idx(i), 0, col0 + 2))],
        out_specs=pl.BlockSpec((TM, d), lambda i: (i, 0)),
        out_shape=jax.ShapeDtypeStruct((m_rows, d), F32),
        compiler_params=_params("parallel"),
        name="moe_combine",
    )(h, ysel, gate_t.T, tab)


def kernel(x_prompt, x_sample, c_prompt, c_sample, cache_mla_latent, cache_mla_krope, state_win_k, state_win_v, page_table, w_mod, b_mod, g_attn, g_ffn, w_mla_down, g_mla_q_lora, g_mla_kv_lora, w_mla_uq, w_mla_uk, w_mla_uv, g_mla_qn, g_mla_qr, g_mla_kn, g_mla_kr, w_mla_o, w_kvmod, b_kvmod, g_kv_norm, w_kv, g_swa_k, w_swa_q, g_swa_q, swa_sinks, w_swa_o, w_router, b_router, w_up, b_up, w_down, b_down):
    nbp, seq, d = x_prompt.shape
    nbs, t_new, _ = x_sample.shape
    q_lora = g_mla_q_lora.shape[1]
    kv_lora = g_mla_kv_lora.shape[1]
    heads, nope = w_mla_uk.shape[2], w_mla_uk.shape[3]
    rope = g_mla_qr.shape[1]
    vdim = w_mla_uv.shape[3]
    n_pages = page_table.shape[1]
    past = n_pages * PAGE
    swa_heads = swa_sinks.shape[1]
    hd = g_swa_k.shape[0]
    kvh = w_kv.shape[1] // (2 * hd)
    assert nope + 2 * rope == LANES and 2 * vdim == LANES and 2 * hd == LANES
    assert seq % TM == 0 and TM % nbs == 0 and (nbs * t_new) % TM == 0 and nbs % 8 == 0
    assert w_mod.shape[0] == 2 and n_pages % SAMPLE_PAGES == 0

    mp = nbp * seq
    ms = nbs * t_new
    m_rows = mp + ms
    n_tok_blocks = m_rows // TM
    n_prompt_blocks = mp // TM
    blocks_per_seq = seq // TM

    def tab_idx(i):
        return jnp.minimum(i * TM // seq, nbp)

    def tok(i):
        return (i, 0)

    def const2(i):
        return (0, 0)

    h0 = jnp.concatenate([x_prompt.reshape(mp, d), x_sample.transpose(1, 0, 2).reshape(ms, d)], axis=0)
    c_all = jnp.concatenate([c_prompt, c_sample], axis=0)
    tabs = [_mod_table(_adaln(c_all, w_mod[l], b_mod[l]), nbp) for l in range(2)]
    tab_kv = _mod_table(_adaln(c_all, w_kvmod, b_kvmod), nbp)

    def tab_spec(col):
        return pl.BlockSpec((None, nbs, d), lambda i: (tab_idx(i), 0, col))

    cos_p, sin_p = _rope_tables(jnp.arange(seq), rope)
    cos_s, sin_s = _rope_tables(past + jnp.arange(t_new), rope)
    cos_tab = jnp.concatenate([cos_p, jnp.repeat(cos_s, nbs, axis=0)], axis=0)
    sin_tab = jnp.concatenate([sin_p, jnp.repeat(sin_s, nbs, axis=0)], axis=0)

    def pos_blk(i):
        return (jnp.where(i < n_prompt_blocks, i % blocks_per_seq, blocks_per_seq + i - n_prompt_blocks), 0)

    wd = w_mla_down[0]
    w_down_ext = jnp.concatenate([wd, _rot_cols(wd[:, q_lora + kv_lora:])], axis=1).astype(BF16)
    nd = w_down_ext.shape[1]
    cq, ckv, kr = pl.pallas_call(
        functools.partial(_mla_down_kernel, q_lora=q_lora, kv_lora=kv_lora, rope=rope),
        grid=(n_tok_blocks,),
        in_specs=[pl.BlockSpec((TM, d), tok), pl.BlockSpec((1, d), const2), tab_spec(0), tab_spec(1),
                  pl.BlockSpec((d, nd), const2), pl.BlockSpec((1, q_lora), const2),
                  pl.BlockSpec((1, kv_lora), const2), pl.BlockSpec((1, rope), const2),
                  pl.BlockSpec((1, rope), const2), pl.BlockSpec((TM, rope), pos_blk),
                  pl.BlockSpec((TM, rope), pos_blk)],
        out_specs=[pl.BlockSpec((TM, q_lora), tok), pl.BlockSpec((TM, kv_lora), tok),
                   pl.BlockSpec((TM, rope), tok)],
        out_shape=[jax.ShapeDtypeStruct((m_rows, q_lora), BF16),
                   jax.ShapeDtypeStruct((m_rows, kv_lora), F32),
                   jax.ShapeDtypeStruct((m_rows, rope), F32)],
        compiler_params=_params("parallel"),
        name="mla_down",
    )(h0, g_attn[0].reshape(1, d), tabs[0], tabs[0], w_down_ext, g_mla_q_lora, g_mla_kv_lora,
      g_mla_kr, _swap_halves(g_mla_kr), cos_tab, sin_tab)

    wq = w_mla_uq[0]
    w_q_cat = jnp.concatenate([wq, _rot_cols(wq[..., nope:])], axis=-1).reshape(q_lora, heads * LANES).astype(BF16)
    gq_cat = jnp.concatenate([g_mla_qn[0], g_mla_qr[0], _swap_halves(g_mla_qr[0])]).reshape(1, LANES)
    q_tab = jnp.concatenate([jnp.ones((cos_tab.shape[0], nope), F32), cos_tab, sin_tab], axis=1)
    mla_scale = (nope + rope) ** -0.5
    q_cat = pl.pallas_call(
        functools.partial(_q_up_kernel, heads=heads, nope=nope, rope=rope, scale=mla_scale),
        grid=(n_tok_blocks,),
        in_specs=[pl.BlockSpec((TM, q_lora), tok), pl.BlockSpec((q_lora, heads * LANES), const2),
                  pl.BlockSpec((1, LANES), const2), pl.BlockSpec((TM, LANES), pos_blk)],
        out_specs=pl.BlockSpec((heads, TM, LANES), lambda i: (0, i, 0)),
        out_shape=jax.ShapeDtypeStruct((heads, m_rows, LANES), BF16),
        compiler_params=_params("parallel"),
        name="q_up",
    )(cq, w_q_cat, gq_cat, q_tab)

    wuk = w_mla_uk[0]
    wuv = w_mla_uv[0]
    wuk_t = wuk.reshape(kv_lora, heads * nope).T.astype(BF16)
    tk = TM
    nk = seq // tk
    kt, v_p = pl.pallas_call(
        functools.partial(_kv_up_kernel, heads=heads, nope=nope, rope=rope),
        grid=(n_prompt_blocks,),
        in_specs=[pl.BlockSpec((TM, kv_lora), tok), pl.BlockSpec((TM, rope), tok),
                  pl.BlockSpec((heads * nope, kv_lora), const2), pl.BlockSpec((nope, 1), const2),
                  pl.BlockSpec((kv_lora, heads * vdim), const2), pl.BlockSpec((rope, rope), const2)],
        out_specs=[pl.BlockSpec((1, heads, 1, LANES, tk), lambda i: (i // nk, 0, i % nk, 0, 0)),
                   pl.BlockSpec((TM, heads * vdim), tok)],
        out_shape=[jax.ShapeDtypeStruct((nbp, heads, nk, LANES, tk), BF16),
                   jax.ShapeDtypeStruct((mp, heads * vdim), BF16)],
        compiler_params=_params("parallel"),
        name="kv_up",
    )(ckv, kr, wuk_t, g_mla_kn[0].reshape(nope, 1), wuv.reshape(kv_lora, heads * vdim).astype(BF16),
      jnp.eye(rope, dtype=BF16))

    tq = 256
    nq = seq // tq
    att_p = pl.pallas_call(
        functools.partial(_mla_flash_kernel, tq=tq, tk=tk, vdim=vdim),
        grid=(nbp, heads // 2, nq),
        in_specs=[pl.BlockSpec((2, tq, LANES), lambda b, hp, qi: (hp, b * nq + qi, 0)),
                  pl.BlockSpec((1, 2, nk, LANES, tk), lambda b, hp, qi: (b, hp, 0, 0, 0)),
                  pl.BlockSpec((seq, LANES), lambda b, hp, qi: (b, hp))],
        out_specs=pl.BlockSpec((tq, LANES), lambda b, hp, qi: (b * nq + qi, hp)),
        out_shape=jax.ShapeDtypeStruct((mp, heads * vdim), BF16),
        compiler_params=_params("parallel", "parallel", "arbitrary"),
        name="mla_prompt_attention",
    )(q_cat, kt, v_p)

    sample_blk = mp // ms
    qa, qr = pl.pallas_call(
        functools.partial(_q_absorb_kernel, heads=heads, nope=nope, rope=rope),
        grid=(1,),
        in_specs=[pl.BlockSpec((heads, ms, LANES), lambda i: (0, sample_blk, 0)),
                  pl.BlockSpec((heads, nope, kv_lora), lambda i: (0, 0, 0)),
                  pl.BlockSpec((1, nope), const2)],
        out_specs=[pl.BlockSpec((heads, ms, kv_lora), lambda i: (0, 0, 0)),
                   pl.BlockSpec((heads, ms, rope), lambda i: (0, 0, 0))],
        out_shape=[jax.ShapeDtypeStruct((heads, ms, kv_lora), BF16),
                   jax.ShapeDtypeStruct((heads, ms, rope), BF16)],
        compiler_params=_params("arbitrary"),
        name="q_absorb",
    )(q_cat, wuk.transpose(1, 2, 0).astype(BF16), g_mla_kn)
    rows = t_new * heads

    def per_batch(x):
        return x.reshape(heads, t_new, nbs, -1).transpose(2, 1, 0, 3).reshape(nbs, rows, -1)

    def new_rows(x):
        x = x.reshape(t_new, nbs, -1).transpose(1, 0, 2)
        return jnp.pad(x, ((0, 0), (0, 8 - t_new), (0, 0)))

    n_chunks = n_pages // SAMPLE_PAGES
    o_lat = pl.pallas_call(
        functools.partial(_mla_sample_kernel, heads=heads, nope=nope, n_chunks=n_chunks, t_new=t_new),
        grid_spec=pltpu.PrefetchScalarGridSpec(
            num_scalar_prefetch=1,
            grid=(nbs,),
            in_specs=[pl.BlockSpec((1, rows, kv_lora), lambda n, pt: (n, 0, 0)),
                      pl.BlockSpec((1, rows, rope), lambda n, pt: (n, 0, 0)),
                      pl.BlockSpec((heads * nope, kv_lora), lambda n, pt: (0, 0)),
                      pl.BlockSpec((1, 8, kv_lora), lambda n, pt: (n, 0, 0)),
                      pl.BlockSpec((1, 8, rope), lambda n, pt: (n, 0, 0)),
                      pl.BlockSpec(memory_space=pl.ANY),
                      pl.BlockSpec(memory_space=pl.ANY)],
            out_specs=pl.BlockSpec((1, rows, kv_lora), lambda n, pt: (n, 0, 0)),
            scratch_shapes=[pltpu.VMEM((2, SAMPLE_PAGES, PAGE, kv_lora), F32),
                            pltpu.VMEM((2, SAMPLE_PAGES, PAGE, rope), F32),
                            pltpu.SemaphoreType.DMA((2, 2))]),
        out_shape=jax.ShapeDtypeStruct((nbs, rows, kv_lora), F32),
        compiler_params=_params("arbitrary"),
        name="mla_sample_attention",
    )(page_table, per_batch(qa), per_batch(qr), wuk_t, new_rows(ckv[mp:]), new_rows(kr[mp:]),
      cache_mla_latent[0], cache_mla_krope[0])
    o_lat = o_lat.reshape(nbs, t_new, heads, kv_lora).transpose(2, 1, 0, 3).reshape(heads, ms, kv_lora)
    att_s = pl.pallas_call(
        functools.partial(_unabsorb_kernel, heads=heads),
        grid=(1,),
        in_specs=[pl.BlockSpec((heads, ms, kv_lora), lambda i: (0, 0, 0)),
                  pl.BlockSpec((heads, kv_lora, vdim), lambda i: (0, 0, 0))],
        out_specs=pl.BlockSpec((ms, heads * vdim), const2),
        out_shape=jax.ShapeDtypeStruct((ms, heads * vdim), BF16),
        compiler_params=_params("arbitrary"),
        name="unabsorb",
    )(o_lat.astype(BF16), wuv.transpose(1, 0, 2).astype(BF16))

    def oproj(att, w, h, tab):
        return pl.pallas_call(
            _oproj_kernel,
            grid=(n_tok_blocks,),
            in_specs=[pl.BlockSpec((TM, att.shape[1]), tok), pl.BlockSpec(w.shape, const2),
                      pl.BlockSpec((TM, d), tok),
                      pl.BlockSpec((None, nbs, d), lambda i: (tab_idx(i), 0, 2))],
            out_specs=pl.BlockSpec((TM, d), tok),
            out_shape=jax.ShapeDtypeStruct((m_rows, d), F32),
            compiler_params=_params("parallel"),
            name="oproj",
        )(att, w.astype(BF16), h, tab)

    h = oproj(jnp.concatenate([att_p, att_s], axis=0), w_mla_o[0], h0, tabs[0])
    h1 = _moe(h, tabs[0], 3, g_ffn[0], w_router[0], b_router[0], w_up[0], b_up[0], w_down[0], b_down[0],
              seq=seq, nb_prompt=nbp)

    kw = kvh * hd
    k_all, v_all, k_hm, v_hm = pl.pallas_call(
        functools.partial(_shared_kv_kernel, kvh=kvh, hd=hd),
        grid=(n_tok_blocks,),
        in_specs=[pl.BlockSpec((TM, d), tok), pl.BlockSpec((1, d), const2), tab_spec(0), tab_spec(1),
                  pl.BlockSpec((d, 2 * kw), const2), pl.BlockSpec((1, kw), const2)],
        out_specs=[pl.BlockSpec((TM, kw), tok), pl.BlockSpec((TM, kw), tok),
                   pl.BlockSpec((kvh, TM, hd), lambda i: (0, i, 0)),
                   pl.BlockSpec((kvh, TM, hd), lambda i: (0, i, 0))],
        out_shape=[jax.ShapeDtypeStruct((m_rows, kw), F32), jax.ShapeDtypeStruct((m_rows, kw), F32),
                   jax.ShapeDtypeStruct((kvh, m_rows, hd), BF16),
                   jax.ShapeDtypeStruct((kvh, m_rows, hd), BF16)],
        compiler_params=_params("parallel"),
        name="shared_kv",
    )(h1, g_kv_norm.reshape(1, d), tab_kv, tab_kv, w_kv.astype(BF16), jnp.tile(g_swa_k, kvh).reshape(1, kw))

    swa_scale = hd ** -0.5
    q_hm = pl.pallas_call(
        functools.partial(_swa_q_kernel, heads=swa_heads, hd=hd, scale=swa_scale),
        grid=(n_tok_blocks,),
        in_specs=[pl.BlockSpec((TM, d), tok), pl.BlockSpec((1, d), const2), tab_spec(0), tab_spec(1),
                  pl.BlockSpec((d, swa_heads * hd), const2), pl.BlockSpec((1, swa_heads * hd), const2)],
        out_specs=pl.BlockSpec((swa_heads, TM, hd), lambda i: (0, i, 0)),
        out_shape=jax.ShapeDtypeStruct((swa_heads, m_rows, hd), BF16),
        compiler_params=_params("parallel"),
        name="swa_q",
    )(h1, g_attn[1].reshape(1, d), tabs[1], tabs[1], w_swa_q[0].astype(BF16),
      jnp.tile(g_swa_q[0], swa_heads).reshape(1, swa_heads * hd))

    def swa(q, kp, kc, vp, vc, tq_, n_seq, blocks, prev_idx, first_block_has_no_prev):
        nrows = n_seq * blocks * tq_
        return pl.pallas_call(
            functools.partial(_swa_kernel, heads=swa_heads, kvh=kvh, tq=tq_,
                              first_block_has_no_prev=first_block_has_no_prev),
            grid=(n_seq, blocks),
            in_specs=[pl.BlockSpec(memory_space=pltpu.SMEM),
                      pl.BlockSpec((swa_heads, tq_, hd), lambda b, i: (0, b * blocks + i, 0)),
                      pl.BlockSpec((kvh, WINDOW, hd), prev_idx),
                      pl.BlockSpec((kvh, tq_, hd), lambda b, i: (0, b * blocks + i, 0)),
                      pl.BlockSpec((kvh, WINDOW, hd), prev_idx),
                      pl.BlockSpec((kvh, tq_, hd), lambda b, i: (0, b * blocks + i, 0))],
            out_specs=pl.BlockSpec((tq_, swa_heads * hd), lambda b, i: (b * blocks + i, 0)),
            out_shape=jax.ShapeDtypeStruct((nrows, swa_heads * hd), BF16),
            compiler_params=_params("parallel", "arbitrary"),
            name="swa_attention",
        )(swa_sinks[0], q, kp, kc, vp, vc)

    wblocks = seq // WINDOW
    att_p = swa(q_hm, k_hm, k_hm, v_hm, v_hm, WINDOW, nbp, wblocks,
                lambda b, i: (0, jnp.maximum(b * wblocks + i - 1, 0), 0), True)

    tqs = 16

    def pad_steps(x):
        x = x[:, mp:].reshape(x.shape[0], t_new, nbs, hd).transpose(0, 2, 1, 3)
        return jnp.pad(x, ((0, 0), (0, 0), (0, tqs - t_new), (0, 0))).reshape(x.shape[0], nbs * tqs, hd)

    def win_hm(x):
        return x.transpose(2, 0, 1, 3).reshape(kvh, nbs * WINDOW, hd).astype(BF16)

    att_s = swa(pad_steps(q_hm), win_hm(state_win_k), pad_steps(k_hm), win_hm(state_win_v), pad_steps(v_hm),
                tqs, nbs, 1, lambda b, i: (0, b, 0), False)
    att_s = att_s.reshape(nbs, tqs, swa_heads * hd)[:, :t_new].transpose(1, 0, 2).reshape(ms, swa_heads * hd)

    h = oproj(jnp.concatenate([att_p[:mp], att_s], axis=0), w_swa_o[0], h1, tabs[1])
    h2 = _moe(h, tabs[1], 3, g_ffn[1], w_router[1], b_router[1], w_up[1], b_up[1], w_down[1], b_down[1],
              seq=seq, nb_prompt=nbp)

    def sample_major(x):
        return x.reshape(t_new, nbs, -1).transpose(1, 0, 2)

    y_prompt = h2[:mp].reshape(nbp, seq, d)
    y_sample = sample_major(h2[mp:])
    lat_p = ckv[:mp].reshape(1, nbp, seq, kv_lora)
    kr_p = kr[:mp].reshape(1, nbp, seq, rope)
    lat_s = sample_major(ckv[mp:])[None]
    kr_s = sample_major(kr[mp:])[None]
    k_p = k_all[:mp].reshape(nbp, seq, kvh, hd)
    v_p4 = v_all[:mp].reshape(nbp, seq, kvh, hd)
    k_n = sample_major(k_all[mp:]).reshape(nbs, t_new, kvh, hd)
    v_n = sample_major(v_all[mp:]).reshape(nbs, t_new, kvh, hd)
    win_k_s = jnp.concatenate([state_win_k, k_n], axis=1)[:, -WINDOW:]
    win_v_s = jnp.concatenate([state_win_v, v_n], axis=1)[:, -WINDOW:]
    return (y_prompt, y_sample, lat_p, kr_p, lat_s, kr_s,
            k_p[:, -WINDOW:], v_p4[:, -WINDOW:], win_k_s, win_v_s)
```

```python
import functools

import jax
import jax.numpy as jnp
from jax import lax
from jax.experimental import pallas as pl
from jax.experimental.pallas import tpu as pltpu

F32 = jnp.float32
BF16 = jnp.bfloat16

RMS_EPS = 1e-6
NEG_INF = -1e30
ROPE_THETA = 10000.0
PAGE = 128
WINDOW = 128
TOP_K = 4
SWIGLU_ALPHA = 1.702
SWIGLU_LIMIT = 7.0
LANES = 128
VMEM_LIMIT = 56 * 1024 * 1024

TM = 512
MOE_TM = 256
SAMPLE_PAGES = 16
SAMPLE_SUB = 256


def _nn(a, b):
    return jnp.dot(a, b, preferred_element_type=F32)


def _nt(a, b):
    return lax.dot_general(a, b, (((1,), (1,)), ((), ())), preferred_element_type=F32)


def _params(*sem):
    return pltpu.CompilerParams(dimension_semantics=sem, vmem_limit_bytes=VMEM_LIMIT)


def _norm_mod(h, g, shift, scale):
    tm, d = h.shape
    nb = shift.shape[0]
    y = h * lax.rsqrt(jnp.mean(h * h, axis=-1, keepdims=True) + RMS_EPS) * g
    y = y.reshape(tm // nb, nb, d) * (1.0 + scale[None]) + shift[None]
    return y.reshape(tm, d)


def _group64_rscale(x):
    lane = lax.broadcasted_iota(jnp.int32, x.shape, 1)
    x2 = x * x
    lo = jnp.sum(jnp.where(lane < 64, x2, 0.0), axis=-1, keepdims=True)
    hi = jnp.sum(jnp.where(lane >= 64, x2, 0.0), axis=-1, keepdims=True)
    return jnp.where(lane < 64, lax.rsqrt(lo / 64.0 + RMS_EPS), lax.rsqrt(hi / 64.0 + RMS_EPS))


def _adaln_kernel(c_ref, w_ref, b_ref, o_ref, *, nb_prompt):
    c = c_ref[...]
    a = (c * jax.nn.sigmoid(c)).astype(BF16)
    res = _nn(a, w_ref[...].astype(BF16)) + b_ref[...]
    nb = res.shape[0] - nb_prompt
    for b in range(nb_prompt):
        o_ref[b] = jnp.broadcast_to(res[b:b + 1], (nb, res.shape[1]))
    o_ref[nb_prompt] = res[nb_prompt:]


def _adaln_table(c, w, b, nb_prompt):
    n, d = c.shape
    nl, _, nout = w.shape
    nb = n - nb_prompt
    tn = 1024
    return pl.pallas_call(
        functools.partial(_adaln_kernel, nb_prompt=nb_prompt),
        grid=(nl, nout // tn),
        in_specs=[pl.BlockSpec((n, d), lambda l, j: (0, 0)),
                  pl.BlockSpec((None, d, tn), lambda l, j: (l, 0, j)),
                  pl.BlockSpec((None, 1, tn), lambda l, j: (l, 0, j))],
        out_specs=pl.BlockSpec((None, nb_prompt + 1, nb, tn), lambda l, j: (l, 0, 0, j)),
        out_shape=jax.ShapeDtypeStruct((nl, nb_prompt + 1, nb, nout), F32),
        compiler_params=_params("parallel", "parallel"),
        name="adaln",
    )(c, w, b.reshape(nl, 1, nout))


def _mla_down_kernel(h_ref, g_ref, sh_ref, sc_ref, w_ref, gql_ref, gkvl_ref, gkr_ref, gkrp_ref,
                     cos_ref, sin_ref, cq_ref, ckv_ref, kr_ref, *, q_lora, kv_lora, rope):
    u = _norm_mod(h_ref[...], g_ref[...], sh_ref[...], sc_ref[...]).astype(BF16)
    a = _nn(u, w_ref[...])
    q = a[:, :q_lora]
    cq_ref[...] = (q * lax.rsqrt(jnp.mean(q * q, axis=-1, keepdims=True) + RMS_EPS)
                   * gql_ref[...]).astype(BF16)
    c = a[:, q_lora:q_lora + kv_lora]
    ckv_ref[...] = c * lax.rsqrt(jnp.mean(c * c, axis=-1, keepdims=True) + RMS_EPS) * gkvl_ref[...]
    o = q_lora + kv_lora
    raw = a[:, o:o + rope]
    rot = a[:, o + rope:o + 2 * rope]
    r = lax.rsqrt(jnp.mean(raw * raw, axis=-1, keepdims=True) + RMS_EPS)
    kr_ref[...] = r * (raw * gkr_ref[...] * cos_ref[...] + rot * gkrp_ref[...] * sin_ref[...])


def _q_up_kernel(cq_ref, w_ref, gq_ref, tab_ref, o_ref, *, heads, nope, rope, scale):
    a = _nn(cq_ref[...], w_ref[...])
    tab = tab_ref[...]
    gq = gq_ref[...]
    for h in range(heads):
        x = a[:, h * LANES:(h + 1) * LANES]
        lane = lax.broadcasted_iota(jnp.int32, x.shape, 1)
        x2 = x * x
        ssn = jnp.sum(jnp.where(lane < nope, x2, 0.0), axis=-1, keepdims=True)
        ssr = jnp.sum(jnp.where((lane >= nope) & (lane < nope + rope), x2, 0.0), axis=-1, keepdims=True)
        r = jnp.where(lane < nope, lax.rsqrt(ssn / nope + RMS_EPS), lax.rsqrt(ssr / rope + RMS_EPS))
        o_ref[h] = (x * r * gq * tab * scale).astype(BF16)


def _kv_up_kernel(ckv_ref, kr_ref, wukt_ref, gkn_ref, wuv_ref, eye_ref, kt_ref, v_ref, *, heads, nope, rope):
    c = ckv_ref[...].astype(BF16)
    tm = c.shape[0]
    knt = _nt(wukt_ref[...], c).reshape(heads, nope, tm)
    ss = jnp.sum(knt * knt, axis=1, keepdims=True)
    kn = knt * lax.rsqrt(ss / nope + RMS_EPS) * gkn_ref[...][None]
    krt = _nt(eye_ref[...], kr_ref[...].astype(BF16)).astype(BF16)
    krt = jnp.broadcast_to(krt[None], (heads, rope, tm))
    kt_ref[0, :, 0, 0:nope, :] = kn.astype(BF16)
    kt_ref[0, :, 0, nope:nope + rope, :] = krt
    kt_ref[0, :, 0, nope + rope:nope + 2 * rope, :] = krt
    v_ref[...] = _nn(c, wuv_ref[...]).astype(BF16)


def _mla_flash_kernel(q_ref, kt_ref, v_ref, o_ref, *, tq, tk, vdim):
    qi = pl.program_id(2)
    n_full = (qi * tq) // tk
    qs = [q_ref[0], q_ref[1]]

    def step(j, carry, masked):
        start = pl.multiple_of(j * tk, tk)
        v = v_ref[pl.ds(start, tk), :]
        out = []
        for hh in range(2):
            m, l, acc = carry[hh]
            s = _nn(qs[hh], kt_ref[0, hh, j])
            if masked:
                row = qi * tq + lax.broadcasted_iota(jnp.int32, s.shape, 0)
                col = j * tk + lax.broadcasted_iota(jnp.int32, s.shape, 1)
                s = jnp.where(col <= row, s, NEG_INF)
            m_new = jnp.maximum(m, jnp.max(s, axis=-1, keepdims=True))
            alpha = jnp.exp(m - m_new)
            p = jnp.exp(s - m_new)
            l = alpha * l + jnp.sum(p, axis=-1, keepdims=True)
            acc = alpha * acc + _nn(p.astype(BF16), v)
            out.append((m_new, l, acc))
        return tuple(out)

    init = tuple((jnp.full((tq, 1), NEG_INF, F32), jnp.zeros((tq, 1), F32), jnp.zeros((tq, LANES), F32))
                 for _ in range(2))
    carry = lax.fori_loop(0, n_full, lambda j, c: step(j, c, False), init)
    (_, l0, acc0), (_, l1, acc1) = step(n_full, carry, True)
    lane = lax.broadcasted_iota(jnp.int32, (tq, LANES), 1)
    o_ref[...] = jnp.where(lane < vdim, acc0 / l0, acc1 / l1).astype(BF16)


def _q_absorb_kernel(q_ref, wuk_ref, gkn_ref, qa_ref, qr_ref, *, heads, nope, rope):
    for h in range(heads):
        q = q_ref[h].astype(F32)
        qg = (q[:, :nope] * gkn_ref[...]).astype(BF16)
        qa_ref[h] = _nn(qg, wuk_ref[h]).astype(BF16)
        qr_ref[h] = (q[:, nope:nope + rope] + q[:, nope + rope:nope + 2 * rope]).astype(BF16)


def _mla_sample_kernel(pt_ref, qa_ref, qr_ref, wukt_ref, cnew_ref, rnew_ref, poolc_ref, poolrt_ref,
                       o_ref, cbuf, rbuf, cb, sc, sems, *, heads, nope, n_chunks, n_batch, t_new):
    n = pl.program_id(0)
    cp = cbuf.shape[1]
    rows = qa_ref.shape[1]
    lat = qa_ref.shape[2]
    nw = heads * nope
    total = n_batch * n_chunks
    ppt = SAMPLE_SUB // PAGE

    def copies(b, ci, slot):
        out = []
        for p in range(cp):
            page = pt_ref[b, ci * cp + p]
            out.append(pltpu.make_async_copy(poolc_ref.at[page], cbuf.at[slot, p], sems.at[0, slot]))
            out.append(pltpu.make_async_copy(poolrt_ref.at[page], rbuf.at[slot, p], sems.at[1, slot]))
        return out

    @pl.when(n == 0)
    def _():
        for cpy in copies(0, 0, 0):
            cpy.start()

    lhs = jnp.concatenate([wukt_ref[...], qa_ref[0]], axis=0)
    qr = qr_ref[0]

    def scores(c, rope_scores):
        tk = c.shape[0]
        big = _nt(lhs, c)
        knt = big[:nw].reshape(heads, nope, tk)
        r = lax.rsqrt(jnp.sum(knt * knt, axis=1) / nope + RMS_EPS)
        return big[nw:] * jnp.concatenate([r] * t_new, axis=0) + rope_scores

    def update(s, c, carry):
        m, l, acc = carry
        m_new = jnp.maximum(m, jnp.max(s, axis=-1, keepdims=True))
        alpha = jnp.exp(m - m_new)
        p = jnp.exp(s - m_new)
        l = alpha * l + jnp.sum(p, axis=-1, keepdims=True)
        acc = alpha * acc + _nn(p.astype(BF16), c)
        return m_new, l, acc

    def chunk(ci, carry):
        g = n * n_chunks + ci
        slot = g % 2
        for cpy in copies(n, ci, slot):
            cpy.wait()
        nxt = jnp.minimum(g + 1, total - 1)
        for cpy in copies(nxt // n_chunks, nxt % n_chunks, 1 - slot):
            cpy.start()
        for j in range(cp // ppt):
            c = cbuf[slot, j * ppt:(j + 1) * ppt].reshape(SAMPLE_SUB, lat).astype(BF16)
            krt = jnp.concatenate([rbuf[slot, j * ppt + p] for p in range(ppt)], axis=1).astype(BF16)
            cb[j * SAMPLE_SUB:(j + 1) * SAMPLE_SUB, :] = c
            sc[:, j * SAMPLE_SUB:(j + 1) * SAMPLE_SUB] = scores(c, _nn(qr, krt))
        return update(sc[...], cb[...], carry)

    init = (jnp.full((rows, 1), NEG_INF, F32), jnp.zeros((rows, 1), F32), jnp.zeros((rows, lat), F32))
    carry = lax.fori_loop(0, n_chunks, chunk, init)

    @pl.when(n == n_batch - 1)
    def _():
        for cpy in copies(n, n_chunks - 1, total % 2):
            cpy.wait()

    c = cnew_ref[0].astype(BF16)
    tk = c.shape[0]
    t_of_row = lax.broadcasted_iota(jnp.int32, (rows, tk), 0) // heads
    col = lax.broadcasted_iota(jnp.int32, (rows, tk), 1)
    s = jnp.where(col <= t_of_row, scores(c, _nt(qr, rnew_ref[0].astype(BF16))), NEG_INF)
    m, l, acc = update(s, c, carry)
    o_ref[0] = acc / l


def _unabsorb_kernel(o_ref, wuv_ref, out_ref, *, heads):
    out_ref[...] = jnp.concatenate(
        [_nn(o_ref[h], wuv_ref[h]) for h in range(heads)], axis=-1).astype(BF16)


def _oproj_kernel(att_ref, w_ref, h_ref, gate_ref, o_ref):
    y = _nn(att_ref[...], w_ref[...])
    tm, d = y.shape
    gate = gate_ref[...]
    nb = gate.shape[0]
    o_ref[...] = h_ref[...] + (y.reshape(tm // nb, nb, d) * gate[None]).reshape(tm, d)


def _router_kernel(h_ref, g_ref, sh_ref, sc_ref, whi_ref, wlo_ref, b_ref, u_ref, idx_ref, gate_ref):
    u = _norm_mod(h_ref[...], g_ref[...], sh_ref[...], sc_ref[...])
    uhi = u.astype(BF16)
    ulo = (u - uhi.astype(F32)).astype(BF16)
    u_ref[...] = u
    whi = whi_ref[...]
    logits = _nt(whi, uhi) + _nt(whi, ulo) + _nt(wlo_ref[...], uhi) + b_ref[...]
    ne = logits.shape[0]
    eid = lax.broadcasted_iota(jnp.int32, logits.shape, 0)
    work = logits
    vals, idxs = [], []
    for _ in range(TOP_K):
        m = jnp.max(work, axis=0, keepdims=True)
        idx = jnp.min(jnp.where(work == m, eid, ne), axis=0, keepdims=True)
        vals.append(m)
        idxs.append(idx)
        work = jnp.where(eid == idx, -jnp.inf, work)
    es = [jnp.exp(v - vals[0]) for v in vals]
    den = es[0] + es[1] + es[2] + es[3]
    idx_ref[...] = jnp.concatenate(idxs, axis=0)
    gate_ref[...] = jnp.concatenate([e / den for e in es], axis=0)


def _expert_kernel(be_ref, nused_ref, x_ref, wup_ref, bup_ref, wdn_ref, bdn_ref, y_ref, wup_bf, wdn_bf, *, ff):
    i = pl.program_id(0)
    prev = be_ref[jnp.maximum(i - 1, 0)]
    new_expert = (i == 0) | (be_ref[i] != prev)

    @pl.when(new_expert & (i < nused_ref[0]))
    def _():
        wup_bf[...] = wup_ref[0].astype(BF16)
        wdn_bf[...] = wdn_ref[0].astype(BF16)

    @pl.when(i < nused_ref[0])
    def _():
        x = x_ref[...].astype(BF16)
        fc = 512
        acc = jnp.zeros(y_ref.shape, F32)
        for c0 in range(0, ff, fc):
            glu = _nn(x, wup_bf[:, c0:c0 + fc]) + bup_ref[0, :, c0:c0 + fc]
            lin = _nn(x, wup_bf[:, ff + c0:ff + c0 + fc]) + bup_ref[0, :, ff + c0:ff + c0 + fc]
            glu = jnp.minimum(glu, SWIGLU_LIMIT)
            lin = jnp.clip(lin, -SWIGLU_LIMIT, SWIGLU_LIMIT)
            act = glu * jax.nn.sigmoid(SWIGLU_ALPHA * glu) * (lin + 1.0)
            acc = acc + _nn(act.astype(BF16), wdn_bf[c0:c0 + fc, :])
        y_ref[...] = acc + bdn_ref[0]

    @pl.when(i >= nused_ref[0])
    def _():
        y_ref[...] = jnp.zeros(y_ref.shape, F32)


def _combine_kernel(h_ref, y_ref, gk_ref, gate_ref, o_ref):
    gk = gk_ref[...]
    y = y_ref[0] * gk[:, 0:1]
    for k in range(1, TOP_K):
        y = y + y_ref[k] * gk[:, k:k + 1]
    tm, d = y.shape
    gate = gate_ref[...]
    nb = gate.shape[0]
    o_ref[...] = h_ref[...] + (y.reshape(tm // nb, nb, d) * gate[None]).reshape(tm, d)


def _shared_kv_kernel(h_ref, g_ref, sh_ref, sc_ref, w_ref, gk_ref, k_ref, v_ref, khm_ref, vhm_ref, *, kvh, hd):
    u = _norm_mod(h_ref[...], g_ref[...], sh_ref[...], sc_ref[...]).astype(BF16)
    a = _nn(u, w_ref[...])
    kw = kvh * hd
    gk = gk_ref[...]
    ks = []
    for j in range(kw // LANES):
        x = a[:, j * LANES:(j + 1) * LANES]
        ks.append(x * _group64_rscale(x) * gk[:, j * LANES:(j + 1) * LANES])
    k = jnp.concatenate(ks, axis=-1)
    v = a[:, kw:2 * kw]
    k_ref[...] = k
    v_ref[...] = v
    for hh in range(kvh):
        khm_ref[hh] = k[:, hh * hd:(hh + 1) * hd].astype(BF16)
        vhm_ref[hh] = v[:, hh * hd:(hh + 1) * hd].astype(BF16)


def _swa_q_kernel(h_ref, g_ref, sh_ref, sc_ref, w_ref, gq_ref, q_ref, *, heads, hd, scale):
    u = _norm_mod(h_ref[...], g_ref[...], sh_ref[...], sc_ref[...]).astype(BF16)
    a = _nn(u, w_ref[...])
    gq = gq_ref[...]
    for j in range(heads * hd // LANES):
        x = a[:, j * LANES:(j + 1) * LANES]
        y = x * _group64_rscale(x) * gq[:, j * LANES:(j + 1) * LANES] * scale
        q_ref[2 * j] = y[:, :hd].astype(BF16)
        q_ref[2 * j + 1] = y[:, hd:].astype(BF16)


def _swa_kernel(sink_ref, q_ref, kp_ref, kc_ref, vp_ref, vc_ref, o_ref, *, heads, kvh, tq, first_block_has_no_prev):
    group = heads // kvh
    nk = WINDOW + tq
    rows = group * tq
    rowi = lax.broadcasted_iota(jnp.int32, (rows, nk), 0)
    col = lax.broadcasted_iota(jnp.int32, (rows, nk), 1)
    t = rowi % tq
    dist = t + WINDOW - col
    valid = (dist >= 0) & (dist < WINDOW)
    if first_block_has_no_prev:
        valid = valid & ((col >= WINDOW) | (pl.program_id(1) > 0))
    distf = dist.astype(F32)
    hrow = lax.broadcasted_iota(jnp.int32, (rows, 1), 0) // tq
    outs = []
    for g in range(kvh):
        k = jnp.concatenate([kp_ref[g], kc_ref[g]], axis=0)
        v = jnp.concatenate([vp_ref[g], vc_ref[g]], axis=0)
        q = q_ref[g * group:(g + 1) * group].reshape(rows, q_ref.shape[2])
        slope = jnp.zeros((rows, 1), F32)
        sink = jnp.zeros((rows, 1), F32)
        for j in range(group):
            h = g * group + j
            slope = jnp.where(hrow == j, 2.0 ** (-8.0 * (h + 1) / heads), slope)
            sink = jnp.where(hrow == j, sink_ref[h], sink)
        s = jnp.where(valid, _nt(q, k) - slope * distf, NEG_INF)
        m = jnp.maximum(jnp.max(s, axis=-1, keepdims=True), sink)
        e = jnp.exp(s - m)
        p = e / (jnp.sum(e, axis=-1, keepdims=True) + jnp.exp(sink - m))
        o = _nn(p.astype(BF16), v)
        for j in range(group):
            outs.append(o[j * tq:(j + 1) * tq])
    o_ref[...] = jnp.concatenate(outs, axis=-1).astype(BF16)


def _rope_tables(pos, rope):
    half = rope // 2
    inv = ROPE_THETA ** (-jnp.arange(half, dtype=F32) / half)
    ang = pos.astype(F32)[:, None] * inv[None, :]
    cos, sin = jnp.cos(ang), jnp.sin(ang)
    return jnp.concatenate([cos, cos], axis=-1), jnp.concatenate([sin, sin], axis=-1)


def _rot_cols(w):
    half = w.shape[-1] // 2
    return jnp.concatenate([-w[..., half:], w[..., :half]], axis=-1)


def _swap_halves(g):
    half = g.shape[-1] // 2
    return jnp.concatenate([g[..., half:], g[..., :half]], axis=-1)


def _moe(h, tab, layer, col0, g_ffn, w_router, b_router, w_up, b_up, w_down, b_down, *, seq, nb_prompt):
    m_rows, d = h.shape
    ne = w_router.shape[1]
    ff = w_down.shape[2]
    nb = tab.shape[2]
    n_tok_blocks = m_rows // TM

    def tab_spec(col):
        return pl.BlockSpec((None, None, nb, d),
                            lambda i: (layer, jnp.minimum(i * TM // seq, nb_prompt), 0, col))

    wr_t = w_router.T
    wr_hi = wr_t.astype(BF16)
    wr_lo = (wr_t - wr_hi.astype(F32)).astype(BF16)
    u, idx_t, gate_t = pl.pallas_call(
        _router_kernel,
        grid=(n_tok_blocks,),
        in_specs=[pl.BlockSpec((TM, d), lambda i: (i, 0)),
                  pl.BlockSpec((1, d), lambda i: (0, 0)),
                  tab_spec(col0), tab_spec(col0 + 1),
                  pl.BlockSpec((ne, d), lambda i: (0, 0)),
                  pl.BlockSpec((ne, d), lambda i: (0, 0)),
                  pl.BlockSpec((ne, 1), lambda i: (0, 0))],
        out_specs=[pl.BlockSpec((TM, d), lambda i: (i, 0)),
                   pl.BlockSpec((TOP_K, TM), lambda i: (0, i)),
                   pl.BlockSpec((TOP_K, TM), lambda i: (0, i))],
        out_shape=[jax.ShapeDtypeStruct((m_rows, d), F32),
                   jax.ShapeDtypeStruct((TOP_K, m_rows), jnp.int32),
                   jax.ShapeDtypeStruct((TOP_K, m_rows), F32)],
        compiler_params=_params("parallel"),
        name="router",
    )(h, g_ffn.reshape(1, d), tab, tab, wr_hi, wr_lo, b_router.reshape(ne, 1))

    a = m_rows * TOP_K
    e_flat = idx_t.T.reshape(a)
    order = jnp.argsort(e_flat).astype(jnp.int32)
    rank = jnp.argsort(order).astype(jnp.int32)
    onehot = e_flat[:, None] == jnp.arange(ne, dtype=jnp.int32)[None, :]
    counts = jnp.sum(onehot, axis=0, dtype=jnp.int32)
    padded = (counts + MOE_TM - 1) // MOE_TM * MOE_TM
    pad_end = jnp.cumsum(padded)
    shift = (pad_end - padded) - (jnp.cumsum(counts) - counts)
    pos = rank + jnp.sum(jnp.where(onehot, shift[None, :], 0), axis=1)
    n_blk = -(-a // MOE_TM) + ne
    n_slots = n_blk * MOE_TM
    blk_start = jnp.arange(n_blk, dtype=jnp.int32) * MOE_TM
    blk_expert = jnp.minimum(jnp.sum(pad_end[None, :] <= blk_start[:, None], axis=1), ne - 1).astype(jnp.int32)
    n_used = (pad_end[-1] // MOE_TM).astype(jnp.int32).reshape(1)
    slot_sorted = jnp.arange(n_slots, dtype=jnp.int32) - jnp.repeat(shift[blk_expert], MOE_TM)
    slot_tok = order[jnp.clip(slot_sorted, 0, a - 1)] // TOP_K
    xs = jnp.take(u, slot_tok, axis=0, mode="clip")

    def live(i, nused):
        return jnp.minimum(i, nused[0] - 1)

    yb = pl.pallas_call(
        functools.partial(_expert_kernel, ff=ff),
        grid_spec=pltpu.PrefetchScalarGridSpec(
            num_scalar_prefetch=2,
            grid=(n_blk,),
            in_specs=[pl.BlockSpec((MOE_TM, d), lambda i, be, nu: (live(i, nu), 0)),
                      pl.BlockSpec((None, 1, d, 2 * ff), lambda i, be, nu: (layer, be[live(i, nu)], 0, 0)),
                      pl.BlockSpec((None, 1, 1, 2 * ff), lambda i, be, nu: (layer, be[live(i, nu)], 0, 0)),
                      pl.BlockSpec((None, 1, ff, d), lambda i, be, nu: (layer, be[live(i, nu)], 0, 0)),
                      pl.BlockSpec((None, 1, 1, d), lambda i, be, nu: (layer, be[live(i, nu)], 0, 0))],
            out_specs=pl.BlockSpec((MOE_TM, d), lambda i, be, nu: (i, 0)),
            scratch_shapes=[pltpu.VMEM((d, 2 * ff), BF16), pltpu.VMEM((ff, d), BF16)]),
        out_shape=jax.ShapeDtypeStruct((n_slots, d), F32),
        compiler_params=_params("arbitrary"),
        name="experts",
    )(blk_expert, n_used, xs, w_up, b_up.reshape(b_up.shape[0], ne, 1, 2 * ff), w_down,
      b_down.reshape(b_down.shape[0], ne, 1, d))

    ysel = jnp.take(yb, pos.reshape(m_rows, TOP_K).T.reshape(a), axis=0, mode="clip").reshape(TOP_K, m_rows, d)
    return pl.pallas_call(
        _combine_kernel,
        grid=(n_tok_blocks,),
        in_specs=[pl.BlockSpec((TM, d), lambda i: (i, 0)),
                  pl.BlockSpec((TOP_K, TM, d), lambda i: (0, i, 0)),
                  pl.BlockSpec((TM, TOP_K), lambda i: (i, 0)),
                  tab_spec(col0 + 2)],
        out_specs=pl.BlockSpec((TM, d), lambda i: (i, 0)),
        out_shape=jax.ShapeDtypeStruct((m_rows, d), F32),
        compiler_params=_params("parallel"),
        name="moe_combine",
    )(h, ysel, gate_t.T, tab)


def kernel(x_prompt, x_sample, c_prompt, c_sample, cache_mla_latent, cache_mla_krope, state_win_k, state_win_v, page_table, w_mod, b_mod, g_attn, g_ffn, w_mla_down, g_mla_q_lora, g_mla_kv_lora, w_mla_uq, w_mla_uk, w_mla_uv, g_mla_qn, g_mla_qr, g_mla_kn, g_mla_kr, w_mla_o, w_kvmod, b_kvmod, g_kv_norm, w_kv, g_swa_k, w_swa_q, g_swa_q, swa_sinks, w_swa_o, w_router, b_router, w_up, b_up, w_down, b_down):
    nbp, seq, d = x_prompt.shape
    nbs, t_new, _ = x_sample.shape
    q_lora = g_mla_q_lora.shape[1]
    kv_lora = g_mla_kv_lora.shape[1]
    heads, nope = w_mla_uk.shape[2], w_mla_uk.shape[3]
    rope = g_mla_qr.shape[1]
    vdim = w_mla_uv.shape[3]
    n_pages = page_table.shape[1]
    past = n_pages * PAGE
    swa_heads = swa_sinks.shape[1]
    hd = g_swa_k.shape[0]
    kvh = w_kv.shape[1] // (2 * hd)
    assert nope + 2 * rope == LANES and 2 * vdim == LANES and 2 * hd == LANES
    assert seq % TM == 0 and TM % nbs == 0 and (nbs * t_new) % TM == 0 and nbs % 8 == 0
    assert w_mod.shape[0] == 2 and n_pages % SAMPLE_PAGES == 0

    mp = nbp * seq
    ms = nbs * t_new
    m_rows = mp + ms
    n_tok_blocks = m_rows // TM
    n_prompt_blocks = mp // TM
    blocks_per_seq = seq // TM

    def tab_idx(i):
        return jnp.minimum(i * TM // seq, nbp)

    def tok(i):
        return (i, 0)

    def const2(i):
        return (0, 0)

    h0 = jnp.concatenate([x_prompt.reshape(mp, d), x_sample.transpose(1, 0, 2).reshape(ms, d)], axis=0)
    c_all = jnp.concatenate([c_prompt, c_sample], axis=0)
    tab = _adaln_table(c_all, w_mod, b_mod, nbp)
    tab_kv = _adaln_table(c_all, w_kvmod[None], b_kvmod[None], nbp)

    def tab_spec(layer, col):
        return pl.BlockSpec((None, None, nbs, d), lambda i: (layer, tab_idx(i), 0, col))

    cos_p, sin_p = _rope_tables(jnp.arange(seq), rope)
    cos_s, sin_s = _rope_tables(past + jnp.arange(t_new), rope)
    cos_tab = jnp.concatenate([cos_p, jnp.repeat(cos_s, nbs, axis=0)], axis=0)
    sin_tab = jnp.concatenate([sin_p, jnp.repeat(sin_s, nbs, axis=0)], axis=0)

    def pos_blk(i):
        return (jnp.where(i < n_prompt_blocks, i % blocks_per_seq, blocks_per_seq + i - n_prompt_blocks), 0)

    wd = w_mla_down[0]
    w_down_ext = jnp.concatenate([wd, _rot_cols(wd[:, q_lora + kv_lora:])], axis=1).astype(BF16)
    nd = w_down_ext.shape[1]
    cq, ckv, kr = pl.pallas_call(
        functools.partial(_mla_down_kernel, q_lora=q_lora, kv_lora=kv_lora, rope=rope),
        grid=(n_tok_blocks,),
        in_specs=[pl.BlockSpec((TM, d), tok), pl.BlockSpec((1, d), const2), tab_spec(0, 0), tab_spec(0, 1),
                  pl.BlockSpec((d, nd), const2), pl.BlockSpec((1, q_lora), const2),
                  pl.BlockSpec((1, kv_lora), const2), pl.BlockSpec((1, rope), const2),
                  pl.BlockSpec((1, rope), const2), pl.BlockSpec((TM, rope), pos_blk),
                  pl.BlockSpec((TM, rope), pos_blk)],
        out_specs=[pl.BlockSpec((TM, q_lora), tok), pl.BlockSpec((TM, kv_lora), tok),
                   pl.BlockSpec((TM, rope), tok)],
        out_shape=[jax.ShapeDtypeStruct((m_rows, q_lora), BF16),
                   jax.ShapeDtypeStruct((m_rows, kv_lora), F32),
                   jax.ShapeDtypeStruct((m_rows, rope), F32)],
        compiler_params=_params("parallel"),
        name="mla_down",
    )(h0, g_attn[0].reshape(1, d), tab, tab, w_down_ext, g_mla_q_lora, g_mla_kv_lora,
      g_mla_kr, _swap_halves(g_mla_kr), cos_tab, sin_tab)

    wq = w_mla_uq[0]
    w_q_cat = jnp.concatenate([wq, _rot_cols(wq[..., nope:])], axis=-1).reshape(q_lora, heads * LANES).astype(BF16)
    gq_cat = jnp.concatenate([g_mla_qn[0], g_mla_qr[0], _swap_halves(g_mla_qr[0])]).reshape(1, LANES)
    q_tab = jnp.concatenate([jnp.ones((cos_tab.shape[0], nope), F32), cos_tab, sin_tab], axis=1)
    mla_scale = (nope + rope) ** -0.5
    q_cat = pl.pallas_call(
        functools.partial(_q_up_kernel, heads=heads, nope=nope, rope=rope, scale=mla_scale),
        grid=(n_tok_blocks,),
        in_specs=[pl.BlockSpec((TM, q_lora), tok), pl.BlockSpec((q_lora, heads * LANES), const2),
                  pl.BlockSpec((1, LANES), const2), pl.BlockSpec((TM, LANES), pos_blk)],
        out_specs=pl.BlockSpec((heads, TM, LANES), lambda i: (0, i, 0)),
        out_shape=jax.ShapeDtypeStruct((heads, m_rows, LANES), BF16),
        compiler_params=_params("parallel"),
        name="q_up",
    )(cq, w_q_cat, gq_cat, q_tab)

    wuk = w_mla_uk[0]
    wuv = w_mla_uv[0]
    wuk_t = wuk.reshape(kv_lora, heads * nope).T.astype(BF16)
    tk = TM
    nk = seq // tk
    kt, v_p = pl.pallas_call(
        functools.partial(_kv_up_kernel, heads=heads, nope=nope, rope=rope),
        grid=(n_prompt_blocks,),
        in_specs=[pl.BlockSpec((TM, kv_lora), tok), pl.BlockSpec((TM, rope), tok),
                  pl.BlockSpec((heads * nope, kv_lora), const2), pl.BlockSpec((nope, 1), const2),
                  pl.BlockSpec((kv_lora, heads * vdim), const2), pl.BlockSpec((rope, rope), const2)],
        out_specs=[pl.BlockSpec((1, heads, 1, LANES, tk), lambda i: (i // nk, 0, i % nk, 0, 0)),
                   pl.BlockSpec((TM, heads * vdim), tok)],
        out_shape=[jax.ShapeDtypeStruct((nbp, heads, nk, LANES, tk), BF16),
                   jax.ShapeDtypeStruct((mp, heads * vdim), BF16)],
        compiler_params=_params("parallel"),
        name="kv_up",
    )(ckv, kr, wuk_t, g_mla_kn[0].reshape(nope, 1), wuv.reshape(kv_lora, heads * vdim).astype(BF16),
      jnp.eye(rope, dtype=BF16))

    tq = 256
    nq = seq // tq
    att_p = pl.pallas_call(
        functools.partial(_mla_flash_kernel, tq=tq, tk=tk, vdim=vdim),
        grid=(nbp, heads // 2, nq),
        in_specs=[pl.BlockSpec((2, tq, LANES), lambda b, hp, qi: (hp, b * nq + qi, 0)),
                  pl.BlockSpec((1, 2, nk, LANES, tk), lambda b, hp, qi: (b, hp, 0, 0, 0)),
                  pl.BlockSpec((seq, LANES), lambda b, hp, qi: (b, hp))],
        out_specs=pl.BlockSpec((tq, LANES), lambda b, hp, qi: (b * nq + qi, hp)),
        out_shape=jax.ShapeDtypeStruct((mp, heads * vdim), BF16),
        compiler_params=_params("parallel", "parallel", "arbitrary"),
        name="mla_prompt_attention",
    )(q_cat, kt, v_p)

    sample_blk = mp // ms
    qa, qr = pl.pallas_call(
        functools.partial(_q_absorb_kernel, heads=heads, nope=nope, rope=rope),
        grid=(1,),
        in_specs=[pl.BlockSpec((heads, ms, LANES), lambda i: (0, sample_blk, 0)),
                  pl.BlockSpec((heads, nope, kv_lora), lambda i: (0, 0, 0)),
                  pl.BlockSpec((1, nope), const2)],
        out_specs=[pl.BlockSpec((heads, ms, kv_lora), lambda i: (0, 0, 0)),
                   pl.BlockSpec((heads, ms, rope), lambda i: (0, 0, 0))],
        out_shape=[jax.ShapeDtypeStruct((heads, ms, kv_lora), BF16),
                   jax.ShapeDtypeStruct((heads, ms, rope), BF16)],
        compiler_params=_params("arbitrary"),
        name="q_absorb",
    )(q_cat, wuk.transpose(1, 2, 0).astype(BF16), g_mla_kn)
    rows = t_new * heads

    def per_batch(x):
        return x.reshape(heads, t_new, nbs, -1).transpose(2, 1, 0, 3).reshape(nbs, rows, -1)

    def new_rows(x):
        x = x.reshape(t_new, nbs, -1).transpose(1, 0, 2)
        return jnp.pad(x, ((0, 0), (0, 8 - t_new), (0, 0)))

    n_chunks = n_pages // SAMPLE_PAGES
    sample_tokens = SAMPLE_PAGES * PAGE
    o_lat = pl.pallas_call(
        functools.partial(_mla_sample_kernel, heads=heads, nope=nope, n_chunks=n_chunks, n_batch=nbs,
                          t_new=t_new),
        grid_spec=pltpu.PrefetchScalarGridSpec(
            num_scalar_prefetch=1,
            grid=(nbs,),
            in_specs=[pl.BlockSpec((1, rows, kv_lora), lambda n, pt: (n, 0, 0)),
                      pl.BlockSpec((1, rows, rope), lambda n, pt: (n, 0, 0)),
                      pl.BlockSpec((heads * nope, kv_lora), lambda n, pt: (0, 0)),
                      pl.BlockSpec((1, 8, kv_lora), lambda n, pt: (n, 0, 0)),
                      pl.BlockSpec((1, 8, rope), lambda n, pt: (n, 0, 0)),
                      pl.BlockSpec(memory_space=pl.ANY),
                      pl.BlockSpec(memory_space=pl.ANY)],
            out_specs=pl.BlockSpec((1, rows, kv_lora), lambda n, pt: (n, 0, 0)),
            scratch_shapes=[pltpu.VMEM((2, SAMPLE_PAGES, PAGE, kv_lora), F32),
                            pltpu.VMEM((2, SAMPLE_PAGES, rope, PAGE), F32),
                            pltpu.VMEM((sample_tokens, kv_lora), BF16),
                            pltpu.VMEM((rows, sample_tokens), F32),
                            pltpu.SemaphoreType.DMA((2, 2))]),
        out_shape=jax.ShapeDtypeStruct((nbs, rows, kv_lora), F32),
        compiler_params=_params("arbitrary"),
        name="mla_sample_attention",
    )(page_table, per_batch(qa), per_batch(qr), wuk_t, new_rows(ckv[mp:]), new_rows(kr[mp:]),
      cache_mla_latent[0], cache_mla_krope[0].transpose(0, 2, 1))
    o_lat = o_lat.reshape(nbs, t_new, heads, kv_lora).transpose(2, 1, 0, 3).reshape(heads, ms, kv_lora)
    att_s = pl.pallas_call(
        functools.partial(_unabsorb_kernel, heads=heads),
        grid=(1,),
        in_specs=[pl.BlockSpec((heads, ms, kv_lora), lambda i: (0, 0, 0)),
                  pl.BlockSpec((heads, kv_lora, vdim), lambda i: (0, 0, 0))],
        out_specs=pl.BlockSpec((ms, heads * vdim), const2),
        out_shape=jax.ShapeDtypeStruct((ms, heads * vdim), BF16),
        compiler_params=_params("arbitrary"),
        name="unabsorb",
    )(o_lat.astype(BF16), wuv.transpose(1, 0, 2).astype(BF16))

    def oproj(att, w, h, layer):
        return pl.pallas_call(
            _oproj_kernel,
            grid=(n_tok_blocks,),
            in_specs=[pl.BlockSpec((TM, att.shape[1]), tok), pl.BlockSpec(w.shape, const2),
                      pl.BlockSpec((TM, d), tok), tab_spec(layer, 2)],
            out_specs=pl.BlockSpec((TM, d), tok),
            out_shape=jax.ShapeDtypeStruct((m_rows, d), F32),
            compiler_params=_params("parallel"),
            name="oproj",
        )(att, w.astype(BF16), h, tab)

    h = oproj(jnp.concatenate([att_p, att_s], axis=0), w_mla_o[0], h0, 0)
    h1 = _moe(h, tab, 0, 3, g_ffn[0], w_router[0], b_router[0], w_up, b_up, w_down, b_down,
              seq=seq, nb_prompt=nbp)

    kw = kvh * hd
    k_all, v_all, k_hm, v_hm = pl.pallas_call(
        functools.partial(_shared_kv_kernel, kvh=kvh, hd=hd),
        grid=(n_tok_blocks,),
        in_specs=[pl.BlockSpec((TM, d), tok), pl.BlockSpec((1, d), const2), tab_spec(0, 0), tab_spec(0, 1),
                  pl.BlockSpec((d, 2 * kw), const2), pl.BlockSpec((1, kw), const2)],
        out_specs=[pl.BlockSpec((TM, kw), tok), pl.BlockSpec((TM, kw), tok),
                   pl.BlockSpec((kvh, TM, hd), lambda i: (0, i, 0)),
                   pl.BlockSpec((kvh, TM, hd), lambda i: (0, i, 0))],
        out_shape=[jax.ShapeDtypeStruct((m_rows, kw), F32), jax.ShapeDtypeStruct((m_rows, kw), F32),
                   jax.ShapeDtypeStruct((kvh, m_rows, hd), BF16),
                   jax.ShapeDtypeStruct((kvh, m_rows, hd), BF16)],
        compiler_params=_params("parallel"),
        name="shared_kv",
    )(h1, g_kv_norm.reshape(1, d), tab_kv, tab_kv, w_kv.astype(BF16), jnp.tile(g_swa_k, kvh).reshape(1, kw))

    swa_scale = hd ** -0.5
    q_hm = pl.pallas_call(
        functools.partial(_swa_q_kernel, heads=swa_heads, hd=hd, scale=swa_scale),
        grid=(n_tok_blocks,),
        in_specs=[pl.BlockSpec((TM, d), tok), pl.BlockSpec((1, d), const2), tab_spec(1, 0), tab_spec(1, 1),
                  pl.BlockSpec((d, swa_heads * hd), const2), pl.BlockSpec((1, swa_heads * hd), const2)],
        out_specs=pl.BlockSpec((swa_heads, TM, hd), lambda i: (0, i, 0)),
        out_shape=jax.ShapeDtypeStruct((swa_heads, m_rows, hd), BF16),
        compiler_params=_params("parallel"),
        name="swa_q",
    )(h1, g_attn[1].reshape(1, d), tab, tab, w_swa_q[0].astype(BF16),
      jnp.tile(g_swa_q[0], swa_heads).reshape(1, swa_heads * hd))

    def swa(q, kp, kc, vp, vc, tq_, n_seq, blocks, prev_idx, first_block_has_no_prev):
        nrows = n_seq * blocks * tq_
        return pl.pallas_call(
            functools.partial(_swa_kernel, heads=swa_heads, kvh=kvh, tq=tq_,
                              first_block_has_no_prev=first_block_has_no_prev),
            grid=(n_seq, blocks),
            in_specs=[pl.BlockSpec(memory_space=pltpu.SMEM),
                      pl.BlockSpec((swa_heads, tq_, hd), lambda b, i: (0, b * blocks + i, 0)),
                      pl.BlockSpec((kvh, WINDOW, hd), prev_idx),
                      pl.BlockSpec((kvh, tq_, hd), lambda b, i: (0, b * blocks + i, 0)),
                      pl.BlockSpec((kvh, WINDOW, hd), prev_idx),
                      pl.BlockSpec((kvh, tq_, hd), lambda b, i: (0, b * blocks + i, 0))],
            out_specs=pl.BlockSpec((tq_, swa_heads * hd), lambda b, i: (b * blocks + i, 0)),
            out_shape=jax.ShapeDtypeStruct((nrows, swa_heads * hd), BF16),
            compiler_params=_params("parallel", "arbitrary"),
            name="swa_attention",
        )(swa_sinks[0], q, kp, kc, vp, vc)

    wblocks = seq // WINDOW
    att_p = swa(q_hm, k_hm, k_hm, v_hm, v_hm, WINDOW, nbp, wblocks,
                lambda b, i: (0, jnp.maximum(b * wblocks + i - 1, 0), 0), True)

    tqs = 16

    def pad_steps(x):
        x = x[:, mp:].reshape(x.shape[0], t_new, nbs, hd).transpose(0, 2, 1, 3)
        return jnp.pad(x, ((0, 0), (0, 0), (0, tqs - t_new), (0, 0))).reshape(x.shape[0], nbs * tqs, hd)

    def win_hm(x):
        return x.transpose(2, 0, 1, 3).reshape(kvh, nbs * WINDOW, hd).astype(BF16)

    att_s = swa(pad_steps(q_hm), win_hm(state_win_k), pad_steps(k_hm), win_hm(state_win_v), pad_steps(v_hm),
                tqs, nbs, 1, lambda b, i: (0, b, 0), False)
    att_s = att_s.reshape(nbs, tqs, swa_heads * hd)[:, :t_new].transpose(1, 0, 2).reshape(ms, swa_heads * hd)

    h = oproj(jnp.concatenate([att_p[:mp], att_s], axis=0), w_swa_o[0], h1, 1)
    h2 = _moe(h, tab, 1, 3, g_ffn[1], w_router[1], b_router[1], w_up, b_up, w_down, b_down,
              seq=seq, nb_prompt=nbp)

    def sample_major(x):
        return x.reshape(t_new, nbs, -1).transpose(1, 0, 2)

    y_prompt = h2[:mp].reshape(nbp, seq, d)
    y_sample = sample_major(h2[mp:])
    lat_p = ckv[:mp].reshape(1, nbp, seq, kv_lora)
    kr_p = kr[:mp].reshape(1, nbp, seq, rope)
    lat_s = sample_major(ckv[mp:])[None]
    kr_s = sample_major(kr[mp:])[None]
    k_p = k_all[:mp].reshape(nbp, seq, kvh, hd)
    v_p4 = v_all[:mp].reshape(nbp, seq, kvh, hd)
    k_n = sample_major(k_all[mp:]).reshape(nbs, t_new, kvh, hd)
    v_n = sample_major(v_all[mp:]).reshape(nbs, t_new, kvh, hd)
    win_k_s = jnp.concatenate([state_win_k, k_n], axis=1)[:, -WINDOW:]
    win_v_s = jnp.concatenate([state_win_v, v_n], axis=1)[:, -WINDOW:]
    return (y_prompt, y_sample, lat_p, kr_p, lat_s, kr_s,
            k_p[:, -WINDOW:], v_p4[:, -WINDOW:], win_k_s, win_v_s)
```

```python
import functools

import jax
import jax.numpy as jnp
from jax import lax
from jax.experimental import pallas as pl
from jax.experimental.pallas import tpu as pltpu

F32 = jnp.float32
BF16 = jnp.bfloat16

RMS_EPS = 1e-6
NEG_INF = -1e30
ROPE_THETA = 10000.0
LOG2_E = 1.4426950408889634
PAGE = 128
WINDOW = 128
TOP_K = 4
SWIGLU_ALPHA = 1.702
SWIGLU_LIMIT = 7.0
LANES = 128
VMEM_LIMIT = 56 * 1024 * 1024

TM = 512
MOE_TM = 512
MOE_SUB = 256
SAMPLE_PAGES = 16
SAMPLE_SUB = 256


def _nn(a, b):
    return jnp.dot(a, b, preferred_element_type=F32)


def _nt(a, b):
    return lax.dot_general(a, b, (((1,), (1,)), ((), ())), preferred_element_type=F32)


def _params(*sem):
    return pltpu.CompilerParams(dimension_semantics=sem, vmem_limit_bytes=VMEM_LIMIT)


def _norm_mod(h, g, shift, scale):
    tm, d = h.shape
    nb = shift.shape[0]
    y = h * lax.rsqrt(jnp.mean(h * h, axis=-1, keepdims=True) + RMS_EPS) * g
    y = y.reshape(tm // nb, nb, d) * (1.0 + scale[None]) + shift[None]
    return y.reshape(tm, d)


def _group64_rscale(x):
    lane = lax.broadcasted_iota(jnp.int32, x.shape, 1)
    x2 = x * x
    lo = jnp.sum(jnp.where(lane < 64, x2, 0.0), axis=-1, keepdims=True)
    hi = jnp.sum(jnp.where(lane >= 64, x2, 0.0), axis=-1, keepdims=True)
    return jnp.where(lane < 64, lax.rsqrt(lo / 64.0 + RMS_EPS), lax.rsqrt(hi / 64.0 + RMS_EPS))


def _adaln_kernel(c_ref, w_ref, b_ref, o_ref, *, nb_prompt):
    c = c_ref[...]
    a = (c * jax.nn.sigmoid(c)).astype(BF16)
    res = _nn(a, w_ref[...].astype(BF16)) + b_ref[...]
    nb = res.shape[0] - nb_prompt
    for b in range(nb_prompt):
        o_ref[b] = jnp.broadcast_to(res[b:b + 1], (nb, res.shape[1]))
    o_ref[nb_prompt] = res[nb_prompt:]


def _adaln_table(c, w, b, nb_prompt):
    n, d = c.shape
    nl, _, nout = w.shape
    nb = n - nb_prompt
    tn = 1024
    return pl.pallas_call(
        functools.partial(_adaln_kernel, nb_prompt=nb_prompt),
        grid=(nl, nout // tn),
        in_specs=[pl.BlockSpec((n, d), lambda l, j: (0, 0)),
                  pl.BlockSpec((None, d, tn), lambda l, j: (l, 0, j)),
                  pl.BlockSpec((None, 1, tn), lambda l, j: (l, 0, j))],
        out_specs=pl.BlockSpec((None, nb_prompt + 1, nb, tn), lambda l, j: (l, 0, 0, j)),
        out_shape=jax.ShapeDtypeStruct((nl, nb_prompt + 1, nb, nout), F32),
        compiler_params=_params("parallel", "parallel"),
        name="adaln",
    )(c, w, b.reshape(nl, 1, nout))


def _mla_down_kernel(h_ref, g_ref, sh_ref, sc_ref, w_ref, gql_ref, gkvl_ref, gkr_ref, gkrp_ref,
                     cos_ref, sin_ref, cq_ref, ckv_ref, kr_ref, *, q_lora, kv_lora, rope):
    u = _norm_mod(h_ref[...], g_ref[...], sh_ref[...], sc_ref[...]).astype(BF16)
    a = _nn(u, w_ref[...])
    q = a[:, :q_lora]
    cq_ref[...] = (q * lax.rsqrt(jnp.mean(q * q, axis=-1, keepdims=True) + RMS_EPS)
                   * gql_ref[...]).astype(BF16)
    c = a[:, q_lora:q_lora + kv_lora]
    ckv_ref[...] = c * lax.rsqrt(jnp.mean(c * c, axis=-1, keepdims=True) + RMS_EPS) * gkvl_ref[...]
    o = q_lora + kv_lora
    raw = a[:, o:o + rope]
    rot = a[:, o + rope:o + 2 * rope]
    r = lax.rsqrt(jnp.mean(raw * raw, axis=-1, keepdims=True) + RMS_EPS)
    kr_ref[...] = r * (raw * gkr_ref[...] * cos_ref[...] + rot * gkrp_ref[...] * sin_ref[...])


def _q_up_kernel(cq_ref, w_ref, gq_ref, tab_ref, o_ref, *, heads, nope, rope, scale):
    a = _nn(cq_ref[...], w_ref[...])
    tab = tab_ref[...]
    gq = gq_ref[...]
    for h in range(heads):
        x = a[:, h * LANES:(h + 1) * LANES]
        lane = lax.broadcasted_iota(jnp.int32, x.shape, 1)
        x2 = x * x
        ssn = jnp.sum(jnp.where(lane < nope, x2, 0.0), axis=-1, keepdims=True)
        ssr = jnp.sum(jnp.where((lane >= nope) & (lane < nope + rope), x2, 0.0), axis=-1, keepdims=True)
        r = jnp.where(lane < nope, lax.rsqrt(ssn / nope + RMS_EPS), lax.rsqrt(ssr / rope + RMS_EPS))
        o_ref[h] = (x * r * gq * tab * scale).astype(BF16)


def _kv_up_kernel(ckv_ref, kr_ref, wukt_ref, gkn_ref, wuv_ref, eye_ref, kt_ref, v_ref, *, heads, nope, rope):
    c = ckv_ref[...].astype(BF16)
    tm = c.shape[0]
    knt = _nt(wukt_ref[...], c).reshape(heads, nope, tm)
    ss = jnp.sum(knt * knt, axis=1, keepdims=True)
    kn = knt * lax.rsqrt(ss / nope + RMS_EPS) * gkn_ref[...][None]
    krt = _nt(eye_ref[...], kr_ref[...].astype(BF16)).astype(BF16)
    krt = jnp.broadcast_to(krt[None], (heads, rope, tm))
    kt_ref[0, :, 0, 0:nope, :] = kn.astype(BF16)
    kt_ref[0, :, 0, nope:nope + rope, :] = krt
    kt_ref[0, :, 0, nope + rope:nope + 2 * rope, :] = krt
    v_ref[...] = _nn(c, wuv_ref[...]).astype(BF16)


def _mla_flash_kernel(q_ref, kt_ref, v_ref, o_ref, *, tq, tk, vdim):
    qi = pl.program_id(2)
    n_full = (qi * tq) // tk
    qs = [q_ref[0], q_ref[1]]

    def step(j, carry, masked):
        start = pl.multiple_of(j * tk, tk)
        v = v_ref[pl.ds(start, tk), :]
        out = []
        for hh in range(2):
            m, l, acc = carry[hh]
            s = _nn(qs[hh], kt_ref[0, hh, j])
            if masked:
                row = qi * tq + lax.broadcasted_iota(jnp.int32, s.shape, 0)
                col = j * tk + lax.broadcasted_iota(jnp.int32, s.shape, 1)
                s = jnp.where(col <= row, s, NEG_INF)
            m_new = jnp.maximum(m, jnp.max(s, axis=-1, keepdims=True))
            alpha = jnp.exp2(m - m_new)
            p = jnp.exp2(s - m_new)
            l = alpha * l + jnp.sum(p, axis=-1, keepdims=True)
            acc = alpha * acc + _nn(p.astype(BF16), v)
            out.append((m_new, l, acc))
        return tuple(out)

    init = tuple((jnp.full((tq, 1), NEG_INF, F32), jnp.zeros((tq, 1), F32), jnp.zeros((tq, LANES), F32))
                 for _ in range(2))
    carry = lax.fori_loop(0, n_full, lambda j, c: step(j, c, False), init)
    (_, l0, acc0), (_, l1, acc1) = step(n_full, carry, True)
    lane = lax.broadcasted_iota(jnp.int32, (tq, LANES), 1)
    o_ref[...] = jnp.where(lane < vdim, acc0 / l0, acc1 / l1).astype(BF16)


def _q_absorb_kernel(q_ref, wuk_ref, gkn_ref, qa_ref, qr_ref, *, heads, nope, rope):
    for h in range(heads):
        q = q_ref[h].astype(F32)
        qg = (q[:, :nope] * gkn_ref[...]).astype(BF16)
        qa_ref[h] = _nn(qg, wuk_ref[h]).astype(BF16)
        qr_ref[h] = (q[:, nope:nope + rope] + q[:, nope + rope:nope + 2 * rope]).astype(BF16)


def _mla_sample_kernel(pt_ref, qa_ref, qr_ref, wukt_ref, cnew_ref, rnew_ref, poolc_ref, poolrt_ref,
                       o_ref, cbuf, rbuf, cb0, cb1, sc0, sc1, sems, *, heads, nope, n_chunks, n_batch, t_new):
    n = pl.program_id(0)
    cp = cbuf.shape[1]
    rows = qa_ref.shape[1]
    lat = qa_ref.shape[2]
    nw = heads * nope
    ppt = SAMPLE_SUB // PAGE
    sets = ((cb0, sc0), (cb1, sc1))

    def page_copies(page, slot, p):
        return (pltpu.make_async_copy(poolc_ref.at[page], cbuf.at[slot, p], sems.at[0, slot]),
                pltpu.make_async_copy(poolrt_ref.at[page], rbuf.at[slot, p], sems.at[1, slot]))

    def start_pages(b, ci, slot, p0, p1):
        for p in range(p0, p1):
            for cpy in page_copies(pt_ref[b, ci * cp + p], slot, p):
                cpy.start()

    def wait_chunk(slot):
        for p in range(cp):
            for cpy in page_copies(0, slot, p):
                cpy.wait()

    @pl.when(n == 0)
    def _():
        start_pages(0, 0, 0, 0, cp)

    lhs = jnp.concatenate([wukt_ref[...], qa_ref[0]], axis=0)
    qr = qr_ref[0]

    def scores(c, rope_scores):
        tk = c.shape[0]
        big = _nt(lhs, c)
        knt = big[:nw].reshape(heads, nope, tk)
        r = lax.rsqrt(jnp.sum(knt * knt, axis=1) / nope + RMS_EPS)
        return big[nw:] * jnp.concatenate([r] * t_new, axis=0) + rope_scores

    def update(s, c, carry):
        m, l, acc = carry
        m_new = jnp.maximum(m, jnp.max(s, axis=-1, keepdims=True))
        alpha = jnp.exp2(m - m_new)
        p = jnp.exp2(s - m_new)
        l = alpha * l + jnp.sum(p, axis=-1, keepdims=True)
        acc = alpha * acc + _nn(p.astype(BF16), c)
        return m_new, l, acc

    def score_chunk(slot, nxt_b, nxt_ci):
        cbs, scs = sets[slot]
        wait_chunk(slot)
        for j in range(cp // ppt):
            start_pages(nxt_b, nxt_ci, 1 - slot, j * ppt, (j + 1) * ppt)
            c = cbuf[slot, j * ppt:(j + 1) * ppt].reshape(SAMPLE_SUB, lat).astype(BF16)
            krt = jnp.concatenate([rbuf[slot, j * ppt + p] for p in range(ppt)], axis=1).astype(BF16)
            cbs[j * SAMPLE_SUB:(j + 1) * SAMPLE_SUB, :] = c
            scs[:, j * SAMPLE_SUB:(j + 1) * SAMPLE_SUB] = scores(c, _nn(qr, krt))

    def absorb(slot, carry):
        cbs, scs = sets[slot]
        return update(scs[...], cbs[...], carry)

    def pair(k, carry):
        score_chunk(1, n, 2 * k + 2)
        carry = absorb(0, carry)
        score_chunk(0, n, 2 * k + 3)
        return absorb(1, carry)

    init = (jnp.full((rows, 1), NEG_INF, F32), jnp.zeros((rows, 1), F32), jnp.zeros((rows, lat), F32))
    score_chunk(0, n, 1)
    carry = lax.fori_loop(0, (n_chunks - 2) // 2, pair, init)
    score_chunk(1, jnp.minimum(n + 1, n_batch - 1), 0)
    carry = absorb(0, carry)
    carry = absorb(1, carry)

    @pl.when(n == n_batch - 1)
    def _():
        wait_chunk(0)

    c = cnew_ref[0].astype(BF16)
    tk = c.shape[0]
    t_of_row = lax.broadcasted_iota(jnp.int32, (rows, tk), 0) // heads
    col = lax.broadcasted_iota(jnp.int32, (rows, tk), 1)
    s = jnp.where(col <= t_of_row, scores(c, _nt(qr, rnew_ref[0].astype(BF16))), NEG_INF)
    m, l, acc = update(s, c, carry)
    o_ref[0] = acc / l


def _unabsorb_kernel(o_ref, wuv_ref, out_ref, *, heads):
    out_ref[...] = jnp.concatenate(
        [_nn(o_ref[h], wuv_ref[h]) for h in range(heads)], axis=-1).astype(BF16)


def _oproj_kernel(attp_ref, atts_ref, w_ref, h_ref, gate_ref, o_ref, *, n_prompt_blocks):
    att = jnp.where(pl.program_id(0) < n_prompt_blocks, attp_ref[...], atts_ref[...])
    y = _nn(att, w_ref[...])
    tm, d = y.shape
    gate = gate_ref[...]
    nb = gate.shape[0]
    o_ref[...] = h_ref[...] + (y.reshape(tm // nb, nb, d) * gate[None]).reshape(tm, d)


def _router_kernel(h_ref, g_ref, sh_ref, sc_ref, whi_ref, wlo_ref, b_ref, u_ref, idx_ref, gate_ref):
    u = _norm_mod(h_ref[...], g_ref[...], sh_ref[...], sc_ref[...])
    uhi = u.astype(BF16)
    ulo = (u - uhi.astype(F32)).astype(BF16)
    u_ref[...] = u
    whi = whi_ref[...]
    logits = _nt(whi, uhi) + _nt(whi, ulo) + _nt(wlo_ref[...], uhi) + b_ref[...]
    ne = logits.shape[0]
    eid = lax.broadcasted_iota(jnp.int32, logits.shape, 0)
    work = logits
    vals, idxs = [], []
    for _ in range(TOP_K):
        m = jnp.max(work, axis=0, keepdims=True)
        idx = jnp.min(jnp.where(work == m, eid, ne), axis=0, keepdims=True)
        vals.append(m)
        idxs.append(idx)
        work = jnp.where(eid == idx, -jnp.inf, work)
    es = [jnp.exp(v - vals[0]) for v in vals]
    den = es[0] + es[1] + es[2] + es[3]
    idx_ref[...] = jnp.concatenate(idxs, axis=0)
    gate_ref[...] = jnp.concatenate([e / den for e in es], axis=0)


def _expert_kernel(be_ref, nrows_ref, nused_ref, x_ref, wup_ref, bup_ref, wdn_ref, bdn_ref, y_ref, wup_bf, wdn_bf,
                   *, ff):
    i = pl.program_id(0)
    n_here = nrows_ref[i]
    prev = be_ref[jnp.maximum(i - 1, 0)]
    new_expert = (i == 0) | (be_ref[i] != prev)

    @pl.when(new_expert & (n_here > 0))
    def _():
        wup_bf[...] = wup_ref[0].astype(BF16)
        wdn_bf[...] = wdn_ref[0].astype(BF16)

    for r0 in range(0, MOE_TM, MOE_SUB):
        @pl.when(n_here > r0)
        def _(r0=r0):
            x = x_ref[r0:r0 + MOE_SUB, :].astype(BF16)
            fc = 512
            acc = jnp.zeros((MOE_SUB, y_ref.shape[1]), F32)
            for c0 in range(0, ff, fc):
                glu = _nn(x, wup_bf[:, c0:c0 + fc]) + bup_ref[0, :, c0:c0 + fc]
                lin = _nn(x, wup_bf[:, ff + c0:ff + c0 + fc]) + bup_ref[0, :, ff + c0:ff + c0 + fc]
                glu = jnp.minimum(glu, SWIGLU_LIMIT)
                lin = jnp.clip(lin, -SWIGLU_LIMIT, SWIGLU_LIMIT)
                act = glu * jax.nn.sigmoid(SWIGLU_ALPHA * glu) * (lin + 1.0)
                acc = acc + _nn(act.astype(BF16), wdn_bf[c0:c0 + fc, :])
            y_ref[r0:r0 + MOE_SUB, :] = acc + bdn_ref[0]

        @pl.when(n_here <= r0)
        def _(r0=r0):
            y_ref[r0:r0 + MOE_SUB, :] = jnp.zeros((MOE_SUB, y_ref.shape[1]), F32)


def _combine_kernel(h_ref, y_ref, gk_ref, gate_ref, o_ref):
    gk = gk_ref[...]
    y = y_ref[0] * gk[:, 0:1]
    for k in range(1, TOP_K):
        y = y + y_ref[k] * gk[:, k:k + 1]
    tm, d = y.shape
    gate = gate_ref[...]
    nb = gate.shape[0]
    o_ref[...] = h_ref[...] + (y.reshape(tm // nb, nb, d) * gate[None]).reshape(tm, d)


def _shared_kv_kernel(h_ref, g_ref, sh_ref, sc_ref, w_ref, gk_ref, k_ref, v_ref, khm_ref, vhm_ref, *, kvh, hd):
    u = _norm_mod(h_ref[...], g_ref[...], sh_ref[...], sc_ref[...]).astype(BF16)
    a = _nn(u, w_ref[...])
    kw = kvh * hd
    gk = gk_ref[...]
    ks = []
    for j in range(kw // LANES):
        x = a[:, j * LANES:(j + 1) * LANES]
        ks.append(x * _group64_rscale(x) * gk[:, j * LANES:(j + 1) * LANES])
    k = jnp.concatenate(ks, axis=-1)
    v = a[:, kw:2 * kw]
    k_ref[...] = k
    v_ref[...] = v
    for hh in range(kvh):
        khm_ref[hh] = k[:, hh * hd:(hh + 1) * hd].astype(BF16)
        vhm_ref[hh] = v[:, hh * hd:(hh + 1) * hd].astype(BF16)


def _swa_q_kernel(h_ref, g_ref, sh_ref, sc_ref, w_ref, gq_ref, q_ref, *, heads, hd, scale):
    u = _norm_mod(h_ref[...], g_ref[...], sh_ref[...], sc_ref[...]).astype(BF16)
    a = _nn(u, w_ref[...])
    gq = gq_ref[...]
    for j in range(heads * hd // LANES):
        x = a[:, j * LANES:(j + 1) * LANES]
        y = x * _group64_rscale(x) * gq[:, j * LANES:(j + 1) * LANES] * scale
        q_ref[2 * j] = y[:, :hd].astype(BF16)
        q_ref[2 * j + 1] = y[:, hd:].astype(BF16)


def _softmax_pv(s, v):
    m = jnp.max(s, axis=-1, keepdims=True)
    e = jnp.exp(s - m)
    return _nn(e.astype(BF16), v) / jnp.sum(e, axis=-1, keepdims=True)


def _swa_prompt_kernel(bias_ref, q_ref, kp_ref, kc_ref, vp_ref, vc_ref, o_ref, *, heads, kvh, tq):
    group = heads // kvh
    rows = group * tq
    not_first = lax.broadcasted_iota(jnp.int32, (WINDOW + tq, q_ref.shape[2]), 0) > 0
    outs = []
    for g in range(kvh):
        k = jnp.where(not_first, jnp.concatenate([kp_ref[g], kc_ref[g]], axis=0), 0)
        v = jnp.where(not_first, jnp.concatenate([vp_ref[g], vc_ref[g]], axis=0), 0)
        q = q_ref[g * group:(g + 1) * group].reshape(rows, q_ref.shape[2])
        o = _softmax_pv(_nt(q, k) + bias_ref[g], v)
        for j in range(group):
            outs.append(o[j * tq:(j + 1) * tq])
    o_ref[...] = jnp.concatenate(outs, axis=-1).astype(BF16)


def _swa_sample_kernel(bias_ref, q_ref, k_ref, v_ref, o_ref):
    bias = bias_ref[...]
    for b in range(q_ref.shape[0]):
        o_ref[b] = _softmax_pv(_nt(q_ref[b], k_ref[b]) + bias, v_ref[b]).astype(BF16)


def _rope_tables(pos, rope):
    half = rope // 2
    inv = ROPE_THETA ** (-jnp.arange(half, dtype=F32) / half)
    ang = pos.astype(F32)[:, None] * inv[None, :]
    cos, sin = jnp.cos(ang), jnp.sin(ang)
    return jnp.concatenate([cos, cos], axis=-1), jnp.concatenate([sin, sin], axis=-1)


def _rot_cols(w):
    half = w.shape[-1] // 2
    return jnp.concatenate([-w[..., half:], w[..., :half]], axis=-1)


def _swap_halves(g):
    half = g.shape[-1] // 2
    return jnp.concatenate([g[..., half:], g[..., :half]], axis=-1)


def _moe(h, tab, layer, col0, g_ffn, w_router, b_router, w_up, b_up, w_down, b_down, *, seq, nb_prompt):
    m_rows, d = h.shape
    ne = w_router.shape[1]
    ff = w_down.shape[2]
    nb = tab.shape[2]
    n_tok_blocks = m_rows // TM

    def tab_spec(col):
        return pl.BlockSpec((None, None, nb, d),
                            lambda i: (layer, jnp.minimum(i * TM // seq, nb_prompt), 0, col))

    wr_t = w_router.T
    wr_hi = wr_t.astype(BF16)
    wr_lo = (wr_t - wr_hi.astype(F32)).astype(BF16)
    u, idx_t, gate_t = pl.pallas_call(
        _router_kernel,
        grid=(n_tok_blocks,),
        in_specs=[pl.BlockSpec((TM, d), lambda i: (i, 0)),
                  pl.BlockSpec((1, d), lambda i: (0, 0)),
                  tab_spec(col0), tab_spec(col0 + 1),
                  pl.BlockSpec((ne, d), lambda i: (0, 0)),
                  pl.BlockSpec((ne, d), lambda i: (0, 0)),
                  pl.BlockSpec((ne, 1), lambda i: (0, 0))],
        out_specs=[pl.BlockSpec((TM, d), lambda i: (i, 0)),
                   pl.BlockSpec((TOP_K, TM), lambda i: (0, i)),
                   pl.BlockSpec((TOP_K, TM), lambda i: (0, i))],
        out_shape=[jax.ShapeDtypeStruct((m_rows, d), F32),
                   jax.ShapeDtypeStruct((TOP_K, m_rows), jnp.int32),
                   jax.ShapeDtypeStruct((TOP_K, m_rows), F32)],
        compiler_params=_params("parallel"),
        name="router",
    )(h, g_ffn.reshape(1, d), tab, tab, wr_hi, wr_lo, b_router.reshape(ne, 1))

    a = m_rows * TOP_K
    e_flat = idx_t.T.reshape(a)
    order = jnp.argsort(e_flat).astype(jnp.int32)
    rank = jnp.argsort(order).astype(jnp.int32)
    onehot = e_flat[:, None] == jnp.arange(ne, dtype=jnp.int32)[None, :]
    counts = jnp.sum(onehot, axis=0, dtype=jnp.int32)
    padded = (counts + MOE_TM - 1) // MOE_TM * MOE_TM
    pad_end = jnp.cumsum(padded)
    shift = (pad_end - padded) - (jnp.cumsum(counts) - counts)
    pos = rank + jnp.sum(jnp.where(onehot, shift[None, :], 0), axis=1)
    n_blk = -(-a // MOE_TM) + ne
    n_slots = n_blk * MOE_TM
    blk_start = jnp.arange(n_blk, dtype=jnp.int32) * MOE_TM
    blk_expert = jnp.minimum(jnp.sum(pad_end[None, :] <= blk_start[:, None], axis=1), ne - 1).astype(jnp.int32)
    n_used = (pad_end[-1] // MOE_TM).astype(jnp.int32).reshape(1)
    blk_rows = jnp.clip((pad_end - padded + counts)[blk_expert] - blk_start, 0, MOE_TM)
    blk_rows = jnp.where(blk_start < pad_end[-1], blk_rows, 0).astype(jnp.int32)
    slot_sorted = jnp.arange(n_slots, dtype=jnp.int32) - jnp.repeat(shift[blk_expert], MOE_TM)
    slot_tok = order[jnp.clip(slot_sorted, 0, a - 1)] // TOP_K
    xs = jnp.take(u, slot_tok, axis=0, mode="clip")

    def live(i, nused):
        return jnp.minimum(i, nused[0] - 1)

    yb = pl.pallas_call(
        functools.partial(_expert_kernel, ff=ff),
        grid_spec=pltpu.PrefetchScalarGridSpec(
            num_scalar_prefetch=3,
            grid=(n_blk,),
            in_specs=[pl.BlockSpec((MOE_TM, d), lambda i, be, nr, nu: (live(i, nu), 0)),
                      pl.BlockSpec((None, 1, d, 2 * ff), lambda i, be, nr, nu: (layer, be[live(i, nu)], 0, 0)),
                      pl.BlockSpec((None, 1, 1, 2 * ff), lambda i, be, nr, nu: (layer, be[live(i, nu)], 0, 0)),
                      pl.BlockSpec((None, 1, ff, d), lambda i, be, nr, nu: (layer, be[live(i, nu)], 0, 0)),
                      pl.BlockSpec((None, 1, 1, d), lambda i, be, nr, nu: (layer, be[live(i, nu)], 0, 0))],
            out_specs=pl.BlockSpec((MOE_TM, d), lambda i, be, nr, nu: (i, 0)),
            scratch_shapes=[pltpu.VMEM((d, 2 * ff), BF16), pltpu.VMEM((ff, d), BF16)]),
        out_shape=jax.ShapeDtypeStruct((n_slots, d), F32),
        compiler_params=_params("arbitrary"),
        name="experts",
    )(blk_expert, blk_rows, n_used, xs, w_up, b_up.reshape(b_up.shape[0], ne, 1, 2 * ff), w_down,
      b_down.reshape(b_down.shape[0], ne, 1, d))

    ysel = jnp.take(yb, pos.reshape(m_rows, TOP_K).T.reshape(a), axis=0, mode="clip").reshape(TOP_K, m_rows, d)
    return pl.pallas_call(
        _combine_kernel,
        grid=(n_tok_blocks,),
        in_specs=[pl.BlockSpec((TM, d), lambda i: (i, 0)),
                  pl.BlockSpec((TOP_K, TM, d), lambda i: (0, i, 0)),
                  pl.BlockSpec((TM, TOP_K), lambda i: (i, 0)),
                  tab_spec(col0 + 2)],
        out_specs=pl.BlockSpec((TM, d), lambda i: (i, 0)),
        out_shape=jax.ShapeDtypeStruct((m_rows, d), F32),
        compiler_params=_params("parallel"),
        name="moe_combine",
    )(h, ysel, gate_t.T, tab)


def kernel(x_prompt, x_sample, c_prompt, c_sample, cache_mla_latent, cache_mla_krope, state_win_k, state_win_v, page_table, w_mod, b_mod, g_attn, g_ffn, w_mla_down, g_mla_q_lora, g_mla_kv_lora, w_mla_uq, w_mla_uk, w_mla_uv, g_mla_qn, g_mla_qr, g_mla_kn, g_mla_kr, w_mla_o, w_kvmod, b_kvmod, g_kv_norm, w_kv, g_swa_k, w_swa_q, g_swa_q, swa_sinks, w_swa_o, w_router, b_router, w_up, b_up, w_down, b_down):
    nbp, seq, d = x_prompt.shape
    nbs, t_new, _ = x_sample.shape
    q_lora = g_mla_q_lora.shape[1]
    kv_lora = g_mla_kv_lora.shape[1]
    heads, nope = w_mla_uk.shape[2], w_mla_uk.shape[3]
    rope = g_mla_qr.shape[1]
    vdim = w_mla_uv.shape[3]
    n_pages = page_table.shape[1]
    past = n_pages * PAGE
    swa_heads = swa_sinks.shape[1]
    hd = g_swa_k.shape[0]
    kvh = w_kv.shape[1] // (2 * hd)
    assert nope + 2 * rope == LANES and 2 * vdim == LANES and 2 * hd == LANES
    assert seq % TM == 0 and TM % nbs == 0 and (nbs * t_new) % TM == 0 and nbs % 8 == 0
    assert w_mod.shape[0] == 2 and n_pages % (2 * SAMPLE_PAGES) == 0

    mp = nbp * seq
    ms = nbs * t_new
    m_rows = mp + ms
    n_tok_blocks = m_rows // TM
    n_prompt_blocks = mp // TM
    blocks_per_seq = seq // TM

    def tab_idx(i):
        return jnp.minimum(i * TM // seq, nbp)

    def tok(i):
        return (i, 0)

    def const2(i):
        return (0, 0)

    h0 = jnp.concatenate([x_prompt.reshape(mp, d), x_sample.transpose(1, 0, 2).reshape(ms, d)], axis=0)
    c_all = jnp.concatenate([c_prompt, c_sample], axis=0)
    tab = _adaln_table(c_all, w_mod, b_mod, nbp)
    tab_kv = _adaln_table(c_all, w_kvmod[None], b_kvmod[None], nbp)

    def tab_spec(layer, col):
        return pl.BlockSpec((None, None, nbs, d), lambda i: (layer, tab_idx(i), 0, col))

    cos_p, sin_p = _rope_tables(jnp.arange(seq), rope)
    cos_s, sin_s = _rope_tables(past + jnp.arange(t_new), rope)
    cos_tab = jnp.concatenate([cos_p, jnp.repeat(cos_s, nbs, axis=0)], axis=0)
    sin_tab = jnp.concatenate([sin_p, jnp.repeat(sin_s, nbs, axis=0)], axis=0)

    def pos_blk(i):
        return (jnp.where(i < n_prompt_blocks, i % blocks_per_seq, blocks_per_seq + i - n_prompt_blocks), 0)

    wd = w_mla_down[0]
    w_down_ext = jnp.concatenate([wd, _rot_cols(wd[:, q_lora + kv_lora:])], axis=1).astype(BF16)
    nd = w_down_ext.shape[1]
    cq, ckv, kr = pl.pallas_call(
        functools.partial(_mla_down_kernel, q_lora=q_lora, kv_lora=kv_lora, rope=rope),
        grid=(n_tok_blocks,),
        in_specs=[pl.BlockSpec((TM, d), tok), pl.BlockSpec((1, d), const2), tab_spec(0, 0), tab_spec(0, 1),
                  pl.BlockSpec((d, nd), const2), pl.BlockSpec((1, q_lora), const2),
                  pl.BlockSpec((1, kv_lora), const2), pl.BlockSpec((1, rope), const2),
                  pl.BlockSpec((1, rope), const2), pl.BlockSpec((TM, rope), pos_blk),
                  pl.BlockSpec((TM, rope), pos_blk)],
        out_specs=[pl.BlockSpec((TM, q_lora), tok), pl.BlockSpec((TM, kv_lora), tok),
                   pl.BlockSpec((TM, rope), tok)],
        out_shape=[jax.ShapeDtypeStruct((m_rows, q_lora), BF16),
                   jax.ShapeDtypeStruct((m_rows, kv_lora), F32),
                   jax.ShapeDtypeStruct((m_rows, rope), F32)],
        compiler_params=_params("parallel"),
        name="mla_down",
    )(h0, g_attn[0].reshape(1, d), tab, tab, w_down_ext, g_mla_q_lora, g_mla_kv_lora,
      g_mla_kr, _swap_halves(g_mla_kr), cos_tab, sin_tab)

    wq = w_mla_uq[0]
    w_q_cat = jnp.concatenate([wq, _rot_cols(wq[..., nope:])], axis=-1).reshape(q_lora, heads * LANES).astype(BF16)
    gq_cat = jnp.concatenate([g_mla_qn[0], g_mla_qr[0], _swap_halves(g_mla_qr[0])]).reshape(1, LANES)
    q_tab = jnp.concatenate([jnp.ones((cos_tab.shape[0], nope), F32), cos_tab, sin_tab], axis=1)
    mla_scale = (nope + rope) ** -0.5 * LOG2_E
    q_cat = pl.pallas_call(
        functools.partial(_q_up_kernel, heads=heads, nope=nope, rope=rope, scale=mla_scale),
        grid=(n_tok_blocks,),
        in_specs=[pl.BlockSpec((TM, q_lora), tok), pl.BlockSpec((q_lora, heads * LANES), const2),
                  pl.BlockSpec((1, LANES), const2), pl.BlockSpec((TM, LANES), pos_blk)],
        out_specs=pl.BlockSpec((heads, TM, LANES), lambda i: (0, i, 0)),
        out_shape=jax.ShapeDtypeStruct((heads, m_rows, LANES), BF16),
        compiler_params=_params("parallel"),
        name="q_up",
    )(cq, w_q_cat, gq_cat, q_tab)

    wuk = w_mla_uk[0]
    wuv = w_mla_uv[0]
    wuk_t = wuk.reshape(kv_lora, heads * nope).T.astype(BF16)
    tk = TM
    nk = seq // tk
    kt, v_p = pl.pallas_call(
        functools.partial(_kv_up_kernel, heads=heads, nope=nope, rope=rope),
        grid=(n_prompt_blocks,),
        in_specs=[pl.BlockSpec((TM, kv_lora), tok), pl.BlockSpec((TM, rope), tok),
                  pl.BlockSpec((heads * nope, kv_lora), const2), pl.BlockSpec((nope, 1), const2),
                  pl.BlockSpec((kv_lora, heads * vdim), const2), pl.BlockSpec((rope, rope), const2)],
        out_specs=[pl.BlockSpec((1, heads, 1, LANES, tk), lambda i: (i // nk, 0, i % nk, 0, 0)),
                   pl.BlockSpec((TM, heads * vdim), tok)],
        out_shape=[jax.ShapeDtypeStruct((nbp, heads, nk, LANES, tk), BF16),
                   jax.ShapeDtypeStruct((mp, heads * vdim), BF16)],
        compiler_params=_params("parallel"),
        name="kv_up",
    )(ckv, kr, wuk_t, g_mla_kn[0].reshape(nope, 1), wuv.reshape(kv_lora, heads * vdim).astype(BF16),
      jnp.eye(rope, dtype=BF16))

    tq = 256
    nq = seq // tq
    att_p = pl.pallas_call(
        functools.partial(_mla_flash_kernel, tq=tq, tk=tk, vdim=vdim),
        grid=(nbp, heads // 2, nq),
        in_specs=[pl.BlockSpec((2, tq, LANES), lambda b, hp, qi: (hp, b * nq + qi, 0)),
                  pl.BlockSpec((1, 2, nk, LANES, tk), lambda b, hp, qi: (b, hp, 0, 0, 0)),
                  pl.BlockSpec((seq, LANES), lambda b, hp, qi: (b, hp))],
        out_specs=pl.BlockSpec((tq, LANES), lambda b, hp, qi: (b * nq + qi, hp)),
        out_shape=jax.ShapeDtypeStruct((mp, heads * vdim), BF16),
        compiler_params=_params("parallel", "parallel", "arbitrary"),
        name="mla_prompt_attention",
    )(q_cat, kt, v_p)

    sample_blk = mp // ms
    qa, qr = pl.pallas_call(
        functools.partial(_q_absorb_kernel, heads=heads, nope=nope, rope=rope),
        grid=(1,),
        in_specs=[pl.BlockSpec((heads, ms, LANES), lambda i: (0, sample_blk, 0)),
                  pl.BlockSpec((heads, nope, kv_lora), lambda i: (0, 0, 0)),
                  pl.BlockSpec((1, nope), const2)],
        out_specs=[pl.BlockSpec((heads, ms, kv_lora), lambda i: (0, 0, 0)),
                   pl.BlockSpec((heads, ms, rope), lambda i: (0, 0, 0))],
        out_shape=[jax.ShapeDtypeStruct((heads, ms, kv_lora), BF16),
                   jax.ShapeDtypeStruct((heads, ms, rope), BF16)],
        compiler_params=_params("arbitrary"),
        name="q_absorb",
    )(q_cat, wuk.transpose(1, 2, 0).astype(BF16), g_mla_kn)
    rows = t_new * heads

    def per_batch(x):
        return x.reshape(heads, t_new, nbs, -1).transpose(2, 1, 0, 3).reshape(nbs, rows, -1)

    def new_rows(x):
        x = x.reshape(t_new, nbs, -1).transpose(1, 0, 2)
        return jnp.pad(x, ((0, 0), (0, 8 - t_new), (0, 0)))

    n_chunks = n_pages // SAMPLE_PAGES
    sample_tokens = SAMPLE_PAGES * PAGE
    o_lat = pl.pallas_call(
        functools.partial(_mla_sample_kernel, heads=heads, nope=nope, n_chunks=n_chunks, n_batch=nbs,
                          t_new=t_new),
        grid_spec=pltpu.PrefetchScalarGridSpec(
            num_scalar_prefetch=1,
            grid=(nbs,),
            in_specs=[pl.BlockSpec((1, rows, kv_lora), lambda n, pt: (n, 0, 0)),
                      pl.BlockSpec((1, rows, rope), lambda n, pt: (n, 0, 0)),
                      pl.BlockSpec((heads * nope, kv_lora), lambda n, pt: (0, 0)),
                      pl.BlockSpec((1, 8, kv_lora), lambda n, pt: (n, 0, 0)),
                      pl.BlockSpec((1, 8, rope), lambda n, pt: (n, 0, 0)),
                      pl.BlockSpec(memory_space=pl.ANY),
                      pl.BlockSpec(memory_space=pl.ANY)],
            out_specs=pl.BlockSpec((1, rows, kv_lora), lambda n, pt: (n, 0, 0)),
            scratch_shapes=[pltpu.VMEM((2, SAMPLE_PAGES, PAGE, kv_lora), F32),
                            pltpu.VMEM((2, SAMPLE_PAGES, rope, PAGE), F32),
                            pltpu.VMEM((sample_tokens, kv_lora), BF16),
                            pltpu.VMEM((sample_tokens, kv_lora), BF16),
                            pltpu.VMEM((rows, sample_tokens), F32),
                            pltpu.VMEM((rows, sample_tokens), F32),
                            pltpu.SemaphoreType.DMA((2, 2))]),
        out_shape=jax.ShapeDtypeStruct((nbs, rows, kv_lora), F32),
        compiler_params=_params("arbitrary"),
        name="mla_sample_attention",
    )(page_table, per_batch(qa), per_batch(qr), wuk_t, new_rows(ckv[mp:]), new_rows(kr[mp:]),
      cache_mla_latent[0], cache_mla_krope[0].transpose(0, 2, 1))
    o_lat = o_lat.reshape(nbs, t_new, heads, kv_lora).transpose(2, 1, 0, 3).reshape(heads, ms, kv_lora)
    att_s = pl.pallas_call(
        functools.partial(_unabsorb_kernel, heads=heads),
        grid=(1,),
        in_specs=[pl.BlockSpec((heads, ms, kv_lora), lambda i: (0, 0, 0)),
                  pl.BlockSpec((heads, kv_lora, vdim), lambda i: (0, 0, 0))],
        out_specs=pl.BlockSpec((ms, heads * vdim), const2),
        out_shape=jax.ShapeDtypeStruct((ms, heads * vdim), BF16),
        compiler_params=_params("arbitrary"),
        name="unabsorb",
    )(o_lat.astype(BF16), wuv.transpose(1, 0, 2).astype(BF16))

    def oproj(att_prompt, att_sample, w, h, layer):
        return pl.pallas_call(
            functools.partial(_oproj_kernel, n_prompt_blocks=n_prompt_blocks),
            grid=(n_tok_blocks,),
            in_specs=[pl.BlockSpec((TM, w.shape[0]), lambda i: (jnp.minimum(i, n_prompt_blocks - 1), 0)),
                      pl.BlockSpec((TM, w.shape[0]), lambda i: (jnp.maximum(i - n_prompt_blocks, 0), 0)),
                      pl.BlockSpec(w.shape, const2),
                      pl.BlockSpec((TM, d), tok), tab_spec(layer, 2)],
            out_specs=pl.BlockSpec((TM, d), tok),
            out_shape=jax.ShapeDtypeStruct((m_rows, d), F32),
            compiler_params=_params("parallel"),
            name="oproj",
        )(att_prompt, att_sample, w.astype(BF16), h, tab)

    h = oproj(att_p, att_s, w_mla_o[0], h0, 0)
    h1 = _moe(h, tab, 0, 3, g_ffn[0], w_router[0], b_router[0], w_up, b_up, w_down, b_down,
              seq=seq, nb_prompt=nbp)

    kw = kvh * hd
    k_all, v_all, k_hm, v_hm = pl.pallas_call(
        functools.partial(_shared_kv_kernel, kvh=kvh, hd=hd),
        grid=(n_tok_blocks,),
        in_specs=[pl.BlockSpec((TM, d), tok), pl.BlockSpec((1, d), const2), tab_spec(0, 0), tab_spec(0, 1),
                  pl.BlockSpec((d, 2 * kw), const2), pl.BlockSpec((1, kw), const2)],
        out_specs=[pl.BlockSpec((TM, kw), tok), pl.BlockSpec((TM, kw), tok),
                   pl.BlockSpec((kvh, TM, hd), lambda i: (0, i, 0)),
                   pl.BlockSpec((kvh, TM, hd), lambda i: (0, i, 0))],
        out_shape=[jax.ShapeDtypeStruct((m_rows, kw), F32), jax.ShapeDtypeStruct((m_rows, kw), F32),
                   jax.ShapeDtypeStruct((kvh, m_rows, hd), BF16),
                   jax.ShapeDtypeStruct((kvh, m_rows, hd), BF16)],
        compiler_params=_params("parallel"),
        name="shared_kv",
    )(h1, g_kv_norm.reshape(1, d), tab_kv, tab_kv, w_kv.astype(BF16), jnp.tile(g_swa_k, kvh).reshape(1, kw))

    swa_scale = hd ** -0.5
    q_hm = pl.pallas_call(
        functools.partial(_swa_q_kernel, heads=swa_heads, hd=hd, scale=swa_scale),
        grid=(n_tok_blocks,),
        in_specs=[pl.BlockSpec((TM, d), tok), pl.BlockSpec((1, d), const2), tab_spec(1, 0), tab_spec(1, 1),
                  pl.BlockSpec((d, swa_heads * hd), const2), pl.BlockSpec((1, swa_heads * hd), const2)],
        out_specs=pl.BlockSpec((swa_heads, TM, hd), lambda i: (0, i, 0)),
        out_shape=jax.ShapeDtypeStruct((swa_heads, m_rows, hd), BF16),
        compiler_params=_params("parallel"),
        name="swa_q",
    )(h1, g_attn[1].reshape(1, d), tab, tab, w_swa_q[0].astype(BF16),
      jnp.tile(g_swa_q[0], swa_heads).reshape(1, swa_heads * hd))

    group = swa_heads // kvh
    slopes = 2.0 ** (-8.0 * jnp.arange(1, swa_heads + 1, dtype=F32) / swa_heads)
    sinks = swa_sinks[0].astype(F32)

    def bias_table(dist, valid, sink_col, head_of_row):
        b = jnp.where(valid, -slopes[head_of_row][:, None] * dist.astype(F32), NEG_INF)
        return jnp.where(sink_col, sinks[head_of_row][:, None], b)

    tq = WINDOW
    wblocks = seq // tq
    r = jnp.arange(group * tq)
    col = jnp.arange(WINDOW + tq)
    dist = (r % tq)[:, None] + WINDOW - col[None, :]
    inside = (dist >= 0) & (dist < WINDOW)
    bias_p = jnp.stack([
        jnp.stack([bias_table(dist, inside & ok[None, :], (col == 0)[None, :], g * group + r // tq)
                   for g in range(kvh)])
        for ok in (col >= 0, col >= WINDOW)])
    att_p = pl.pallas_call(
        functools.partial(_swa_prompt_kernel, heads=swa_heads, kvh=kvh, tq=tq),
        grid=(nbp, wblocks),
        in_specs=[pl.BlockSpec((None, kvh, group * tq, WINDOW + tq), lambda b, i: (jnp.where(i == 0, 1, 0), 0, 0, 0)),
                  pl.BlockSpec((swa_heads, tq, hd), lambda b, i: (0, b * wblocks + i, 0)),
                  pl.BlockSpec((kvh, WINDOW, hd), lambda b, i: (0, jnp.maximum(b * wblocks + i - 1, 0), 0)),
                  pl.BlockSpec((kvh, tq, hd), lambda b, i: (0, b * wblocks + i, 0)),
                  pl.BlockSpec((kvh, WINDOW, hd), lambda b, i: (0, jnp.maximum(b * wblocks + i - 1, 0), 0)),
                  pl.BlockSpec((kvh, tq, hd), lambda b, i: (0, b * wblocks + i, 0))],
        out_specs=pl.BlockSpec((tq, swa_heads * hd), lambda b, i: (b * wblocks + i, 0)),
        out_shape=jax.ShapeDtypeStruct((mp, swa_heads * hd), BF16),
        compiler_params=_params("parallel", "arbitrary"),
        name="swa_prompt_attention",
    )(bias_p, q_hm, k_hm, k_hm, v_hm, v_hm)

    tpad = 8
    nkg = WINDOW + tpad

    def stack_keys(win, new_hm):
        new = new_hm[:, mp:].reshape(kvh, t_new, nbs, hd).transpose(2, 0, 1, 3)
        new = jnp.pad(new, ((0, 0), (0, 0), (0, tpad - t_new), (0, 0)))
        return jnp.concatenate([win.transpose(0, 2, 1, 3).astype(BF16), new], axis=2).reshape(nbs, kvh * nkg, hd)

    rs = jnp.arange(t_new * swa_heads)
    t_of, h_of = rs // swa_heads, rs % swa_heads
    cs = jnp.arange(kvh * nkg)
    j = cs % nkg
    dist_s = jnp.where(j < WINDOW, WINDOW + t_of[:, None] - j[None, :], t_of[:, None] - (j[None, :] - WINDOW))
    own = (cs // nkg)[None, :] == (h_of // group)[:, None]
    ok_s = own & (dist_s >= 0) & (dist_s < WINDOW) & ((j < WINDOW + t_new)[None, :])
    bias_s = bias_table(dist_s, ok_s, own & ((j == WINDOW + t_new)[None, :]), h_of)
    q_s = q_hm[:, mp:].reshape(swa_heads, t_new, nbs, hd).transpose(2, 1, 0, 3).reshape(nbs, t_new * swa_heads, hd)
    nbi = 8
    att_s = pl.pallas_call(
        _swa_sample_kernel,
        grid=(nbs // nbi,),
        in_specs=[pl.BlockSpec(bias_s.shape, const2),
                  pl.BlockSpec((nbi, t_new * swa_heads, hd), lambda i: (i, 0, 0)),
                  pl.BlockSpec((nbi, kvh * nkg, hd), lambda i: (i, 0, 0)),
                  pl.BlockSpec((nbi, kvh * nkg, hd), lambda i: (i, 0, 0))],
        out_specs=pl.BlockSpec((nbi, t_new * swa_heads, hd), lambda i: (i, 0, 0)),
        out_shape=jax.ShapeDtypeStruct((nbs, t_new * swa_heads, hd), BF16),
        compiler_params=_params("parallel"),
        name="swa_sample_attention",
    )(bias_s, q_s, stack_keys(state_win_k, k_hm), stack_keys(state_win_v, v_hm))
    att_s = att_s.reshape(nbs, t_new, swa_heads * hd).transpose(1, 0, 2).reshape(ms, swa_heads * hd)

    h = oproj(att_p, att_s, w_swa_o[0], h1, 1)
    h2 = _moe(h, tab, 1, 3, g_ffn[1], w_router[1], b_router[1], w_up, b_up, w_down, b_down,
              seq=seq, nb_prompt=nbp)

    def sample_major(x):
        return x.reshape(t_new, nbs, -1).transpose(1, 0, 2)

    y_prompt = h2[:mp].reshape(nbp, seq, d)
    y_sample = sample_major(h2[mp:])
    lat_p = ckv[:mp].reshape(1, nbp, seq, kv_lora)
    kr_p = kr[:mp].reshape(1, nbp, seq, rope)
    lat_s = sample_major(ckv[mp:])[None]
    kr_s = sample_major(kr[mp:])[None]
    k_p = k_all[:mp].reshape(nbp, seq, kvh, hd)
    v_p4 = v_all[:mp].reshape(nbp, seq, kvh, hd)
    k_n = sample_major(k_all[mp:]).reshape(nbs, t_new, kvh, hd)
    v_n = sample_major(v_all[mp:]).reshape(nbs, t_new, kvh, hd)
    win_k_s = jnp.concatenate([state_win_k, k_n], axis=1)[:, -WINDOW:]
    win_v_s = jnp.concatenate([state_win_v, v_n], axis=1)[:, -WINDOW:]
    return (y_prompt, y_sample, lat_p, kr_p, lat_s, kr_s,
            k_p[:, -WINDOW:], v_p4[:, -WINDOW:], win_k_s, win_v_s)
```

```python
import functools

import jax
import jax.numpy as jnp
from jax import lax
from jax.experimental import pallas as pl
from jax.experimental.pallas import tpu as pltpu

F32 = jnp.float32
BF16 = jnp.bfloat16

RMS_EPS = 1e-6
NEG_INF = -1e30
ROPE_THETA = 10000.0
LOG2_E = 1.4426950408889634
PAGE = 128
WINDOW = 128
TOP_K = 4
SWIGLU_ALPHA = 1.702
SWIGLU_LIMIT = 7.0
LANES = 128
VMEM_LIMIT = 56 * 1024 * 1024

TM = 512
MOE_TM = 512
MOE_SUB = 256
FLASH_ROWS = 256
SAMPLE_PAGES = 16
SAMPLE_SUB = 256


def _nn(a, b):
    return jnp.dot(a, b, preferred_element_type=F32)


def _nt(a, b):
    return lax.dot_general(a, b, (((1,), (1,)), ((), ())), preferred_element_type=F32)


def _params(*sem):
    return pltpu.CompilerParams(dimension_semantics=sem, vmem_limit_bytes=VMEM_LIMIT)


def _norm_mod(h, g, shift, scale):
    tm, d = h.shape
    nb = shift.shape[0]
    y = h * lax.rsqrt(jnp.mean(h * h, axis=-1, keepdims=True) + RMS_EPS) * g
    y = y.reshape(tm // nb, nb, d) * (1.0 + scale[None]) + shift[None]
    return y.reshape(tm, d)


def _pack_bf16_pairs(x):
    k = x.shape[1] // 2
    r = x.astype(BF16).astype(F32)
    hi = lax.bitcast_convert_type(r[:, :k], jnp.uint32) & jnp.uint32(0xFFFF0000)
    lo = lax.bitcast_convert_type(r[:, k:], jnp.uint32) >> 16
    return hi | lo


def _unpack_bf16_pairs(w):
    return (lax.bitcast_convert_type(w & jnp.uint32(0xFFFF0000), F32),
            lax.bitcast_convert_type(w << 16, F32))


def _group64_rscale(x):
    lane = lax.broadcasted_iota(jnp.int32, x.shape, 1)
    x2 = x * x
    lo = jnp.sum(jnp.where(lane < 64, x2, 0.0), axis=-1, keepdims=True)
    hi = jnp.sum(jnp.where(lane >= 64, x2, 0.0), axis=-1, keepdims=True)
    return jnp.where(lane < 64, lax.rsqrt(lo / 64.0 + RMS_EPS), lax.rsqrt(hi / 64.0 + RMS_EPS))


def _adaln_kernel(c_ref, w_ref, b_ref, o_ref, *, nb_prompt):
    c = c_ref[...]
    a = (c * jax.nn.sigmoid(c)).astype(BF16)
    res = _nn(a, w_ref[...].astype(BF16)) + b_ref[...]
    nb = res.shape[0] - nb_prompt
    for b in range(nb_prompt):
        o_ref[b] = jnp.broadcast_to(res[b:b + 1], (nb, res.shape[1]))
    o_ref[nb_prompt] = res[nb_prompt:]


def _adaln_table(c, w, b, nb_prompt):
    n, d = c.shape
    nl, _, nout = w.shape
    nb = n - nb_prompt
    tn = 1024
    return pl.pallas_call(
        functools.partial(_adaln_kernel, nb_prompt=nb_prompt),
        grid=(nl, nout // tn),
        in_specs=[pl.BlockSpec((n, d), lambda l, j: (0, 0)),
                  pl.BlockSpec((None, d, tn), lambda l, j: (l, 0, j)),
                  pl.BlockSpec((None, 1, tn), lambda l, j: (l, 0, j))],
        out_specs=pl.BlockSpec((None, nb_prompt + 1, nb, tn), lambda l, j: (l, 0, 0, j)),
        out_shape=jax.ShapeDtypeStruct((nl, nb_prompt + 1, nb, nout), F32),
        compiler_params=_params("parallel", "parallel"),
        name="adaln",
    )(c, w, b.reshape(nl, 1, nout))


def _mla_down_kernel(h_ref, g_ref, sh_ref, sc_ref, w_ref, gql_ref, gkvl_ref, gkr_ref, gkrp_ref,
                     cos_ref, sin_ref, cq_ref, ckv_ref, kr_ref, *, q_lora, kv_lora, rope):
    u = _norm_mod(h_ref[...], g_ref[...], sh_ref[...], sc_ref[...]).astype(BF16)
    a = _nn(u, w_ref[...])
    q = a[:, :q_lora]
    cq_ref[...] = (q * lax.rsqrt(jnp.mean(q * q, axis=-1, keepdims=True) + RMS_EPS)
                   * gql_ref[...]).astype(BF16)
    c = a[:, q_lora:q_lora + kv_lora]
    ckv_ref[...] = c * lax.rsqrt(jnp.mean(c * c, axis=-1, keepdims=True) + RMS_EPS) * gkvl_ref[...]
    o = q_lora + kv_lora
    raw = a[:, o:o + rope]
    rot = a[:, o + rope:o + 2 * rope]
    r = lax.rsqrt(jnp.mean(raw * raw, axis=-1, keepdims=True) + RMS_EPS)
    kr_ref[...] = r * (raw * gkr_ref[...] * cos_ref[...] + rot * gkrp_ref[...] * sin_ref[...])


def _q_up_kernel(cq_ref, w_ref, gq_ref, tab_ref, o_ref, *, heads, nope, rope, scale):
    a = _nn(cq_ref[...], w_ref[...])
    tab = tab_ref[...]
    gq = gq_ref[...]
    for h in range(heads):
        x = a[:, h * LANES:(h + 1) * LANES]
        lane = lax.broadcasted_iota(jnp.int32, x.shape, 1)
        x2 = x * x
        ssn = jnp.sum(jnp.where(lane < nope, x2, 0.0), axis=-1, keepdims=True)
        ssr = jnp.sum(jnp.where((lane >= nope) & (lane < nope + rope), x2, 0.0), axis=-1, keepdims=True)
        r = jnp.where(lane < nope, lax.rsqrt(ssn / nope + RMS_EPS), lax.rsqrt(ssr / rope + RMS_EPS))
        o_ref[h] = (x * r * gq * tab * scale).astype(BF16)


def _kv_up_kernel(ckv_ref, kr_ref, wukt_ref, gkn_ref, wuv_ref, eye_ref, kt_ref, v_ref, *, heads, nope, rope):
    c = ckv_ref[...].astype(BF16)
    tm = c.shape[0]
    knt = _nt(wukt_ref[...], c).reshape(heads, nope, tm)
    ss = jnp.sum(knt * knt, axis=1, keepdims=True)
    kn = knt * lax.rsqrt(ss / nope + RMS_EPS) * gkn_ref[...][None]
    krt = _nt(eye_ref[...], kr_ref[...].astype(BF16)).astype(BF16)
    krt = jnp.broadcast_to(krt[None], (heads, rope, tm))
    kt_ref[0, :, 0, 0:nope, :] = kn.astype(BF16)
    kt_ref[0, :, 0, nope:nope + rope, :] = krt
    kt_ref[0, :, 0, nope + rope:nope + 2 * rope, :] = krt
    v_ref[...] = _nn(c, wuv_ref[...]).astype(BF16)


def _mla_flash_kernel(q_ref, kt_ref, v_ref, o_ref, *, tq, tk, vdim):
    qi = pl.program_id(2)
    n_full = (qi * tq) // tk
    parts = tq // FLASH_ROWS
    qs = [[q_ref[hh, r * FLASH_ROWS:(r + 1) * FLASH_ROWS, :] for r in range(parts)] for hh in range(2)]

    def step(j, carry, masked):
        start = pl.multiple_of(j * tk, tk)
        v = v_ref[pl.ds(start, tk), :]
        out = []
        for hh in range(2):
            kt = kt_ref[0, hh, j]
            for r in range(parts):
                m, l, acc = carry[hh * parts + r]
                s = _nn(qs[hh][r], kt)
                if masked:
                    row = qi * tq + r * FLASH_ROWS + lax.broadcasted_iota(jnp.int32, s.shape, 0)
                    col = j * tk + lax.broadcasted_iota(jnp.int32, s.shape, 1)
                    s = jnp.where(col <= row, s, NEG_INF)
                m_new = jnp.maximum(m, jnp.max(s, axis=-1, keepdims=True))
                alpha = jnp.exp2(m - m_new)
                p = jnp.exp2(s - m_new)
                l = alpha * l + jnp.sum(p, axis=-1, keepdims=True)
                acc = alpha * acc + _nn(p.astype(BF16), v)
                out.append((m_new, l, acc))
        return tuple(out)

    init = tuple((jnp.full((FLASH_ROWS, 1), NEG_INF, F32), jnp.zeros((FLASH_ROWS, 1), F32),
                  jnp.zeros((FLASH_ROWS, LANES), F32)) for _ in range(2 * parts))
    carry = lax.fori_loop(0, n_full, lambda j, c: step(j, c, False), init)
    carry = step(n_full, carry, True)
    lane = lax.broadcasted_iota(jnp.int32, (FLASH_ROWS, LANES), 1)
    for r in range(parts):
        (_, l0, acc0), (_, l1, acc1) = carry[r], carry[parts + r]
        o_ref[r * FLASH_ROWS:(r + 1) * FLASH_ROWS, :] = jnp.where(lane < vdim, acc0 / l0, acc1 / l1).astype(BF16)


def _q_absorb_kernel(q_ref, wuk_ref, gkn_ref, qa_ref, qr_ref, *, heads, nope, rope):
    for h in range(heads):
        q = q_ref[h].astype(F32)
        qg = (q[:, :nope] * gkn_ref[...]).astype(BF16)
        qa_ref[h] = _nn(qg, wuk_ref[h]).astype(BF16)
        qr_ref[h] = (q[:, nope:nope + rope] + q[:, nope + rope:nope + 2 * rope]).astype(BF16)


def _mla_sample_kernel(pt_ref, qa_ref, qr_ref, wukt_ref, cnew_ref, rnew_ref, poolc_ref, poolrt_ref,
                       o_ref, cbuf, rbuf, cb0, cb1, sc0, sc1, sems, *, heads, nope, n_chunks, n_batch, t_new):
    n = pl.program_id(0)
    cp = cbuf.shape[1]
    rows = qa_ref.shape[1]
    lat = qa_ref.shape[2]
    nw = heads * nope
    ppt = SAMPLE_SUB // PAGE
    sets = ((cb0, sc0), (cb1, sc1))

    def page_copies(page, slot, p):
        return (pltpu.make_async_copy(poolc_ref.at[page], cbuf.at[slot, p], sems.at[0, slot]),
                pltpu.make_async_copy(poolrt_ref.at[page], rbuf.at[slot, p], sems.at[1, slot]))

    def start_pages(b, ci, slot, p0, p1):
        for p in range(p0, p1):
            for cpy in page_copies(pt_ref[b, ci * cp + p], slot, p):
                cpy.start()

    def wait_chunk(slot):
        for p in range(cp):
            for cpy in page_copies(0, slot, p):
                cpy.wait()

    @pl.when(n == 0)
    def _():
        start_pages(0, 0, 0, 0, cp)

    lhs = jnp.concatenate([wukt_ref[...], qa_ref[0]], axis=0)
    qr = qr_ref[0]

    def scores(c, rope_scores):
        tk = c.shape[0]
        big = _nt(lhs, c)
        knt = big[:nw].reshape(heads, nope, tk)
        r = lax.rsqrt(jnp.sum(knt * knt, axis=1) / nope + RMS_EPS)
        return big[nw:] * jnp.concatenate([r] * t_new, axis=0) + rope_scores

    def update(s, c, carry):
        m, l, acc = carry
        m_new = jnp.maximum(m, jnp.max(s, axis=-1, keepdims=True))
        alpha = jnp.exp2(m - m_new)
        p = jnp.exp2(s - m_new)
        l = alpha * l + jnp.sum(p, axis=-1, keepdims=True)
        acc = alpha * acc + _nn(p.astype(BF16), c)
        return m_new, l, acc

    def absorb(slot, carry):
        cbs, scs = sets[slot]
        return update(scs[...], cbs[...], carry)

    def score_chunk(slot, nxt_b, nxt_ci, carry, absorb_other):
        cbs, scs = sets[slot]
        nt = cp // ppt

        def tiles(j0, j1):
            for j in range(j0, j1):
                c = cbuf[slot, j * ppt:(j + 1) * ppt].reshape(SAMPLE_SUB, lat).astype(BF16)
                krt = jnp.concatenate([rbuf[slot, j * ppt + p] for p in range(ppt)], axis=1).astype(BF16)
                cbs[j * SAMPLE_SUB:(j + 1) * SAMPLE_SUB, :] = c
                scs[:, j * SAMPLE_SUB:(j + 1) * SAMPLE_SUB] = scores(c, _nn(qr, krt))

        wait_chunk(slot)
        start_pages(nxt_b, nxt_ci, 1 - slot, 0, cp // 2)
        tiles(0, nt // 2)
        if absorb_other:
            carry = absorb(1 - slot, carry)
        start_pages(nxt_b, nxt_ci, 1 - slot, cp // 2, cp)
        tiles(nt // 2, nt)
        return carry

    def pair(k, carry):
        carry = score_chunk(1, n, 2 * k + 2, carry, True)
        return score_chunk(0, n, 2 * k + 3, carry, True)

    init = (jnp.full((rows, 1), NEG_INF, F32), jnp.zeros((rows, 1), F32), jnp.zeros((rows, lat), F32))
    carry = score_chunk(0, n, 1, init, False)
    carry = lax.fori_loop(0, (n_chunks - 2) // 2, pair, carry)
    carry = score_chunk(1, jnp.minimum(n + 1, n_batch - 1), 0, carry, True)
    carry = absorb(1, carry)

    @pl.when(n == n_batch - 1)
    def _():
        wait_chunk(0)

    c = cnew_ref[0].astype(BF16)
    tk = c.shape[0]
    t_of_row = lax.broadcasted_iota(jnp.int32, (rows, tk), 0) // heads
    col = lax.broadcasted_iota(jnp.int32, (rows, tk), 1)
    s = jnp.where(col <= t_of_row, scores(c, _nt(qr, rnew_ref[0].astype(BF16))), NEG_INF)
    m, l, acc = update(s, c, carry)
    o_ref[0] = acc / l


def _unabsorb_kernel(o_ref, wuv_ref, out_ref, *, heads):
    out_ref[...] = jnp.concatenate(
        [_nn(o_ref[h], wuv_ref[h]) for h in range(heads)], axis=-1).astype(BF16)


def _oproj_kernel(attp_ref, atts_ref, w_ref, h_ref, gate_ref, o_ref, *, n_prompt_blocks):
    att = jnp.where(pl.program_id(0) < n_prompt_blocks, attp_ref[...], atts_ref[...])
    y = _nn(att, w_ref[...])
    tm, d = y.shape
    gate = gate_ref[...]
    nb = gate.shape[0]
    o_ref[...] = h_ref[...] + (y.reshape(tm // nb, nb, d) * gate[None]).reshape(tm, d)


def _router_kernel(h_ref, g_ref, sh_ref, sc_ref, whi_ref, wlo_ref, b_ref, u_ref, idx_ref, gate_ref):
    u = _norm_mod(h_ref[...], g_ref[...], sh_ref[...], sc_ref[...])
    uhi = u.astype(BF16)
    ulo = (u - uhi.astype(F32)).astype(BF16)
    u_ref[...] = _pack_bf16_pairs(u)
    whi = whi_ref[...]
    logits = _nt(whi, uhi) + _nt(whi, ulo) + _nt(wlo_ref[...], uhi) + b_ref[...]
    ne = logits.shape[0]
    eid = lax.broadcasted_iota(jnp.int32, logits.shape, 0)
    work = logits
    vals, idxs = [], []
    for _ in range(TOP_K):
        m = jnp.max(work, axis=0, keepdims=True)
        idx = jnp.min(jnp.where(work == m, eid, ne), axis=0, keepdims=True)
        vals.append(m)
        idxs.append(idx)
        work = jnp.where(eid == idx, -jnp.inf, work)
    es = [jnp.exp(v - vals[0]) for v in vals]
    den = es[0] + es[1] + es[2] + es[3]
    idx_ref[...] = jnp.concatenate(idxs, axis=0)
    gate_ref[...] = jnp.concatenate([e / den for e in es], axis=0)


def _expert_kernel(be_ref, nrows_ref, nused_ref, x_ref, wup_ref, bup_ref, wdn_ref, bdn_ref, y_ref, wup_bf, wdn_bf,
                   *, ff):
    i = pl.program_id(0)
    n_here = nrows_ref[i]
    prev = be_ref[jnp.maximum(i - 1, 0)]
    new_expert = (i == 0) | (be_ref[i] != prev)

    @pl.when(new_expert & (n_here > 0))
    def _():
        wup_bf[...] = wup_ref[0].astype(BF16)
        wdn_bf[...] = wdn_ref[0].astype(BF16)

    for r0 in range(0, MOE_TM, MOE_SUB):
        @pl.when(n_here > r0)
        def _(r0=r0):
            xa, xb = _unpack_bf16_pairs(x_ref[r0:r0 + MOE_SUB, :])
            x = jnp.concatenate([xa.astype(BF16), xb.astype(BF16)], axis=1)
            fc = 512
            acc = jnp.zeros((MOE_SUB, wdn_bf.shape[1]), F32)
            for c0 in range(0, ff, fc):
                glu = _nn(x, wup_bf[:, c0:c0 + fc]) + bup_ref[0, :, c0:c0 + fc]
                lin = _nn(x, wup_bf[:, ff + c0:ff + c0 + fc]) + bup_ref[0, :, ff + c0:ff + c0 + fc]
                glu = jnp.minimum(glu, SWIGLU_LIMIT)
                lin = jnp.clip(lin, -SWIGLU_LIMIT, SWIGLU_LIMIT)
                act = glu * jax.nn.sigmoid(SWIGLU_ALPHA * glu) * (lin + 1.0)
                acc = acc + _nn(act.astype(BF16), wdn_bf[c0:c0 + fc, :])
            y_ref[r0:r0 + MOE_SUB, :] = _pack_bf16_pairs(acc + bdn_ref[0])

        @pl.when(n_here <= r0)
        def _(r0=r0):
            y_ref[r0:r0 + MOE_SUB, :] = jnp.zeros((MOE_SUB, y_ref.shape[1]), jnp.uint32)


def _combine_kernel(h_ref, y_ref, gk_ref, gate_ref, o_ref):
    gk = gk_ref[...]
    ya, yb = _unpack_bf16_pairs(y_ref[0])
    ya, yb = ya * gk[:, 0:1], yb * gk[:, 0:1]
    for k in range(1, TOP_K):
        a, b = _unpack_bf16_pairs(y_ref[k])
        ya, yb = ya + a * gk[:, k:k + 1], yb + b * gk[:, k:k + 1]
    y = jnp.concatenate([ya, yb], axis=1)
    tm, d = y.shape
    gate = gate_ref[...]
    nb = gate.shape[0]
    o_ref[...] = h_ref[...] + (y.reshape(tm // nb, nb, d) * gate[None]).reshape(tm, d)


def _shared_kv_kernel(h_ref, g_ref, sh_ref, sc_ref, w_ref, gk_ref, k_ref, v_ref, khm_ref, vhm_ref, *, kvh, hd):
    u = _norm_mod(h_ref[...], g_ref[...], sh_ref[...], sc_ref[...]).astype(BF16)
    a = _nn(u, w_ref[...])
    kw = kvh * hd
    gk = gk_ref[...]
    ks = []
    for j in range(kw // LANES):
        x = a[:, j * LANES:(j + 1) * LANES]
        ks.append(x * _group64_rscale(x) * gk[:, j * LANES:(j + 1) * LANES])
    k = jnp.concatenate(ks, axis=-1)
    v = a[:, kw:2 * kw]
    k_ref[...] = k
    v_ref[...] = v
    for hh in range(kvh):
        khm_ref[hh] = k[:, hh * hd:(hh + 1) * hd].astype(BF16)
        vhm_ref[hh] = v[:, hh * hd:(hh + 1) * hd].astype(BF16)


def _swa_q_kernel(h_ref, g_ref, sh_ref, sc_ref, w_ref, gq_ref, q_ref, *, heads, hd, scale):
    u = _norm_mod(h_ref[...], g_ref[...], sh_ref[...], sc_ref[...]).astype(BF16)
    a = _nn(u, w_ref[...])
    gq = gq_ref[...]
    for j in range(heads * hd // LANES):
        x = a[:, j * LANES:(j + 1) * LANES]
        y = x * _group64_rscale(x) * gq[:, j * LANES:(j + 1) * LANES] * scale
        q_ref[2 * j] = y[:, :hd].astype(BF16)
        q_ref[2 * j + 1] = y[:, hd:].astype(BF16)


def _softmax_pv(s, v):
    m = jnp.max(s, axis=-1, keepdims=True)
    e = jnp.exp(s - m)
    return _nn(e.astype(BF16), v) / jnp.sum(e, axis=-1, keepdims=True)


def _swa_prompt_kernel(bias_ref, q_ref, kp_ref, kc_ref, vp_ref, vc_ref, o_ref, *, heads, kvh, tq):
    group = heads // kvh
    rows = group * tq
    not_first = lax.broadcasted_iota(jnp.int32, (WINDOW + tq, q_ref.shape[2]), 0) > 0
    outs = []
    for g in range(kvh):
        k = jnp.where(not_first, jnp.concatenate([kp_ref[g], kc_ref[g]], axis=0), 0)
        v = jnp.where(not_first, jnp.concatenate([vp_ref[g], vc_ref[g]], axis=0), 0)
        q = q_ref[g * group:(g + 1) * group].reshape(rows, q_ref.shape[2])
        o = _softmax_pv(_nt(q, k) + bias_ref[g], v)
        for j in range(group):
            outs.append(o[j * tq:(j + 1) * tq])
    o_ref[...] = jnp.concatenate(outs, axis=-1).astype(BF16)


def _swa_sample_kernel(bias_ref, q_ref, k_ref, v_ref, o_ref):
    bias = bias_ref[...]
    for b in range(q_ref.shape[0]):
        o_ref[b] = _softmax_pv(_nt(q_ref[b], k_ref[b]) + bias, v_ref[b]).astype(BF16)


def _rope_tables(pos, rope):
    half = rope // 2
    inv = ROPE_THETA ** (-jnp.arange(half, dtype=F32) / half)
    ang = pos.astype(F32)[:, None] * inv[None, :]
    cos, sin = jnp.cos(ang), jnp.sin(ang)
    return jnp.concatenate([cos, cos], axis=-1), jnp.concatenate([sin, sin], axis=-1)


def _rot_cols(w):
    half = w.shape[-1] // 2
    return jnp.concatenate([-w[..., half:], w[..., :half]], axis=-1)


def _swap_halves(g):
    half = g.shape[-1] // 2
    return jnp.concatenate([g[..., half:], g[..., :half]], axis=-1)


def _moe(h, tab, layer, col0, g_ffn, w_router, b_router, w_up, b_up, w_down, b_down, *, seq, nb_prompt):
    m_rows, d = h.shape
    ne = w_router.shape[1]
    ff = w_down.shape[2]
    nb = tab.shape[2]
    n_tok_blocks = m_rows // TM

    def tab_spec(col):
        return pl.BlockSpec((None, None, nb, d),
                            lambda i: (layer, jnp.minimum(i * TM // seq, nb_prompt), 0, col))

    wr_t = w_router.T
    wr_hi = wr_t.astype(BF16)
    wr_lo = (wr_t - wr_hi.astype(F32)).astype(BF16)
    u, idx_t, gate_t = pl.pallas_call(
        _router_kernel,
        grid=(n_tok_blocks,),
        in_specs=[pl.BlockSpec((TM, d), lambda i: (i, 0)),
                  pl.BlockSpec((1, d), lambda i: (0, 0)),
                  tab_spec(col0), tab_spec(col0 + 1),
                  pl.BlockSpec((ne, d), lambda i: (0, 0)),
                  pl.BlockSpec((ne, d), lambda i: (0, 0)),
                  pl.BlockSpec((ne, 1), lambda i: (0, 0))],
        out_specs=[pl.BlockSpec((TM, d // 2), lambda i: (i, 0)),
                   pl.BlockSpec((TOP_K, TM), lambda i: (0, i)),
                   pl.BlockSpec((TOP_K, TM), lambda i: (0, i))],
        out_shape=[jax.ShapeDtypeStruct((m_rows, d // 2), jnp.uint32),
                   jax.ShapeDtypeStruct((TOP_K, m_rows), jnp.int32),
                   jax.ShapeDtypeStruct((TOP_K, m_rows), F32)],
        compiler_params=_params("parallel"),
        name="router",
    )(h, g_ffn.reshape(1, d), tab, tab, wr_hi, wr_lo, b_router.reshape(ne, 1))

    a = m_rows * TOP_K
    e_flat = idx_t.T.reshape(a)
    order = jnp.argsort(e_flat).astype(jnp.int32)
    rank = jnp.argsort(order).astype(jnp.int32)
    onehot = e_flat[:, None] == jnp.arange(ne, dtype=jnp.int32)[None, :]
    counts = jnp.sum(onehot, axis=0, dtype=jnp.int32)
    padded = (counts + MOE_TM - 1) // MOE_TM * MOE_TM
    pad_end = jnp.cumsum(padded)
    shift = (pad_end - padded) - (jnp.cumsum(counts) - counts)
    pos = rank + jnp.sum(jnp.where(onehot, shift[None, :], 0), axis=1)
    n_blk = -(-a // MOE_TM) + ne
    n_slots = n_blk * MOE_TM
    blk_start = jnp.arange(n_blk, dtype=jnp.int32) * MOE_TM
    blk_expert = jnp.minimum(jnp.sum(pad_end[None, :] <= blk_start[:, None], axis=1), ne - 1).astype(jnp.int32)
    n_used = (pad_end[-1] // MOE_TM).astype(jnp.int32).reshape(1)
    blk_rows = jnp.clip((pad_end - padded + counts)[blk_expert] - blk_start, 0, MOE_TM)
    blk_rows = jnp.where(blk_start < pad_end[-1], blk_rows, 0).astype(jnp.int32)
    slot_sorted = jnp.arange(n_slots, dtype=jnp.int32) - jnp.repeat(shift[blk_expert], MOE_TM)
    slot_tok = order[jnp.clip(slot_sorted, 0, a - 1)] // TOP_K
    xs = jnp.take(u, slot_tok, axis=0, mode="clip")

    def live(i, nused):
        return jnp.minimum(i, nused[0] - 1)

    yb = pl.pallas_call(
        functools.partial(_expert_kernel, ff=ff),
        grid_spec=pltpu.PrefetchScalarGridSpec(
            num_scalar_prefetch=3,
            grid=(n_blk,),
            in_specs=[pl.BlockSpec((MOE_TM, d // 2), lambda i, be, nr, nu: (live(i, nu), 0)),
                      pl.BlockSpec((None, 1, d, 2 * ff), lambda i, be, nr, nu: (layer, be[live(i, nu)], 0, 0)),
                      pl.BlockSpec((None, 1, 1, 2 * ff), lambda i, be, nr, nu: (layer, be[live(i, nu)], 0, 0)),
                      pl.BlockSpec((None, 1, ff, d), lambda i, be, nr, nu: (layer, be[live(i, nu)], 0, 0)),
                      pl.BlockSpec((None, 1, 1, d), lambda i, be, nr, nu: (layer, be[live(i, nu)], 0, 0))],
            out_specs=pl.BlockSpec((MOE_TM, d // 2), lambda i, be, nr, nu: (i, 0)),
            scratch_shapes=[pltpu.VMEM((d, 2 * ff), BF16), pltpu.VMEM((ff, d), BF16)]),
        out_shape=jax.ShapeDtypeStruct((n_slots, d // 2), jnp.uint32),
        compiler_params=_params("arbitrary"),
        name="experts",
    )(blk_expert, blk_rows, n_used, xs, w_up, b_up.reshape(b_up.shape[0], ne, 1, 2 * ff), w_down,
      b_down.reshape(b_down.shape[0], ne, 1, d))

    ysel = jnp.take(yb, pos.reshape(m_rows, TOP_K).T.reshape(a), axis=0, mode="clip").reshape(TOP_K, m_rows, d // 2)
    return pl.pallas_call(
        _combine_kernel,
        grid=(n_tok_blocks,),
        in_specs=[pl.BlockSpec((TM, d), lambda i: (i, 0)),
                  pl.BlockSpec((TOP_K, TM, d // 2), lambda i: (0, i, 0)),
                  pl.BlockSpec((TM, TOP_K), lambda i: (i, 0)),
                  tab_spec(col0 + 2)],
        out_specs=pl.BlockSpec((TM, d), lambda i: (i, 0)),
        out_shape=jax.ShapeDtypeStruct((m_rows, d), F32),
        compiler_params=_params("parallel"),
        name="moe_combine",
    )(h, ysel, gate_t.T, tab)


def kernel(x_prompt, x_sample, c_prompt, c_sample, cache_mla_latent, cache_mla_krope, state_win_k, state_win_v, page_table, w_mod, b_mod, g_attn, g_ffn, w_mla_down, g_mla_q_lora, g_mla_kv_lora, w_mla_uq, w_mla_uk, w_mla_uv, g_mla_qn, g_mla_qr, g_mla_kn, g_mla_kr, w_mla_o, w_kvmod, b_kvmod, g_kv_norm, w_kv, g_swa_k, w_swa_q, g_swa_q, swa_sinks, w_swa_o, w_router, b_router, w_up, b_up, w_down, b_down):
    nbp, seq, d = x_prompt.shape
    nbs, t_new, _ = x_sample.shape
    q_lora = g_mla_q_lora.shape[1]
    kv_lora = g_mla_kv_lora.shape[1]
    heads, nope = w_mla_uk.shape[2], w_mla_uk.shape[3]
    rope = g_mla_qr.shape[1]
    vdim = w_mla_uv.shape[3]
    n_pages = page_table.shape[1]
    past = n_pages * PAGE
    swa_heads = swa_sinks.shape[1]
    hd = g_swa_k.shape[0]
    kvh = w_kv.shape[1] // (2 * hd)
    assert nope + 2 * rope == LANES and 2 * vdim == LANES and 2 * hd == LANES
    assert seq % TM == 0 and TM % nbs == 0 and (nbs * t_new) % TM == 0 and nbs % 8 == 0
    assert w_mod.shape[0] == 2 and n_pages % (2 * SAMPLE_PAGES) == 0

    mp = nbp * seq
    ms = nbs * t_new
    m_rows = mp + ms
    n_tok_blocks = m_rows // TM
    n_prompt_blocks = mp // TM
    blocks_per_seq = seq // TM

    def tab_idx(i):
        return jnp.minimum(i * TM // seq, nbp)

    def tok(i):
        return (i, 0)

    def const2(i):
        return (0, 0)

    h0 = jnp.concatenate([x_prompt.reshape(mp, d), x_sample.transpose(1, 0, 2).reshape(ms, d)], axis=0)
    c_all = jnp.concatenate([c_prompt, c_sample], axis=0)
    tab = _adaln_table(c_all, w_mod, b_mod, nbp)
    tab_kv = _adaln_table(c_all, w_kvmod[None], b_kvmod[None], nbp)

    def tab_spec(layer, col):
        return pl.BlockSpec((None, None, nbs, d), lambda i: (layer, tab_idx(i), 0, col))

    cos_p, sin_p = _rope_tables(jnp.arange(seq), rope)
    cos_s, sin_s = _rope_tables(past + jnp.arange(t_new), rope)
    cos_tab = jnp.concatenate([cos_p, jnp.repeat(cos_s, nbs, axis=0)], axis=0)
    sin_tab = jnp.concatenate([sin_p, jnp.repeat(sin_s, nbs, axis=0)], axis=0)

    def pos_blk(i):
        return (jnp.where(i < n_prompt_blocks, i % blocks_per_seq, blocks_per_seq + i - n_prompt_blocks), 0)

    wd = w_mla_down[0]
    w_down_ext = jnp.concatenate([wd, _rot_cols(wd[:, q_lora + kv_lora:])], axis=1).astype(BF16)
    nd = w_down_ext.shape[1]
    cq, ckv, kr = pl.pallas_call(
        functools.partial(_mla_down_kernel, q_lora=q_lora, kv_lora=kv_lora, rope=rope),
        grid=(n_tok_blocks,),
        in_specs=[pl.BlockSpec((TM, d), tok), pl.BlockSpec((1, d), const2), tab_spec(0, 0), tab_spec(0, 1),
                  pl.BlockSpec((d, nd), const2), pl.BlockSpec((1, q_lora), const2),
                  pl.BlockSpec((1, kv_lora), const2), pl.BlockSpec((1, rope), const2),
                  pl.BlockSpec((1, rope), const2), pl.BlockSpec((TM, rope), pos_blk),
                  pl.BlockSpec((TM, rope), pos_blk)],
        out_specs=[pl.BlockSpec((TM, q_lora), tok), pl.BlockSpec((TM, kv_lora), tok),
                   pl.BlockSpec((TM, rope), tok)],
        out_shape=[jax.ShapeDtypeStruct((m_rows, q_lora), BF16),
                   jax.ShapeDtypeStruct((m_rows, kv_lora), F32),
                   jax.ShapeDtypeStruct((m_rows, rope), F32)],
        compiler_params=_params("parallel"),
        name="mla_down",
    )(h0, g_attn[0].reshape(1, d), tab, tab, w_down_ext, g_mla_q_lora, g_mla_kv_lora,
      g_mla_kr, _swap_halves(g_mla_kr), cos_tab, sin_tab)

    wq = w_mla_uq[0]
    w_q_cat = jnp.concatenate([wq, _rot_cols(wq[..., nope:])], axis=-1).reshape(q_lora, heads * LANES).astype(BF16)
    gq_cat = jnp.concatenate([g_mla_qn[0], g_mla_qr[0], _swap_halves(g_mla_qr[0])]).reshape(1, LANES)
    q_tab = jnp.concatenate([jnp.ones((cos_tab.shape[0], nope), F32), cos_tab, sin_tab], axis=1)
    mla_scale = (nope + rope) ** -0.5 * LOG2_E
    q_cat = pl.pallas_call(
        functools.partial(_q_up_kernel, heads=heads, nope=nope, rope=rope, scale=mla_scale),
        grid=(n_tok_blocks,),
        in_specs=[pl.BlockSpec((TM, q_lora), tok), pl.BlockSpec((q_lora, heads * LANES), const2),
                  pl.BlockSpec((1, LANES), const2), pl.BlockSpec((TM, LANES), pos_blk)],
        out_specs=pl.BlockSpec((heads, TM, LANES), lambda i: (0, i, 0)),
        out_shape=jax.ShapeDtypeStruct((heads, m_rows, LANES), BF16),
        compiler_params=_params("parallel"),
        name="q_up",
    )(cq, w_q_cat, gq_cat, q_tab)

    wuk = w_mla_uk[0]
    wuv = w_mla_uv[0]
    wuk_t = wuk.reshape(kv_lora, heads * nope).T.astype(BF16)
    tk = TM
    nk = seq // tk
    kt, v_p = pl.pallas_call(
        functools.partial(_kv_up_kernel, heads=heads, nope=nope, rope=rope),
        grid=(n_prompt_blocks,),
        in_specs=[pl.BlockSpec((TM, kv_lora), tok), pl.BlockSpec((TM, rope), tok),
                  pl.BlockSpec((heads * nope, kv_lora), const2), pl.BlockSpec((nope, 1), const2),
                  pl.BlockSpec((kv_lora, heads * vdim), const2), pl.BlockSpec((rope, rope), const2)],
        out_specs=[pl.BlockSpec((1, heads, 1, LANES, tk), lambda i: (i // nk, 0, i % nk, 0, 0)),
                   pl.BlockSpec((TM, heads * vdim), tok)],
        out_shape=[jax.ShapeDtypeStruct((nbp, heads, nk, LANES, tk), BF16),
                   jax.ShapeDtypeStruct((mp, heads * vdim), BF16)],
        compiler_params=_params("parallel"),
        name="kv_up",
    )(ckv, kr, wuk_t, g_mla_kn[0].reshape(nope, 1), wuv.reshape(kv_lora, heads * vdim).astype(BF16),
      jnp.eye(rope, dtype=BF16))

    tq = tk
    nq = seq // tq
    att_p = pl.pallas_call(
        functools.partial(_mla_flash_kernel, tq=tq, tk=tk, vdim=vdim),
        grid=(nbp, heads // 2, nq),
        in_specs=[pl.BlockSpec((2, tq, LANES), lambda b, hp, qi: (hp, b * nq + qi, 0)),
                  pl.BlockSpec((1, 2, nk, LANES, tk), lambda b, hp, qi: (b, hp, 0, 0, 0)),
                  pl.BlockSpec((seq, LANES), lambda b, hp, qi: (b, hp))],
        out_specs=pl.BlockSpec((tq, LANES), lambda b, hp, qi: (b * nq + qi, hp)),
        out_shape=jax.ShapeDtypeStruct((mp, heads * vdim), BF16),
        compiler_params=_params("parallel", "parallel", "arbitrary"),
        name="mla_prompt_attention",
    )(q_cat, kt, v_p)

    sample_blk = mp // ms
    qa, qr = pl.pallas_call(
        functools.partial(_q_absorb_kernel, heads=heads, nope=nope, rope=rope),
        grid=(1,),
        in_specs=[pl.BlockSpec((heads, ms, LANES), lambda i: (0, sample_blk, 0)),
                  pl.BlockSpec((heads, nope, kv_lora), lambda i: (0, 0, 0)),
                  pl.BlockSpec((1, nope), const2)],
        out_specs=[pl.BlockSpec((heads, ms, kv_lora), lambda i: (0, 0, 0)),
                   pl.BlockSpec((heads, ms, rope), lambda i: (0, 0, 0))],
        out_shape=[jax.ShapeDtypeStruct((heads, ms, kv_lora), BF16),
                   jax.ShapeDtypeStruct((heads, ms, rope), BF16)],
        compiler_params=_params("arbitrary"),
        name="q_absorb",
    )(q_cat, wuk.transpose(1, 2, 0).astype(BF16), g_mla_kn)
    rows = t_new * heads

    def per_batch(x):
        return x.reshape(heads, t_new, nbs, -1).transpose(2, 1, 0, 3).reshape(nbs, rows, -1)

    def new_rows(x):
        x = x.reshape(t_new, nbs, -1).transpose(1, 0, 2)
        return jnp.pad(x, ((0, 0), (0, 8 - t_new), (0, 0)))

    n_chunks = n_pages // SAMPLE_PAGES
    sample_tokens = SAMPLE_PAGES * PAGE
    o_lat = pl.pallas_call(
        functools.partial(_mla_sample_kernel, heads=heads, nope=nope, n_chunks=n_chunks, n_batch=nbs,
                          t_new=t_new),
        grid_spec=pltpu.PrefetchScalarGridSpec(
            num_scalar_prefetch=1,
            grid=(nbs,),
            in_specs=[pl.BlockSpec((1, rows, kv_lora), lambda n, pt: (n, 0, 0)),
                      pl.BlockSpec((1, rows, rope), lambda n, pt: (n, 0, 0)),
                      pl.BlockSpec((heads * nope, kv_lora), lambda n, pt: (0, 0)),
                      pl.BlockSpec((1, 8, kv_lora), lambda n, pt: (n, 0, 0)),
                      pl.BlockSpec((1, 8, rope), lambda n, pt: (n, 0, 0)),
                      pl.BlockSpec(memory_space=pl.ANY),
                      pl.BlockSpec(memory_space=pl.ANY)],
            out_specs=pl.BlockSpec((1, rows, kv_lora), lambda n, pt: (n, 0, 0)),
            scratch_shapes=[pltpu.VMEM((2, SAMPLE_PAGES, PAGE, kv_lora), F32),
                            pltpu.VMEM((2, SAMPLE_PAGES, rope, PAGE), F32),
                            pltpu.VMEM((sample_tokens, kv_lora), BF16),
                            pltpu.VMEM((sample_tokens, kv_lora), BF16),
                            pltpu.VMEM((rows, sample_tokens), F32),
                            pltpu.VMEM((rows, sample_tokens), F32),
                            pltpu.SemaphoreType.DMA((2, 2))]),
        out_shape=jax.ShapeDtypeStruct((nbs, rows, kv_lora), F32),
        compiler_params=_params("arbitrary"),
        name="mla_sample_attention",
    )(page_table, per_batch(qa), per_batch(qr), wuk_t, new_rows(ckv[mp:]), new_rows(kr[mp:]),
      cache_mla_latent[0], cache_mla_krope[0].transpose(0, 2, 1))
    o_lat = o_lat.reshape(nbs, t_new, heads, kv_lora).transpose(2, 1, 0, 3).reshape(heads, ms, kv_lora)
    att_s = pl.pallas_call(
        functools.partial(_unabsorb_kernel, heads=heads),
        grid=(1,),
        in_specs=[pl.BlockSpec((heads, ms, kv_lora), lambda i: (0, 0, 0)),
                  pl.BlockSpec((heads, kv_lora, vdim), lambda i: (0, 0, 0))],
        out_specs=pl.BlockSpec((ms, heads * vdim), const2),
        out_shape=jax.ShapeDtypeStruct((ms, heads * vdim), BF16),
        compiler_params=_params("arbitrary"),
        name="unabsorb",
    )(o_lat.astype(BF16), wuv.transpose(1, 0, 2).astype(BF16))

    def oproj(att_prompt, att_sample, w, h, layer):
        return pl.pallas_call(
            functools.partial(_oproj_kernel, n_prompt_blocks=n_prompt_blocks),
            grid=(n_tok_blocks,),
            in_specs=[pl.BlockSpec((TM, w.shape[0]), lambda i: (jnp.minimum(i, n_prompt_blocks - 1), 0)),
                      pl.BlockSpec((TM, w.shape[0]), lambda i: (jnp.maximum(i - n_prompt_blocks, 0), 0)),
                      pl.BlockSpec(w.shape, const2),
                      pl.BlockSpec((TM, d), tok), tab_spec(layer, 2)],
            out_specs=pl.BlockSpec((TM, d), tok),
            out_shape=jax.ShapeDtypeStruct((m_rows, d), F32),
            compiler_params=_params("parallel"),
            name="oproj",
        )(att_prompt, att_sample, w.astype(BF16), h, tab)

    h = oproj(att_p, att_s, w_mla_o[0], h0, 0)
    h1 = _moe(h, tab, 0, 3, g_ffn[0], w_router[0], b_router[0], w_up, b_up, w_down, b_down,
              seq=seq, nb_prompt=nbp)

    kw = kvh * hd
    k_all, v_all, k_hm, v_hm = pl.pallas_call(
        functools.partial(_shared_kv_kernel, kvh=kvh, hd=hd),
        grid=(n_tok_blocks,),
        in_specs=[pl.BlockSpec((TM, d), tok), pl.BlockSpec((1, d), const2), tab_spec(0, 0), tab_spec(0, 1),
                  pl.BlockSpec((d, 2 * kw), const2), pl.BlockSpec((1, kw), const2)],
        out_specs=[pl.BlockSpec((TM, kw), tok), pl.BlockSpec((TM, kw), tok),
                   pl.BlockSpec((kvh, TM, hd), lambda i: (0, i, 0)),
                   pl.BlockSpec((kvh, TM, hd), lambda i: (0, i, 0))],
        out_shape=[jax.ShapeDtypeStruct((m_rows, kw), F32), jax.ShapeDtypeStruct((m_rows, kw), F32),
                   jax.ShapeDtypeStruct((kvh, m_rows, hd), BF16),
                   jax.ShapeDtypeStruct((kvh, m_rows, hd), BF16)],
        compiler_params=_params("parallel"),
        name="shared_kv",
    )(h1, g_kv_norm.reshape(1, d), tab_kv, tab_kv, w_kv.astype(BF16), jnp.tile(g_swa_k, kvh).reshape(1, kw))

    swa_scale = hd ** -0.5
    q_hm = pl.pallas_call(
        functools.partial(_swa_q_kernel, heads=swa_heads, hd=hd, scale=swa_scale),
        grid=(n_tok_blocks,),
        in_specs=[pl.BlockSpec((TM, d), tok), pl.BlockSpec((1, d), const2), tab_spec(1, 0), tab_spec(1, 1),
                  pl.BlockSpec((d, swa_heads * hd), const2), pl.BlockSpec((1, swa_heads * hd), const2)],
        out_specs=pl.BlockSpec((swa_heads, TM, hd), lambda i: (0, i, 0)),
        out_shape=jax.ShapeDtypeStruct((swa_heads, m_rows, hd), BF16),
        compiler_params=_params("parallel"),
        name="swa_q",
    )(h1, g_attn[1].reshape(1, d), tab, tab, w_swa_q[0].astype(BF16),
      jnp.tile(g_swa_q[0], swa_heads).reshape(1, swa_heads * hd))

    group = swa_heads // kvh
    slopes = 2.0 ** (-8.0 * jnp.arange(1, swa_heads + 1, dtype=F32) / swa_heads)
    sinks = swa_sinks[0].astype(F32)

    def bias_table(dist, valid, sink_col, head_of_row):
        b = jnp.where(valid, -slopes[head_of_row][:, None] * dist.astype(F32), NEG_INF)
        return jnp.where(sink_col, sinks[head_of_row][:, None], b)

    tq = WINDOW
    wblocks = seq // tq
    r = jnp.arange(group * tq)
    col = jnp.arange(WINDOW + tq)
    dist = (r % tq)[:, None] + WINDOW - col[None, :]
    inside = (dist >= 0) & (dist < WINDOW)
    bias_p = jnp.stack([
        jnp.stack([bias_table(dist, inside & ok[None, :], (col == 0)[None, :], g * group + r // tq)
                   for g in range(kvh)])
        for ok in (col >= 0, col >= WINDOW)])
    att_p = pl.pallas_call(
        functools.partial(_swa_prompt_kernel, heads=swa_heads, kvh=kvh, tq=tq),
        grid=(nbp, wblocks),
        in_specs=[pl.BlockSpec((None, kvh, group * tq, WINDOW + tq), lambda b, i: (jnp.where(i == 0, 1, 0), 0, 0, 0)),
                  pl.BlockSpec((swa_heads, tq, hd), lambda b, i: (0, b * wblocks + i, 0)),
                  pl.BlockSpec((kvh, WINDOW, hd), lambda b, i: (0, jnp.maximum(b * wblocks + i - 1, 0), 0)),
                  pl.BlockSpec((kvh, tq, hd), lambda b, i: (0, b * wblocks + i, 0)),
                  pl.BlockSpec((kvh, WINDOW, hd), lambda b, i: (0, jnp.maximum(b * wblocks + i - 1, 0), 0)),
                  pl.BlockSpec((kvh, tq, hd), lambda b, i: (0, b * wblocks + i, 0))],
        out_specs=pl.BlockSpec((tq, swa_heads * hd), lambda b, i: (b * wblocks + i, 0)),
        out_shape=jax.ShapeDtypeStruct((mp, swa_heads * hd), BF16),
        compiler_params=_params("parallel", "arbitrary"),
        name="swa_prompt_attention",
    )(bias_p, q_hm, k_hm, k_hm, v_hm, v_hm)

    tpad = 8
    nkg = WINDOW + tpad

    def stack_keys(win, new_hm):
        new = new_hm[:, mp:].reshape(kvh, t_new, nbs, hd).transpose(2, 0, 1, 3)
        new = jnp.pad(new, ((0, 0), (0, 0), (0, tpad - t_new), (0, 0)))
        return jnp.concatenate([win.transpose(0, 2, 1, 3).astype(BF16), new], axis=2).reshape(nbs, kvh * nkg, hd)

    rs = jnp.arange(t_new * swa_heads)
    t_of, h_of = rs // swa_heads, rs % swa_heads
    cs = jnp.arange(kvh * nkg)
    j = cs % nkg
    dist_s = jnp.where(j < WINDOW, WINDOW + t_of[:, None] - j[None, :], t_of[:, None] - (j[None, :] - WINDOW))
    own = (cs // nkg)[None, :] == (h_of // group)[:, None]
    ok_s = own & (dist_s >= 0) & (dist_s < WINDOW) & ((j < WINDOW + t_new)[None, :])
    bias_s = bias_table(dist_s, ok_s, own & ((j == WINDOW + t_new)[None, :]), h_of)
    q_s = q_hm[:, mp:].reshape(swa_heads, t_new, nbs, hd).transpose(2, 1, 0, 3).reshape(nbs, t_new * swa_heads, hd)
    nbi = 8
    att_s = pl.pallas_call(
        _swa_sample_kernel,
        grid=(nbs // nbi,),
        in_specs=[pl.BlockSpec(bias_s.shape, const2),
                  pl.BlockSpec((nbi, t_new * swa_heads, hd), lambda i: (i, 0, 0)),
                  pl.BlockSpec((nbi, kvh * nkg, hd), lambda i: (i, 0, 0)),
                  pl.BlockSpec((nbi, kvh * nkg, hd), lambda i: (i, 0, 0))],
        out_specs=pl.BlockSpec((nbi, t_new * swa_heads, hd), lambda i: (i, 0, 0)),
        out_shape=jax.ShapeDtypeStruct((nbs, t_new * swa_heads, hd), BF16),
        compiler_params=_params("parallel"),
        name="swa_sample_attention",
    )(bias_s, q_s, stack_keys(state_win_k, k_hm), stack_keys(state_win_v, v_hm))
    att_s = att_s.reshape(nbs, t_new, swa_heads * hd).transpose(1, 0, 2).reshape(ms, swa_heads * hd)

    h = oproj(att_p, att_s, w_swa_o[0], h1, 1)
    h2 = _moe(h, tab, 1, 3, g_ffn[1], w_router[1], b_router[1], w_up, b_up, w_down, b_down,
              seq=seq, nb_prompt=nbp)

    def sample_major(x):
        return x.reshape(t_new, nbs, -1).transpose(1, 0, 2)

    y_prompt = h2[:mp].reshape(nbp, seq, d)
    y_sample = sample_major(h2[mp:])
    lat_p = ckv[:mp].reshape(1, nbp, seq, kv_lora)
    kr_p = kr[:mp].reshape(1, nbp, seq, rope)
    lat_s = sample_major(ckv[mp:])[None]
    kr_s = sample_major(kr[mp:])[None]
    k_p = k_all[:mp].reshape(nbp, seq, kvh, hd)
    v_p4 = v_all[:mp].reshape(nbp, seq, kvh, hd)
    k_n = sample_major(k_all[mp:]).reshape(nbs, t_new, kvh, hd)
    v_n = sample_major(v_all[mp:]).reshape(nbs, t_new, kvh, hd)
    win_k_s = jnp.concatenate([state_win_k, k_n], axis=1)[:, -WINDOW:]
    win_v_s = jnp.concatenate([state_win_v, v_n], axis=1)[:, -WINDOW:]
    return (y_prompt, y_sample, lat_p, kr_p, lat_s, kr_s,
            k_p[:, -WINDOW:], v_p4[:, -WINDOW:], win_k_s, win_v_s)
```

```python
import functools

import jax
import jax.numpy as jnp
from jax import lax
from jax.experimental import pallas as pl
from jax.experimental.pallas import tpu as pltpu

F32 = jnp.float32
BF16 = jnp.bfloat16

RMS_EPS = 1e-6
NEG_INF = -1e30
ROPE_THETA = 10000.0
LOG2_E = 1.4426950408889634
PAGE = 128
WINDOW = 128
TOP_K = 4
SWIGLU_ALPHA = 1.702
SWIGLU_LIMIT = 7.0
LANES = 128
VMEM_LIMIT = 56 * 1024 * 1024

TM = 512
MOE_TM = 512
MOE_SUB = 256
FLASH_ROWS = 256
SAMPLE_PAGES = 16
SAMPLE_SUB = 256


def _nn(a, b):
    return jnp.dot(a, b, preferred_element_type=F32)


def _nt(a, b):
    return lax.dot_general(a, b, (((1,), (1,)), ((), ())), preferred_element_type=F32)


def _params(*sem):
    return pltpu.CompilerParams(dimension_semantics=sem, vmem_limit_bytes=VMEM_LIMIT)


def _norm_mod(h, g, shift, scale):
    tm, d = h.shape
    nb = shift.shape[0]
    y = h * lax.rsqrt(jnp.mean(h * h, axis=-1, keepdims=True) + RMS_EPS) * g
    y = y.reshape(tm // nb, nb, d) * (1.0 + scale[None]) + shift[None]
    return y.reshape(tm, d)


def _pack_bf16_pairs(x):
    k = x.shape[1] // 2
    r = x.astype(BF16).astype(F32)
    hi = lax.bitcast_convert_type(r[:, :k], jnp.uint32) & jnp.uint32(0xFFFF0000)
    lo = lax.bitcast_convert_type(r[:, k:], jnp.uint32) >> 16
    return hi | lo


def _unpack_bf16_pairs(w):
    return (lax.bitcast_convert_type(w & jnp.uint32(0xFFFF0000), F32),
            lax.bitcast_convert_type(w << 16, F32))


def _group64_rscale(x):
    lane = lax.broadcasted_iota(jnp.int32, x.shape, 1)
    x2 = x * x
    lo = jnp.sum(jnp.where(lane < 64, x2, 0.0), axis=-1, keepdims=True)
    hi = jnp.sum(jnp.where(lane >= 64, x2, 0.0), axis=-1, keepdims=True)
    return jnp.where(lane < 64, lax.rsqrt(lo / 64.0 + RMS_EPS), lax.rsqrt(hi / 64.0 + RMS_EPS))


def _adaln_kernel(c_ref, w_ref, b_ref, o_ref, *, nb_prompt):
    c = c_ref[...]
    a = (c * jax.nn.sigmoid(c)).astype(BF16)
    res = _nn(a, w_ref[...].astype(BF16)) + b_ref[...]
    nb = res.shape[0] - nb_prompt
    for b in range(nb_prompt):
        o_ref[b] = jnp.broadcast_to(res[b:b + 1], (nb, res.shape[1]))
    o_ref[nb_prompt] = res[nb_prompt:]


def _adaln_table(c, w, b, nb_prompt):
    n, d = c.shape
    nl, _, nout = w.shape
    nb = n - nb_prompt
    tn = 1024
    return pl.pallas_call(
        functools.partial(_adaln_kernel, nb_prompt=nb_prompt),
        grid=(nl, nout // tn),
        in_specs=[pl.BlockSpec((n, d), lambda l, j: (0, 0)),
                  pl.BlockSpec((None, d, tn), lambda l, j: (l, 0, j)),
                  pl.BlockSpec((None, 1, tn), lambda l, j: (l, 0, j))],
        out_specs=pl.BlockSpec((None, nb_prompt + 1, nb, tn), lambda l, j: (l, 0, 0, j)),
        out_shape=jax.ShapeDtypeStruct((nl, nb_prompt + 1, nb, nout), F32),
        compiler_params=_params("parallel", "parallel"),
        name="adaln",
    )(c, w, b.reshape(nl, 1, nout))


def _mla_down_kernel(hp_ref, hs_ref, g_ref, sh_ref, sc_ref, w_ref, gql_ref, gkvl_ref, gkr_ref, gkrp_ref,
                     cos_ref, sin_ref, cq_ref, ckvp_ref, ckvs_ref, krp_ref, krs_ref,
                     *, q_lora, kv_lora, rope, n_prompt_blocks):
    is_prompt = pl.program_id(0) < n_prompt_blocks
    h = jnp.where(is_prompt, hp_ref[...], hs_ref[...])
    u = _norm_mod(h, g_ref[...], sh_ref[...], sc_ref[...]).astype(BF16)
    a = _nn(u, w_ref[...])
    q = a[:, :q_lora]
    cq_ref[...] = (q * lax.rsqrt(jnp.mean(q * q, axis=-1, keepdims=True) + RMS_EPS)
                   * gql_ref[...]).astype(BF16)
    c = a[:, q_lora:q_lora + kv_lora]
    ckv = c * lax.rsqrt(jnp.mean(c * c, axis=-1, keepdims=True) + RMS_EPS) * gkvl_ref[...]
    o = q_lora + kv_lora
    raw = a[:, o:o + rope]
    rot = a[:, o + rope:o + 2 * rope]
    r = lax.rsqrt(jnp.mean(raw * raw, axis=-1, keepdims=True) + RMS_EPS)
    kr = r * (raw * gkr_ref[...] * cos_ref[...] + rot * gkrp_ref[...] * sin_ref[...])

    @pl.when(is_prompt)
    def _():
        ckvp_ref[...] = ckv
        krp_ref[...] = kr

    @pl.when(jnp.logical_not(is_prompt))
    def _():
        ckvs_ref[...] = ckv
        krs_ref[...] = kr


def _q_up_kernel(cq_ref, w_ref, gq_ref, tab_ref, o_ref, *, heads, nope, rope, scale):
    a = _nn(cq_ref[...], w_ref[...])
    tab = tab_ref[...]
    gq = gq_ref[...]
    for h in range(heads):
        x = a[:, h * LANES:(h + 1) * LANES]
        lane = lax.broadcasted_iota(jnp.int32, x.shape, 1)
        x2 = x * x
        ssn = jnp.sum(jnp.where(lane < nope, x2, 0.0), axis=-1, keepdims=True)
        ssr = jnp.sum(jnp.where((lane >= nope) & (lane < nope + rope), x2, 0.0), axis=-1, keepdims=True)
        r = jnp.where(lane < nope, lax.rsqrt(ssn / nope + RMS_EPS), lax.rsqrt(ssr / rope + RMS_EPS))
        o_ref[h] = (x * r * gq * tab * scale).astype(BF16)


def _kv_up_kernel(ckv_ref, kr_ref, wukt_ref, gkn_ref, wuv_ref, eye_ref, kt_ref, v_ref, *, heads, nope, rope):
    c = ckv_ref[...].astype(BF16)
    tm = c.shape[0]
    knt = _nt(wukt_ref[...], c).reshape(heads, nope, tm)
    ss = jnp.sum(knt * knt, axis=1, keepdims=True)
    kn = knt * lax.rsqrt(ss / nope + RMS_EPS) * gkn_ref[...][None]
    krt = _nt(eye_ref[...], kr_ref[...].astype(BF16)).astype(BF16)
    krt = jnp.broadcast_to(krt[None], (heads, rope, tm))
    kt_ref[0, :, 0, 0:nope, :] = kn.astype(BF16)
    kt_ref[0, :, 0, nope:nope + rope, :] = krt
    kt_ref[0, :, 0, nope + rope:nope + 2 * rope, :] = krt
    v_ref[...] = _nn(c, wuv_ref[...]).astype(BF16)


def _mla_flash_kernel(q_ref, kt_ref, v_ref, o_ref, *, tq, tk, vdim):
    qi = pl.program_id(2)
    n_full = (qi * tq) // tk
    parts = tq // FLASH_ROWS
    qs = [[q_ref[hh, r * FLASH_ROWS:(r + 1) * FLASH_ROWS, :] for r in range(parts)] for hh in range(2)]

    def step(j, carry, masked):
        start = pl.multiple_of(j * tk, tk)
        v = v_ref[pl.ds(start, tk), :]
        out = []
        for hh in range(2):
            kt = kt_ref[0, hh, j]
            for r in range(parts):
                m, l, acc = carry[hh * parts + r]
                s = _nn(qs[hh][r], kt)
                if masked:
                    row = qi * tq + r * FLASH_ROWS + lax.broadcasted_iota(jnp.int32, s.shape, 0)
                    col = j * tk + lax.broadcasted_iota(jnp.int32, s.shape, 1)
                    s = jnp.where(col <= row, s, NEG_INF)
                m_new = jnp.maximum(m, jnp.max(s, axis=-1, keepdims=True))
                alpha = jnp.exp2(m - m_new)
                p = jnp.exp2(s - m_new)
                l = alpha * l + jnp.sum(p, axis=-1, keepdims=True)
                acc = alpha * acc + _nn(p.astype(BF16), v)
                out.append((m_new, l, acc))
        return tuple(out)

    init = tuple((jnp.full((FLASH_ROWS, 1), NEG_INF, F32), jnp.zeros((FLASH_ROWS, 1), F32),
                  jnp.zeros((FLASH_ROWS, LANES), F32)) for _ in range(2 * parts))
    carry = lax.fori_loop(0, n_full, lambda j, c: step(j, c, False), init)
    carry = step(n_full, carry, True)
    lane = lax.broadcasted_iota(jnp.int32, (FLASH_ROWS, LANES), 1)
    for r in range(parts):
        (_, l0, acc0), (_, l1, acc1) = carry[r], carry[parts + r]
        o_ref[r * FLASH_ROWS:(r + 1) * FLASH_ROWS, :] = jnp.where(lane < vdim, acc0 / l0, acc1 / l1).astype(BF16)


def _q_absorb_kernel(q_ref, wuk_ref, gkn_ref, qa_ref, qr_ref, *, heads, nope, rope):
    for h in range(heads):
        q = q_ref[h].astype(F32)
        qg = (q[:, :nope] * gkn_ref[...]).astype(BF16)
        qa_ref[h] = _nn(qg, wuk_ref[h]).astype(BF16)
        qr_ref[h] = (q[:, nope:nope + rope] + q[:, nope + rope:nope + 2 * rope]).astype(BF16)


def _mla_sample_kernel(pt_ref, qa_ref, qr_ref, wukt_ref, cnew_ref, rnew_ref, poolc_ref, poolrt_ref,
                       o_ref, cbuf, rbuf, cb0, cb1, sc0, sc1, sems, *, heads, nope, n_chunks, n_batch, t_new):
    n = pl.program_id(0)
    cp = cbuf.shape[1]
    rows = qa_ref.shape[1]
    lat = qa_ref.shape[2]
    nw = heads * nope
    ppt = SAMPLE_SUB // PAGE
    sets = ((cb0, sc0), (cb1, sc1))

    def page_copies(page, slot, p):
        return (pltpu.make_async_copy(poolc_ref.at[page], cbuf.at[slot, p], sems.at[0, slot]),
                pltpu.make_async_copy(poolrt_ref.at[page], rbuf.at[slot, p], sems.at[1, slot]))

    def start_pages(b, ci, slot, p0, p1):
        for p in range(p0, p1):
            for cpy in page_copies(pt_ref[b, ci * cp + p], slot, p):
                cpy.start()

    def wait_chunk(slot):
        for p in range(cp):
            for cpy in page_copies(0, slot, p):
                cpy.wait()

    @pl.when(n == 0)
    def _():
        start_pages(0, 0, 0, 0, cp)

    lhs = jnp.concatenate([wukt_ref[...], qa_ref[0]], axis=0)
    qr = qr_ref[0]

    def scores(c, rope_scores):
        tk = c.shape[0]
        big = _nt(lhs, c)
        knt = big[:nw].reshape(heads, nope, tk)
        r = lax.rsqrt(jnp.sum(knt * knt, axis=1) / nope + RMS_EPS)
        return big[nw:] * jnp.concatenate([r] * t_new, axis=0) + rope_scores

    def update(s, c, carry):
        m, l, acc = carry
        m_new = jnp.maximum(m, jnp.max(s, axis=-1, keepdims=True))
        alpha = jnp.exp2(m - m_new)
        p = jnp.exp2(s - m_new)
        l = alpha * l + jnp.sum(p, axis=-1, keepdims=True)
        acc = alpha * acc + _nn(p.astype(BF16), c)
        return m_new, l, acc

    def absorb(slot, carry):
        cbs, scs = sets[slot]
        return update(scs[...], cbs[...], carry)

    def score_chunk(slot, nxt_b, nxt_ci, carry, absorb_other):
        cbs, scs = sets[slot]
        nt = cp // ppt

        def tiles(j0, j1):
            for j in range(j0, j1):
                c = cbuf[slot, j * ppt:(j + 1) * ppt].reshape(SAMPLE_SUB, lat).astype(BF16)
                krt = jnp.concatenate([rbuf[slot, j * ppt + p] for p in range(ppt)], axis=1).astype(BF16)
                cbs[j * SAMPLE_SUB:(j + 1) * SAMPLE_SUB, :] = c
                scs[:, j * SAMPLE_SUB:(j + 1) * SAMPLE_SUB] = scores(c, _nn(qr, krt))

        wait_chunk(slot)
        start_pages(nxt_b, nxt_ci, 1 - slot, 0, cp // 2)
        tiles(0, nt // 2)
        if absorb_other:
            carry = absorb(1 - slot, carry)
        start_pages(nxt_b, nxt_ci, 1 - slot, cp // 2, cp)
        tiles(nt // 2, nt)
        return carry

    def pair(k, carry):
        carry = score_chunk(1, n, 2 * k + 2, carry, True)
        return score_chunk(0, n, 2 * k + 3, carry, True)

    init = (jnp.full((rows, 1), NEG_INF, F32), jnp.zeros((rows, 1), F32), jnp.zeros((rows, lat), F32))
    carry = score_chunk(0, n, 1, init, False)
    carry = lax.fori_loop(0, (n_chunks - 2) // 2, pair, carry)
    carry = score_chunk(1, jnp.minimum(n + 1, n_batch - 1), 0, carry, True)
    carry = absorb(1, carry)

    @pl.when(n == n_batch - 1)
    def _():
        wait_chunk(0)

    c = cnew_ref[0].astype(BF16)
    tk = c.shape[0]
    t_of_row = lax.broadcasted_iota(jnp.int32, (rows, tk), 0) // heads
    col = lax.broadcasted_iota(jnp.int32, (rows, tk), 1)
    s = jnp.where(col <= t_of_row, scores(c, _nt(qr, rnew_ref[0].astype(BF16))), NEG_INF)
    m, l, acc = update(s, c, carry)
    o_ref[0] = acc / l


def _unabsorb_kernel(o_ref, wuv_ref, out_ref, *, heads):
    out_ref[...] = jnp.concatenate(
        [_nn(o_ref[h], wuv_ref[h]) for h in range(heads)], axis=-1).astype(BF16)


def _oproj_kernel(attp_ref, atts_ref, w_ref, hp_ref, hs_ref, gate_ref, o_ref, *, n_prompt_blocks):
    is_prompt = pl.program_id(0) < n_prompt_blocks
    att = jnp.where(is_prompt, attp_ref[...], atts_ref[...])
    y = _nn(att, w_ref[...])
    tm, d = y.shape
    gate = gate_ref[...]
    nb = gate.shape[0]
    h = jnp.where(is_prompt, hp_ref[...], hs_ref[...])
    o_ref[...] = h + (y.reshape(tm // nb, nb, d) * gate[None]).reshape(tm, d)


def _router_kernel(h_ref, g_ref, sh_ref, sc_ref, whi_ref, wlo_ref, b_ref, u_ref, idx_ref, gate_ref):
    u = _norm_mod(h_ref[...], g_ref[...], sh_ref[...], sc_ref[...])
    uhi = u.astype(BF16)
    ulo = (u - uhi.astype(F32)).astype(BF16)
    u_ref[...] = u
    whi = whi_ref[...]
    logits = _nt(whi, uhi) + _nt(whi, ulo) + _nt(wlo_ref[...], uhi) + b_ref[...]
    ne = logits.shape[0]
    eid = lax.broadcasted_iota(jnp.int32, logits.shape, 0)
    work = logits
    vals, idxs = [], []
    for _ in range(TOP_K):
        m = jnp.max(work, axis=0, keepdims=True)
        idx = jnp.min(jnp.where(work == m, eid, ne), axis=0, keepdims=True)
        vals.append(m)
        idxs.append(idx)
        work = jnp.where(eid == idx, -jnp.inf, work)
    es = [jnp.exp(v - vals[0]) for v in vals]
    den = es[0] + es[1] + es[2] + es[3]
    idx_ref[...] = jnp.concatenate(idxs, axis=0)
    gate_ref[...] = jnp.concatenate([e / den for e in es], axis=0)


def _rank_kernel(idx_ref, rank_ref, cnt_ref, carry, *, ne):
    @pl.when(pl.program_id(0) == 0)
    def _():
        carry[...] = jnp.zeros(carry.shape, F32)

    idx = idx_ref[...]
    tb = idx.shape[1]
    eid = lax.broadcasted_iota(jnp.int32, (ne, tb), 0)
    hits = [eid == idx[k:k + 1, :] for k in range(TOP_K)]
    oh = jnp.where(hits[0], 1.0, 0.0)
    for k in range(1, TOP_K):
        oh = oh + jnp.where(hits[k], 1.0, 0.0)
    upper = jnp.where(lax.broadcasted_iota(jnp.int32, (tb, tb), 0) < lax.broadcasted_iota(jnp.int32, (tb, tb), 1),
                      1.0, 0.0).astype(BF16)
    before = _nn(oh.astype(BF16), upper) + carry[...]
    rank_ref[...] = jnp.concatenate(
        [jnp.sum(jnp.where(h, before, 0.0), axis=0, keepdims=True) for h in hits], axis=0).astype(jnp.int32)
    carry[...] = carry[...] + jnp.sum(oh, axis=1, keepdims=True)
    cnt_ref[...] = carry[...]


def _expert_kernel(be_ref, nrows_ref, nused_ref, x_ref, wup_ref, bup_ref, wdn_ref, bdn_ref, y_ref, wup_bf, wdn_bf,
                   *, ff):
    i = pl.program_id(0)
    n_here = nrows_ref[i]
    prev = be_ref[jnp.maximum(i - 1, 0)]
    new_expert = (i == 0) | (be_ref[i] != prev)

    @pl.when(new_expert & (n_here > 0))
    def _():
        wup_bf[...] = wup_ref[0].astype(BF16)
        wdn_bf[...] = wdn_ref[0].astype(BF16)

    for r0 in range(0, MOE_TM, MOE_SUB):
        @pl.when(n_here > r0)
        def _(r0=r0):
            x = x_ref[r0:r0 + MOE_SUB, :].astype(BF16)
            fc = 512
            acc = jnp.zeros((MOE_SUB, wdn_bf.shape[1]), F32)
            for c0 in range(0, ff, fc):
                glu = _nn(x, wup_bf[:, c0:c0 + fc]) + bup_ref[0, :, c0:c0 + fc]
                lin = _nn(x, wup_bf[:, ff + c0:ff + c0 + fc]) + bup_ref[0, :, ff + c0:ff + c0 + fc]
                glu = jnp.minimum(glu, SWIGLU_LIMIT)
                lin = jnp.clip(lin, -SWIGLU_LIMIT, SWIGLU_LIMIT)
                act = glu * jax.nn.sigmoid(SWIGLU_ALPHA * glu) * (lin + 1.0)
                acc = acc + _nn(act.astype(BF16), wdn_bf[c0:c0 + fc, :])
            y_ref[r0:r0 + MOE_SUB, :] = _pack_bf16_pairs(acc + bdn_ref[0])

        @pl.when(n_here <= r0)
        def _(r0=r0):
            y_ref[r0:r0 + MOE_SUB, :] = jnp.zeros((MOE_SUB, y_ref.shape[1]), jnp.uint32)


def _combine_kernel(h_ref, y_ref, gk_ref, gate_ref, *o_refs, n_prompt_blocks):
    gk = gk_ref[...]
    ya, yb = _unpack_bf16_pairs(y_ref[0])
    ya, yb = ya * gk[:, 0:1], yb * gk[:, 0:1]
    for k in range(1, TOP_K):
        a, b = _unpack_bf16_pairs(y_ref[k])
        ya, yb = ya + a * gk[:, k:k + 1], yb + b * gk[:, k:k + 1]
    y = jnp.concatenate([ya, yb], axis=1)
    tm, d = y.shape
    gate = gate_ref[...]
    nb = gate.shape[0]
    res = h_ref[...] + (y.reshape(tm // nb, nb, d) * gate[None]).reshape(tm, d)
    if n_prompt_blocks is None:
        o_refs[0][...] = res
    else:
        @pl.when(pl.program_id(0) < n_prompt_blocks)
        def _():
            o_refs[0][...] = res

        @pl.when(pl.program_id(0) >= n_prompt_blocks)
        def _():
            o_refs[1][...] = res


def _shared_kv_kernel(h_ref, g_ref, sh_ref, sc_ref, w_ref, gk_ref, k_ref, v_ref, khm_ref, vhm_ref, *, kvh, hd):
    u = _norm_mod(h_ref[...], g_ref[...], sh_ref[...], sc_ref[...]).astype(BF16)
    a = _nn(u, w_ref[...])
    kw = kvh * hd
    gk = gk_ref[...]
    ks = []
    for j in range(kw // LANES):
        x = a[:, j * LANES:(j + 1) * LANES]
        ks.append(x * _group64_rscale(x) * gk[:, j * LANES:(j + 1) * LANES])
    k = jnp.concatenate(ks, axis=-1)
    v = a[:, kw:2 * kw]
    k_ref[...] = k
    v_ref[...] = v
    for hh in range(kvh):
        khm_ref[hh] = k[:, hh * hd:(hh + 1) * hd].astype(BF16)
        vhm_ref[hh] = v[:, hh * hd:(hh + 1) * hd].astype(BF16)


def _swa_q_kernel(h_ref, g_ref, sh_ref, sc_ref, w_ref, gq_ref, q_ref, *, heads, hd, scale):
    u = _norm_mod(h_ref[...], g_ref[...], sh_ref[...], sc_ref[...]).astype(BF16)
    a = _nn(u, w_ref[...])
    gq = gq_ref[...]
    for j in range(heads * hd // LANES):
        x = a[:, j * LANES:(j + 1) * LANES]
        y = x * _group64_rscale(x) * gq[:, j * LANES:(j + 1) * LANES] * scale
        q_ref[2 * j] = y[:, :hd].astype(BF16)
        q_ref[2 * j + 1] = y[:, hd:].astype(BF16)


def _softmax_pv(s, v):
    m = jnp.max(s, axis=-1, keepdims=True)
    e = jnp.exp(s - m)
    return _nn(e.astype(BF16), v) / jnp.sum(e, axis=-1, keepdims=True)


def _swa_prompt_kernel(bias_ref, q_ref, kp_ref, kc_ref, vp_ref, vc_ref, o_ref, *, heads, kvh, tq):
    group = heads // kvh
    rows = group * tq
    not_first = lax.broadcasted_iota(jnp.int32, (WINDOW + tq, q_ref.shape[2]), 0) > 0
    outs = []
    for g in range(kvh):
        k = jnp.where(not_first, jnp.concatenate([kp_ref[g], kc_ref[g]], axis=0), 0)
        v = jnp.where(not_first, jnp.concatenate([vp_ref[g], vc_ref[g]], axis=0), 0)
        q = q_ref[g * group:(g + 1) * group].reshape(rows, q_ref.shape[2])
        o = _softmax_pv(_nt(q, k) + bias_ref[g], v)
        for j in range(group):
            outs.append(o[j * tq:(j + 1) * tq])
    o_ref[...] = jnp.concatenate(outs, axis=-1).astype(BF16)


def _swa_sample_kernel(bias_ref, q_ref, k_ref, v_ref, o_ref):
    bias = bias_ref[...]
    for b in range(q_ref.shape[0]):
        o_ref[b] = _softmax_pv(_nt(q_ref[b], k_ref[b]) + bias, v_ref[b]).astype(BF16)


def _rope_tables(pos, rope):
    half = rope // 2
    inv = ROPE_THETA ** (-jnp.arange(half, dtype=F32) / half)
    ang = pos.astype(F32)[:, None] * inv[None, :]
    cos, sin = jnp.cos(ang), jnp.sin(ang)
    return jnp.concatenate([cos, cos], axis=-1), jnp.concatenate([sin, sin], axis=-1)


def _rot_cols(w):
    half = w.shape[-1] // 2
    return jnp.concatenate([-w[..., half:], w[..., :half]], axis=-1)


def _swap_halves(g):
    half = g.shape[-1] // 2
    return jnp.concatenate([g[..., half:], g[..., :half]], axis=-1)


def _moe(h, tab, layer, col0, g_ffn, w_router, b_router, w_up, b_up, w_down, b_down, *, seq, nb_prompt,
         split_rows=None):
    m_rows, d = h.shape
    ne = w_router.shape[1]
    ff = w_down.shape[2]
    nb = tab.shape[2]
    n_tok_blocks = m_rows // TM

    def tab_spec(col):
        return pl.BlockSpec((None, None, nb, d),
                            lambda i: (layer, jnp.minimum(i * TM // seq, nb_prompt), 0, col))

    wr_t = w_router.T
    wr_hi = wr_t.astype(BF16)
    wr_lo = (wr_t - wr_hi.astype(F32)).astype(BF16)
    u, idx_t, gate_t = pl.pallas_call(
        _router_kernel,
        grid=(n_tok_blocks,),
        in_specs=[pl.BlockSpec((TM, d), lambda i: (i, 0)),
                  pl.BlockSpec((1, d), lambda i: (0, 0)),
                  tab_spec(col0), tab_spec(col0 + 1),
                  pl.BlockSpec((ne, d), lambda i: (0, 0)),
                  pl.BlockSpec((ne, d), lambda i: (0, 0)),
                  pl.BlockSpec((ne, 1), lambda i: (0, 0))],
        out_specs=[pl.BlockSpec((TM, d), lambda i: (i, 0)),
                   pl.BlockSpec((TOP_K, TM), lambda i: (0, i)),
                   pl.BlockSpec((TOP_K, TM), lambda i: (0, i))],
        out_shape=[jax.ShapeDtypeStruct((m_rows, d), F32),
                   jax.ShapeDtypeStruct((TOP_K, m_rows), jnp.int32),
                   jax.ShapeDtypeStruct((TOP_K, m_rows), F32)],
        compiler_params=_params("parallel"),
        name="router",
    )(h, g_ffn.reshape(1, d), tab, tab, wr_hi, wr_lo, b_router.reshape(ne, 1))

    a = m_rows * TOP_K
    rank_t, cnt = pl.pallas_call(
        functools.partial(_rank_kernel, ne=ne),
        grid=(n_tok_blocks,),
        in_specs=[pl.BlockSpec((TOP_K, TM), lambda i: (0, i))],
        out_specs=[pl.BlockSpec((TOP_K, TM), lambda i: (0, i)), pl.BlockSpec((ne, 1), lambda i: (0, 0))],
        out_shape=[jax.ShapeDtypeStruct((TOP_K, m_rows), jnp.int32), jax.ShapeDtypeStruct((ne, 1), F32)],
        scratch_shapes=[pltpu.VMEM((ne, 1), F32)],
        compiler_params=_params("arbitrary"),
        name="moe_rank",
    )(idx_t)
    counts = cnt[:, 0].astype(jnp.int32)
    padded = (counts + MOE_TM - 1) // MOE_TM * MOE_TM
    pad_end = jnp.cumsum(padded)
    start = pad_end - padded
    experts = jnp.arange(ne, dtype=jnp.int32)
    pos_t = rank_t + jnp.sum(jnp.where(idx_t[..., None] == experts, start, 0), axis=-1)
    order = jnp.argsort(idx_t.T.reshape(a)).astype(jnp.int32)
    shift = start - (jnp.cumsum(counts) - counts)
    n_blk = -(-a // MOE_TM) + ne
    n_slots = n_blk * MOE_TM
    blk_start = jnp.arange(n_blk, dtype=jnp.int32) * MOE_TM
    blk_expert = jnp.minimum(jnp.sum(pad_end[None, :] <= blk_start[:, None], axis=1), ne - 1).astype(jnp.int32)
    n_used = (pad_end[-1] // MOE_TM).astype(jnp.int32).reshape(1)
    blk_rows = jnp.clip((pad_end - padded + counts)[blk_expert] - blk_start, 0, MOE_TM)
    blk_rows = jnp.where(blk_start < pad_end[-1], blk_rows, 0).astype(jnp.int32)
    slot_sorted = jnp.arange(n_slots, dtype=jnp.int32) - jnp.repeat(shift[blk_expert], MOE_TM)
    slot_tok = order[jnp.clip(slot_sorted, 0, a - 1)] // TOP_K
    xs = jnp.take(u, slot_tok, axis=0, mode="clip")

    def live(i, nused):
        return jnp.minimum(i, nused[0] - 1)

    yb = pl.pallas_call(
        functools.partial(_expert_kernel, ff=ff),
        grid_spec=pltpu.PrefetchScalarGridSpec(
            num_scalar_prefetch=3,
            grid=(n_blk,),
            in_specs=[pl.BlockSpec((MOE_TM, d), lambda i, be, nr, nu: (live(i, nu), 0)),
                      pl.BlockSpec((None, 1, d, 2 * ff), lambda i, be, nr, nu: (layer, be[live(i, nu)], 0, 0)),
                      pl.BlockSpec((None, 1, 1, 2 * ff), lambda i, be, nr, nu: (layer, be[live(i, nu)], 0, 0)),
                      pl.BlockSpec((None, 1, ff, d), lambda i, be, nr, nu: (layer, be[live(i, nu)], 0, 0)),
                      pl.BlockSpec((None, 1, 1, d), lambda i, be, nr, nu: (layer, be[live(i, nu)], 0, 0))],
            out_specs=pl.BlockSpec((MOE_TM, d // 2), lambda i, be, nr, nu: (i, 0)),
            scratch_shapes=[pltpu.VMEM((d, 2 * ff), BF16), pltpu.VMEM((ff, d), BF16)]),
        out_shape=jax.ShapeDtypeStruct((n_slots, d // 2), jnp.uint32),
        compiler_params=_params("arbitrary"),
        name="experts",
    )(blk_expert, blk_rows, n_used, xs, w_up, b_up.reshape(b_up.shape[0], ne, 1, 2 * ff), w_down,
      b_down.reshape(b_down.shape[0], ne, 1, d))

    ysel = jnp.take(yb, pos_t.reshape(a), axis=0, mode="clip").reshape(TOP_K, m_rows, d // 2)
    if split_rows is None:
        out_specs = pl.BlockSpec((TM, d), lambda i: (i, 0))
        out_shape = jax.ShapeDtypeStruct((m_rows, d), F32)
        npb = None
    else:
        npb = split_rows // TM
        out_specs = [pl.BlockSpec((TM, d), lambda i: (jnp.minimum(i, npb - 1), 0)),
                     pl.BlockSpec((TM, d), lambda i: (jnp.maximum(i - npb, 0), 0))]
        out_shape = [jax.ShapeDtypeStruct((split_rows, d), F32), jax.ShapeDtypeStruct((m_rows - split_rows, d), F32)]
    return pl.pallas_call(
        functools.partial(_combine_kernel, n_prompt_blocks=npb),
        grid=(n_tok_blocks,),
        in_specs=[pl.BlockSpec((TM, d), lambda i: (i, 0)),
                  pl.BlockSpec((TOP_K, TM, d // 2), lambda i: (0, i, 0)),
                  pl.BlockSpec((TM, TOP_K), lambda i: (i, 0)),
                  tab_spec(col0 + 2)],
        out_specs=out_specs,
        out_shape=out_shape,
        compiler_params=_params("arbitrary"),
        name="moe_combine",
    )(h, ysel, gate_t.T, tab)


def kernel(x_prompt, x_sample, c_prompt, c_sample, cache_mla_latent, cache_mla_krope, state_win_k, state_win_v, page_table, w_mod, b_mod, g_attn, g_ffn, w_mla_down, g_mla_q_lora, g_mla_kv_lora, w_mla_uq, w_mla_uk, w_mla_uv, g_mla_qn, g_mla_qr, g_mla_kn, g_mla_kr, w_mla_o, w_kvmod, b_kvmod, g_kv_norm, w_kv, g_swa_k, w_swa_q, g_swa_q, swa_sinks, w_swa_o, w_router, b_router, w_up, b_up, w_down, b_down):
    nbp, seq, d = x_prompt.shape
    nbs, t_new, _ = x_sample.shape
    q_lora = g_mla_q_lora.shape[1]
    kv_lora = g_mla_kv_lora.shape[1]
    heads, nope = w_mla_uk.shape[2], w_mla_uk.shape[3]
    rope = g_mla_qr.shape[1]
    vdim = w_mla_uv.shape[3]
    n_pages = page_table.shape[1]
    past = n_pages * PAGE
    swa_heads = swa_sinks.shape[1]
    hd = g_swa_k.shape[0]
    kvh = w_kv.shape[1] // (2 * hd)
    assert nope + 2 * rope == LANES and 2 * vdim == LANES and 2 * hd == LANES
    assert seq % TM == 0 and TM % nbs == 0 and (nbs * t_new) % TM == 0 and nbs % 8 == 0
    assert w_mod.shape[0] == 2 and n_pages % (2 * SAMPLE_PAGES) == 0

    mp = nbp * seq
    ms = nbs * t_new
    m_rows = mp + ms
    n_tok_blocks = m_rows // TM
    n_prompt_blocks = mp // TM
    blocks_per_seq = seq // TM

    def tab_idx(i):
        return jnp.minimum(i * TM // seq, nbp)

    def tok(i):
        return (i, 0)

    def const2(i):
        return (0, 0)

    h0_p = x_prompt.reshape(mp, d)
    h0_s = x_sample.transpose(1, 0, 2).reshape(ms, d)

    def prompt_blk(i):
        return (jnp.minimum(i, n_prompt_blocks - 1), 0)

    def sample_blk(i):
        return (jnp.maximum(i - n_prompt_blocks, 0), 0)

    c_all = jnp.concatenate([c_prompt, c_sample], axis=0)
    tab = _adaln_table(c_all, w_mod, b_mod, nbp)
    tab_kv = _adaln_table(c_all, w_kvmod[None], b_kvmod[None], nbp)

    def tab_spec(layer, col):
        return pl.BlockSpec((None, None, nbs, d), lambda i: (layer, tab_idx(i), 0, col))

    cos_p, sin_p = _rope_tables(jnp.arange(seq), rope)
    cos_s, sin_s = _rope_tables(past + jnp.arange(t_new), rope)
    cos_tab = jnp.concatenate([cos_p, jnp.repeat(cos_s, nbs, axis=0)], axis=0)
    sin_tab = jnp.concatenate([sin_p, jnp.repeat(sin_s, nbs, axis=0)], axis=0)

    def pos_blk(i):
        return (jnp.where(i < n_prompt_blocks, i % blocks_per_seq, blocks_per_seq + i - n_prompt_blocks), 0)

    wd = w_mla_down[0]
    w_down_ext = jnp.concatenate([wd, _rot_cols(wd[:, q_lora + kv_lora:])], axis=1).astype(BF16)
    nd = w_down_ext.shape[1]
    cq, ckv_p, ckv_s, kr_p, kr_s = pl.pallas_call(
        functools.partial(_mla_down_kernel, q_lora=q_lora, kv_lora=kv_lora, rope=rope,
                          n_prompt_blocks=n_prompt_blocks),
        grid=(n_tok_blocks,),
        in_specs=[pl.BlockSpec((TM, d), prompt_blk), pl.BlockSpec((TM, d), sample_blk),
                  pl.BlockSpec((1, d), const2), tab_spec(0, 0), tab_spec(0, 1),
                  pl.BlockSpec((d, nd), const2), pl.BlockSpec((1, q_lora), const2),
                  pl.BlockSpec((1, kv_lora), const2), pl.BlockSpec((1, rope), const2),
                  pl.BlockSpec((1, rope), const2), pl.BlockSpec((TM, rope), pos_blk),
                  pl.BlockSpec((TM, rope), pos_blk)],
        out_specs=[pl.BlockSpec((TM, q_lora), tok),
                   pl.BlockSpec((TM, kv_lora), prompt_blk), pl.BlockSpec((TM, kv_lora), sample_blk),
                   pl.BlockSpec((TM, rope), prompt_blk), pl.BlockSpec((TM, rope), sample_blk)],
        out_shape=[jax.ShapeDtypeStruct((m_rows, q_lora), BF16),
                   jax.ShapeDtypeStruct((mp, kv_lora), F32), jax.ShapeDtypeStruct((ms, kv_lora), F32),
                   jax.ShapeDtypeStruct((mp, rope), F32), jax.ShapeDtypeStruct((ms, rope), F32)],
        compiler_params=_params("arbitrary"),
        name="mla_down",
    )(h0_p, h0_s, g_attn[0].reshape(1, d), tab, tab, w_down_ext, g_mla_q_lora, g_mla_kv_lora,
      g_mla_kr, _swap_halves(g_mla_kr), cos_tab, sin_tab)

    wq = w_mla_uq[0]
    w_q_cat = jnp.concatenate([wq, _rot_cols(wq[..., nope:])], axis=-1).reshape(q_lora, heads * LANES).astype(BF16)
    gq_cat = jnp.concatenate([g_mla_qn[0], g_mla_qr[0], _swap_halves(g_mla_qr[0])]).reshape(1, LANES)
    q_tab = jnp.concatenate([jnp.ones((cos_tab.shape[0], nope), F32), cos_tab, sin_tab], axis=1)
    mla_scale = (nope + rope) ** -0.5 * LOG2_E
    q_cat = pl.pallas_call(
        functools.partial(_q_up_kernel, heads=heads, nope=nope, rope=rope, scale=mla_scale),
        grid=(n_tok_blocks,),
        in_specs=[pl.BlockSpec((TM, q_lora), tok), pl.BlockSpec((q_lora, heads * LANES), const2),
                  pl.BlockSpec((1, LANES), const2), pl.BlockSpec((TM, LANES), pos_blk)],
        out_specs=pl.BlockSpec((heads, TM, LANES), lambda i: (0, i, 0)),
        out_shape=jax.ShapeDtypeStruct((heads, m_rows, LANES), BF16),
        compiler_params=_params("parallel"),
        name="q_up",
    )(cq, w_q_cat, gq_cat, q_tab)

    wuk = w_mla_uk[0]
    wuv = w_mla_uv[0]
    wuk_t = wuk.reshape(kv_lora, heads * nope).T.astype(BF16)
    tk = TM
    nk = seq // tk
    kt, v_p = pl.pallas_call(
        functools.partial(_kv_up_kernel, heads=heads, nope=nope, rope=rope),
        grid=(n_prompt_blocks,),
        in_specs=[pl.BlockSpec((TM, kv_lora), tok), pl.BlockSpec((TM, rope), tok),
                  pl.BlockSpec((heads * nope, kv_lora), const2), pl.BlockSpec((nope, 1), const2),
                  pl.BlockSpec((kv_lora, heads * vdim), const2), pl.BlockSpec((rope, rope), const2)],
        out_specs=[pl.BlockSpec((1, heads, 1, LANES, tk), lambda i: (i // nk, 0, i % nk, 0, 0)),
                   pl.BlockSpec((TM, heads * vdim), tok)],
        out_shape=[jax.ShapeDtypeStruct((nbp, heads, nk, LANES, tk), BF16),
                   jax.ShapeDtypeStruct((mp, heads * vdim), BF16)],
        compiler_params=_params("parallel"),
        name="kv_up",
    )(ckv_p, kr_p, wuk_t, g_mla_kn[0].reshape(nope, 1), wuv.reshape(kv_lora, heads * vdim).astype(BF16),
      jnp.eye(rope, dtype=BF16))

    tq = tk
    nq = seq // tq
    att_p = pl.pallas_call(
        functools.partial(_mla_flash_kernel, tq=tq, tk=tk, vdim=vdim),
        grid=(nbp, heads // 2, nq),
        in_specs=[pl.BlockSpec((2, tq, LANES), lambda b, hp, qi: (hp, b * nq + qi, 0)),
                  pl.BlockSpec((1, 2, nk, LANES, tk), lambda b, hp, qi: (b, hp, 0, 0, 0)),
                  pl.BlockSpec((seq, LANES), lambda b, hp, qi: (b, hp))],
        out_specs=pl.BlockSpec((tq, LANES), lambda b, hp, qi: (b * nq + qi, hp)),
        out_shape=jax.ShapeDtypeStruct((mp, heads * vdim), BF16),
        compiler_params=_params("parallel", "parallel", "arbitrary"),
        name="mla_prompt_attention",
    )(q_cat, kt, v_p)

    first_sample_blk = mp // ms
    qa, qr = pl.pallas_call(
        functools.partial(_q_absorb_kernel, heads=heads, nope=nope, rope=rope),
        grid=(1,),
        in_specs=[pl.BlockSpec((heads, ms, LANES), lambda i: (0, first_sample_blk, 0)),
                  pl.BlockSpec((heads, nope, kv_lora), lambda i: (0, 0, 0)),
                  pl.BlockSpec((1, nope), const2)],
        out_specs=[pl.BlockSpec((heads, ms, kv_lora), lambda i: (0, 0, 0)),
                   pl.BlockSpec((heads, ms, rope), lambda i: (0, 0, 0))],
        out_shape=[jax.ShapeDtypeStruct((heads, ms, kv_lora), BF16),
                   jax.ShapeDtypeStruct((heads, ms, rope), BF16)],
        compiler_params=_params("arbitrary"),
        name="q_absorb",
    )(q_cat, wuk.transpose(1, 2, 0).astype(BF16), g_mla_kn)
    rows = t_new * heads

    def per_batch(x):
        return x.reshape(heads, t_new, nbs, -1).transpose(2, 1, 0, 3).reshape(nbs, rows, -1)

    def new_rows(x):
        x = x.reshape(t_new, nbs, -1).transpose(1, 0, 2)
        return jnp.pad(x, ((0, 0), (0, 8 - t_new), (0, 0)))

    n_chunks = n_pages // SAMPLE_PAGES
    sample_tokens = SAMPLE_PAGES * PAGE
    o_lat = pl.pallas_call(
        functools.partial(_mla_sample_kernel, heads=heads, nope=nope, n_chunks=n_chunks, n_batch=nbs,
                          t_new=t_new),
        grid_spec=pltpu.PrefetchScalarGridSpec(
            num_scalar_prefetch=1,
            grid=(nbs,),
            in_specs=[pl.BlockSpec((1, rows, kv_lora), lambda n, pt: (n, 0, 0)),
                      pl.BlockSpec((1, rows, rope), lambda n, pt: (n, 0, 0)),
                      pl.BlockSpec((heads * nope, kv_lora), lambda n, pt: (0, 0)),
                      pl.BlockSpec((1, 8, kv_lora), lambda n, pt: (n, 0, 0)),
                      pl.BlockSpec((1, 8, rope), lambda n, pt: (n, 0, 0)),
                      pl.BlockSpec(memory_space=pl.ANY),
                      pl.BlockSpec(memory_space=pl.ANY)],
            out_specs=pl.BlockSpec((1, rows, kv_lora), lambda n, pt: (n, 0, 0)),
            scratch_shapes=[pltpu.VMEM((2, SAMPLE_PAGES, PAGE, kv_lora), F32),
                            pltpu.VMEM((2, SAMPLE_PAGES, rope, PAGE), F32),
                            pltpu.VMEM((sample_tokens, kv_lora), BF16),
                            pltpu.VMEM((sample_tokens, kv_lora), BF16),
                            pltpu.VMEM((rows, sample_tokens), F32),
                            pltpu.VMEM((rows, sample_tokens), F32),
                            pltpu.SemaphoreType.DMA((2, 2))]),
        out_shape=jax.ShapeDtypeStruct((nbs, rows, kv_lora), F32),
        compiler_params=_params("arbitrary"),
        name="mla_sample_attention",
    )(page_table, per_batch(qa), per_batch(qr), wuk_t, new_rows(ckv_s), new_rows(kr_s),
      cache_mla_latent[0], cache_mla_krope[0].transpose(0, 2, 1))
    o_lat = o_lat.reshape(nbs, t_new, heads, kv_lora).transpose(2, 1, 0, 3).reshape(heads, ms, kv_lora)
    att_s = pl.pallas_call(
        functools.partial(_unabsorb_kernel, heads=heads),
        grid=(1,),
        in_specs=[pl.BlockSpec((heads, ms, kv_lora), lambda i: (0, 0, 0)),
                  pl.BlockSpec((heads, kv_lora, vdim), lambda i: (0, 0, 0))],
        out_specs=pl.BlockSpec((ms, heads * vdim), const2),
        out_shape=jax.ShapeDtypeStruct((ms, heads * vdim), BF16),
        compiler_params=_params("arbitrary"),
        name="unabsorb",
    )(o_lat.astype(BF16), wuv.transpose(1, 0, 2).astype(BF16))

    def oproj(att_prompt, att_sample, w, h_prompt, h_sample, sample_idx, layer):
        return pl.pallas_call(
            functools.partial(_oproj_kernel, n_prompt_blocks=n_prompt_blocks),
            grid=(n_tok_blocks,),
            in_specs=[pl.BlockSpec((TM, w.shape[0]), prompt_blk), pl.BlockSpec((TM, w.shape[0]), sample_blk),
                      pl.BlockSpec(w.shape, const2),
                      pl.BlockSpec((TM, d), prompt_blk), pl.BlockSpec((TM, d), sample_idx), tab_spec(layer, 2)],
            out_specs=pl.BlockSpec((TM, d), tok),
            out_shape=jax.ShapeDtypeStruct((m_rows, d), F32),
            compiler_params=_params("parallel"),
            name="oproj",
        )(att_prompt, att_sample, w.astype(BF16), h_prompt, h_sample, tab)

    def sample_blk_of_all(i):
        return (n_prompt_blocks + jnp.maximum(i - n_prompt_blocks, 0), 0)

    h = oproj(att_p, att_s, w_mla_o[0], h0_p, h0_s, sample_blk, 0)
    h1 = _moe(h, tab, 0, 3, g_ffn[0], w_router[0], b_router[0], w_up, b_up, w_down, b_down,
              seq=seq, nb_prompt=nbp)

    kw = kvh * hd
    k_all, v_all, k_hm, v_hm = pl.pallas_call(
        functools.partial(_shared_kv_kernel, kvh=kvh, hd=hd),
        grid=(n_tok_blocks,),
        in_specs=[pl.BlockSpec((TM, d), tok), pl.BlockSpec((1, d), const2), tab_spec(0, 0), tab_spec(0, 1),
                  pl.BlockSpec((d, 2 * kw), const2), pl.BlockSpec((1, kw), const2)],
        out_specs=[pl.BlockSpec((TM, kw), tok), pl.BlockSpec((TM, kw), tok),
                   pl.BlockSpec((kvh, TM, hd), lambda i: (0, i, 0)),
                   pl.BlockSpec((kvh, TM, hd), lambda i: (0, i, 0))],
        out_shape=[jax.ShapeDtypeStruct((m_rows, kw), F32), jax.ShapeDtypeStruct((m_rows, kw), F32),
                   jax.ShapeDtypeStruct((kvh, m_rows, hd), BF16),
                   jax.ShapeDtypeStruct((kvh, m_rows, hd), BF16)],
        compiler_params=_params("parallel"),
        name="shared_kv",
    )(h1, g_kv_norm.reshape(1, d), tab_kv, tab_kv, w_kv.astype(BF16), jnp.tile(g_swa_k, kvh).reshape(1, kw))

    swa_scale = hd ** -0.5
    q_hm = pl.pallas_call(
        functools.partial(_swa_q_kernel, heads=swa_heads, hd=hd, scale=swa_scale),
        grid=(n_tok_blocks,),
        in_specs=[pl.BlockSpec((TM, d), tok), pl.BlockSpec((1, d), const2), tab_spec(1, 0), tab_spec(1, 1),
                  pl.BlockSpec((d, swa_heads * hd), const2), pl.BlockSpec((1, swa_heads * hd), const2)],
        out_specs=pl.BlockSpec((swa_heads, TM, hd), lambda i: (0, i, 0)),
        out_shape=jax.ShapeDtypeStruct((swa_heads, m_rows, hd), BF16),
        compiler_params=_params("parallel"),
        name="swa_q",
    )(h1, g_attn[1].reshape(1, d), tab, tab, w_swa_q[0].astype(BF16),
      jnp.tile(g_swa_q[0], swa_heads).reshape(1, swa_heads * hd))

    group = swa_heads // kvh
    slopes = 2.0 ** (-8.0 * jnp.arange(1, swa_heads + 1, dtype=F32) / swa_heads)
    sinks = swa_sinks[0].astype(F32)

    def bias_table(dist, valid, sink_col, head_of_row):
        b = jnp.where(valid, -slopes[head_of_row][:, None] * dist.astype(F32), NEG_INF)
        return jnp.where(sink_col, sinks[head_of_row][:, None], b)

    tq = WINDOW
    wblocks = seq // tq
    r = jnp.arange(group * tq)
    col = jnp.arange(WINDOW + tq)
    dist = (r % tq)[:, None] + WINDOW - col[None, :]
    inside = (dist >= 0) & (dist < WINDOW)
    bias_p = jnp.stack([
        jnp.stack([bias_table(dist, inside & ok[None, :], (col == 0)[None, :], g * group + r // tq)
                   for g in range(kvh)])
        for ok in (col >= 0, col >= WINDOW)])
    att_p = pl.pallas_call(
        functools.partial(_swa_prompt_kernel, heads=swa_heads, kvh=kvh, tq=tq),
        grid=(nbp, wblocks),
        in_specs=[pl.BlockSpec((None, kvh, group * tq, WINDOW + tq), lambda b, i: (jnp.where(i == 0, 1, 0), 0, 0, 0)),
                  pl.BlockSpec((swa_heads, tq, hd), lambda b, i: (0, b * wblocks + i, 0)),
                  pl.BlockSpec((kvh, WINDOW, hd), lambda b, i: (0, jnp.maximum(b * wblocks + i - 1, 0), 0)),
                  pl.BlockSpec((kvh, tq, hd), lambda b, i: (0, b * wblocks + i, 0)),
                  pl.BlockSpec((kvh, WINDOW, hd), lambda b, i: (0, jnp.maximum(b * wblocks + i - 1, 0), 0)),
                  pl.BlockSpec((kvh, tq, hd), lambda b, i: (0, b * wblocks + i, 0))],
        out_specs=pl.BlockSpec((tq, swa_heads * hd), lambda b, i: (b * wblocks + i, 0)),
        out_shape=jax.ShapeDtypeStruct((mp, swa_heads * hd), BF16),
        compiler_params=_params("parallel", "arbitrary"),
        name="swa_prompt_attention",
    )(bias_p, q_hm, k_hm, k_hm, v_hm, v_hm)

    tpad = 8
    nkg = WINDOW + tpad

    def stack_keys(win, new_hm):
        new = new_hm[:, mp:].reshape(kvh, t_new, nbs, hd).transpose(2, 0, 1, 3)
        new = jnp.pad(new, ((0, 0), (0, 0), (0, tpad - t_new), (0, 0)))
        return jnp.concatenate([win.transpose(0, 2, 1, 3).astype(BF16), new], axis=2).reshape(nbs, kvh * nkg, hd)

    rs = jnp.arange(t_new * swa_heads)
    t_of, h_of = rs // swa_heads, rs % swa_heads
    cs = jnp.arange(kvh * nkg)
    j = cs % nkg
    dist_s = jnp.where(j < WINDOW, WINDOW + t_of[:, None] - j[None, :], t_of[:, None] - (j[None, :] - WINDOW))
    own = (cs // nkg)[None, :] == (h_of // group)[:, None]
    ok_s = own & (dist_s >= 0) & (dist_s < WINDOW) & ((j < WINDOW + t_new)[None, :])
    bias_s = bias_table(dist_s, ok_s, own & ((j == WINDOW + t_new)[None, :]), h_of)
    q_s = q_hm[:, mp:].reshape(swa_heads, t_new, nbs, hd).transpose(2, 1, 0, 3).reshape(nbs, t_new * swa_heads, hd)
    nbi = 8
    att_s = pl.pallas_call(
        _swa_sample_kernel,
        grid=(nbs // nbi,),
        in_specs=[pl.BlockSpec(bias_s.shape, const2),
                  pl.BlockSpec((nbi, t_new * swa_heads, hd), lambda i: (i, 0, 0)),
                  pl.BlockSpec((nbi, kvh * nkg, hd), lambda i: (i, 0, 0)),
                  pl.BlockSpec((nbi, kvh * nkg, hd), lambda i: (i, 0, 0))],
        out_specs=pl.BlockSpec((nbi, t_new * swa_heads, hd), lambda i: (i, 0, 0)),
        out_shape=jax.ShapeDtypeStruct((nbs, t_new * swa_heads, hd), BF16),
        compiler_params=_params("parallel"),
        name="swa_sample_attention",
    )(bias_s, q_s, stack_keys(state_win_k, k_hm), stack_keys(state_win_v, v_hm))
    att_s = att_s.reshape(nbs, t_new, swa_heads * hd).transpose(1, 0, 2).reshape(ms, swa_heads * hd)

    h = oproj(att_p, att_s, w_swa_o[0], h1, h1, sample_blk_of_all, 1)
    y_p, y_s = _moe(h, tab, 1, 3, g_ffn[1], w_router[1], b_router[1], w_up, b_up, w_down, b_down,
                    seq=seq, nb_prompt=nbp, split_rows=mp)

    def sample_major(x):
        return x.reshape(t_new, nbs, -1).transpose(1, 0, 2)

    def last_window(x):
        return x[:mp].reshape(nbp, seq, kvh, hd)[:, -WINDOW:]

    y_prompt = y_p.reshape(nbp, seq, d)
    y_sample = sample_major(y_s)
    lat_p = ckv_p.reshape(1, nbp, seq, kv_lora)
    krope_p = kr_p.reshape(1, nbp, seq, rope)
    lat_s = sample_major(ckv_s)[None]
    krope_s = sample_major(kr_s)[None]
    k_n = sample_major(k_all[mp:]).reshape(nbs, t_new, kvh, hd)
    v_n = sample_major(v_all[mp:]).reshape(nbs, t_new, kvh, hd)
    win_k_s = jnp.concatenate([state_win_k, k_n], axis=1)[:, -WINDOW:]
    win_v_s = jnp.concatenate([state_win_v, v_n], axis=1)[:, -WINDOW:]
    return (y_prompt, y_sample, lat_p, krope_p, lat_s, krope_s,
            last_window(k_all), last_window(v_all), win_k_s, win_v_s)
```

```python
import functools

import jax
import jax.numpy as jnp
from jax import lax
from jax.experimental import pallas as pl
from jax.experimental.pallas import tpu as pltpu

F32 = jnp.float32
BF16 = jnp.bfloat16

RMS_EPS = 1e-6
NEG_INF = -1e30
ROPE_THETA = 10000.0
LOG2_E = 1.4426950408889634
PAGE = 128
WINDOW = 128
TOP_K = 4
SWIGLU_ALPHA = 1.702
SWIGLU_LIMIT = 7.0
LANES = 128
VMEM_LIMIT = 56 * 1024 * 1024

TM = 512
MOE_TM = 512
MOE_SUB = 256
FLASH_ROWS = 256
SAMPLE_PAGES = 16
SAMPLE_SUB = 256


def _nn(a, b):
    return jnp.dot(a, b, preferred_element_type=F32)


def _nt(a, b):
    return lax.dot_general(a, b, (((1,), (1,)), ((), ())), preferred_element_type=F32)


def _params(*sem):
    return pltpu.CompilerParams(dimension_semantics=sem, vmem_limit_bytes=VMEM_LIMIT)


def _norm_mod(h, g, shift, scale):
    tm, d = h.shape
    nb = shift.shape[0]
    y = h * lax.rsqrt(jnp.mean(h * h, axis=-1, keepdims=True) + RMS_EPS) * g
    y = y.reshape(tm // nb, nb, d) * (1.0 + scale[None]) + shift[None]
    return y.reshape(tm, d)


def _pack_bf16_pairs(x):
    k = x.shape[1] // 2
    r = x.astype(BF16).astype(F32)
    hi = lax.bitcast_convert_type(r[:, :k], jnp.uint32) & jnp.uint32(0xFFFF0000)
    lo = lax.bitcast_convert_type(r[:, k:], jnp.uint32) >> 16
    return hi | lo


def _unpack_bf16_pairs(w):
    return (lax.bitcast_convert_type(w & jnp.uint32(0xFFFF0000), F32),
            lax.bitcast_convert_type(w << 16, F32))


def _group64_rscale(x):
    lane = lax.broadcasted_iota(jnp.int32, x.shape, 1)
    x2 = x * x
    lo = jnp.sum(jnp.where(lane < 64, x2, 0.0), axis=-1, keepdims=True)
    hi = jnp.sum(jnp.where(lane >= 64, x2, 0.0), axis=-1, keepdims=True)
    return jnp.where(lane < 64, lax.rsqrt(lo / 64.0 + RMS_EPS), lax.rsqrt(hi / 64.0 + RMS_EPS))


def _adaln_kernel(c_ref, w_ref, b_ref, o_ref, *, nb_prompt):
    c = c_ref[...]
    a = (c * jax.nn.sigmoid(c)).astype(BF16)
    res = _nn(a, w_ref[...].astype(BF16)) + b_ref[...]
    nb = res.shape[0] - nb_prompt
    for b in range(nb_prompt):
        o_ref[b] = jnp.broadcast_to(res[b:b + 1], (nb, res.shape[1]))
    o_ref[nb_prompt] = res[nb_prompt:]


def _adaln_table(c, w, b, nb_prompt):
    n, d = c.shape
    nl, _, nout = w.shape
    nb = n - nb_prompt
    tn = 1024
    return pl.pallas_call(
        functools.partial(_adaln_kernel, nb_prompt=nb_prompt),
        grid=(nl, nout // tn),
        in_specs=[pl.BlockSpec((n, d), lambda l, j: (0, 0)),
                  pl.BlockSpec((None, d, tn), lambda l, j: (l, 0, j)),
                  pl.BlockSpec((None, 1, tn), lambda l, j: (l, 0, j))],
        out_specs=pl.BlockSpec((None, nb_prompt + 1, nb, tn), lambda l, j: (l, 0, 0, j)),
        out_shape=jax.ShapeDtypeStruct((nl, nb_prompt + 1, nb, nout), F32),
        compiler_params=_params("parallel", "parallel"),
        name="adaln",
    )(c, w, b.reshape(nl, 1, nout))


def _mla_down_kernel(hp_ref, hs_ref, g_ref, sh_ref, sc_ref, w_ref, gql_ref, gkvl_ref, gkr_ref, gkrp_ref,
                     cos_ref, sin_ref, cq_ref, ckvp_ref, ckvs_ref, krp_ref, krs_ref,
                     *, q_lora, kv_lora, rope, n_prompt_blocks):
    is_prompt = pl.program_id(0) < n_prompt_blocks
    h = jnp.where(is_prompt, hp_ref[...], hs_ref[...])
    u = _norm_mod(h, g_ref[...], sh_ref[...], sc_ref[...]).astype(BF16)
    a = _nn(u, w_ref[...])
    q = a[:, :q_lora]
    cq_ref[...] = (q * lax.rsqrt(jnp.mean(q * q, axis=-1, keepdims=True) + RMS_EPS)
                   * gql_ref[...]).astype(BF16)
    c = a[:, q_lora:q_lora + kv_lora]
    ckv = c * lax.rsqrt(jnp.mean(c * c, axis=-1, keepdims=True) + RMS_EPS) * gkvl_ref[...]
    o = q_lora + kv_lora
    raw = a[:, o:o + rope]
    rot = a[:, o + rope:o + 2 * rope]
    r = lax.rsqrt(jnp.mean(raw * raw, axis=-1, keepdims=True) + RMS_EPS)
    kr = r * (raw * gkr_ref[...] * cos_ref[...] + rot * gkrp_ref[...] * sin_ref[...])

    @pl.when(is_prompt)
    def _():
        ckvp_ref[...] = ckv
        krp_ref[...] = kr

    @pl.when(jnp.logical_not(is_prompt))
    def _():
        ckvs_ref[...] = ckv
        krs_ref[...] = kr


def _q_up_kernel(cq_ref, w_ref, gq_ref, tab_ref, o_ref, *, heads, nope, rope, scale):
    a = _nn(cq_ref[...], w_ref[...])
    tab = tab_ref[...]
    gq = gq_ref[...]
    for h in range(heads):
        x = a[:, h * LANES:(h + 1) * LANES]
        lane = lax.broadcasted_iota(jnp.int32, x.shape, 1)
        x2 = x * x
        ssn = jnp.sum(jnp.where(lane < nope, x2, 0.0), axis=-1, keepdims=True)
        ssr = jnp.sum(jnp.where((lane >= nope) & (lane < nope + rope), x2, 0.0), axis=-1, keepdims=True)
        r = jnp.where(lane < nope, lax.rsqrt(ssn / nope + RMS_EPS), lax.rsqrt(ssr / rope + RMS_EPS))
        o_ref[h] = (x * r * gq * tab * scale).astype(BF16)


def _kv_up_kernel(ckv_ref, kr_ref, wukt_ref, gkn_ref, wuv_ref, eye_ref, kt_ref, v_ref, *, heads, nope, rope):
    c = ckv_ref[...].astype(BF16)
    tm = c.shape[0]
    knt = _nt(wukt_ref[...], c).reshape(heads, nope, tm)
    ss = jnp.sum(knt * knt, axis=1, keepdims=True)
    kn = knt * lax.rsqrt(ss / nope + RMS_EPS) * gkn_ref[...][None]
    krt = _nt(eye_ref[...], kr_ref[...].astype(BF16)).astype(BF16)
    krt = jnp.broadcast_to(krt[None], (heads, rope, tm))
    kt_ref[0, :, 0, 0:nope, :] = kn.astype(BF16)
    kt_ref[0, :, 0, nope:nope + rope, :] = krt
    kt_ref[0, :, 0, nope + rope:nope + 2 * rope, :] = krt
    v_ref[...] = _nn(c, wuv_ref[...]).astype(BF16)


def _mla_flash_kernel(q_ref, kt_ref, v_ref, o_ref, *, tq, tk, vdim):
    qi = pl.program_id(2)
    n_full = (qi * tq) // tk
    parts = tq // FLASH_ROWS
    qs = [[q_ref[hh, r * FLASH_ROWS:(r + 1) * FLASH_ROWS, :] for r in range(parts)] for hh in range(2)]

    def step(j, carry, masked):
        start = pl.multiple_of(j * tk, tk)
        v = v_ref[pl.ds(start, tk), :]
        out = []
        for hh in range(2):
            kt = kt_ref[0, hh, j]
            for r in range(parts):
                m, l, acc = carry[hh * parts + r]
                nk = (r + 1) * FLASH_ROWS if masked else tk
                s = _nn(qs[hh][r], kt[:, :nk])
                if masked:
                    row = r * FLASH_ROWS + lax.broadcasted_iota(jnp.int32, s.shape, 0)
                    col = lax.broadcasted_iota(jnp.int32, s.shape, 1)
                    s = jnp.where(col <= row, s, NEG_INF)
                m_new = jnp.maximum(m, jnp.max(s, axis=-1, keepdims=True))
                alpha = jnp.exp2(m - m_new)
                p = jnp.exp2(s - m_new)
                l = alpha * l + jnp.sum(p, axis=-1, keepdims=True)
                acc = alpha * acc + _nn(p.astype(BF16), v[:nk])
                out.append((m_new, l, acc))
        return tuple(out)

    init = tuple((jnp.full((FLASH_ROWS, 1), NEG_INF, F32), jnp.zeros((FLASH_ROWS, 1), F32),
                  jnp.zeros((FLASH_ROWS, LANES), F32)) for _ in range(2 * parts))
    carry = lax.fori_loop(0, n_full, lambda j, c: step(j, c, False), init)
    carry = step(n_full, carry, True)
    lane = lax.broadcasted_iota(jnp.int32, (FLASH_ROWS, LANES), 1)
    for r in range(parts):
        (_, l0, acc0), (_, l1, acc1) = carry[r], carry[parts + r]
        o_ref[r * FLASH_ROWS:(r + 1) * FLASH_ROWS, :] = jnp.where(lane < vdim, acc0 / l0, acc1 / l1).astype(BF16)


def _q_absorb_kernel(q_ref, wuk_ref, gkn_ref, qa_ref, qr_ref, *, heads, nope, rope):
    for h in range(heads):
        q = q_ref[h].astype(F32)
        qg = (q[:, :nope] * gkn_ref[...]).astype(BF16)
        qa_ref[h] = _nn(qg, wuk_ref[h]).astype(BF16)
        qr_ref[h] = (q[:, nope:nope + rope] + q[:, nope + rope:nope + 2 * rope]).astype(BF16)


def _mla_sample_kernel(pt_ref, qa_ref, qr_ref, wukt_ref, cnew_ref, rnew_ref, poolc_ref, poolrt_ref,
                       o_ref, cbuf, rbuf, cb0, cb1, sc0, sc1, sems, *, heads, nope, n_chunks, n_batch, t_new):
    n = pl.program_id(0)
    cp = cbuf.shape[1]
    rows = qa_ref.shape[1]
    lat = qa_ref.shape[2]
    nw = heads * nope
    ppt = SAMPLE_SUB // PAGE
    sets = ((cb0, sc0), (cb1, sc1))

    def page_copies(page, slot, p):
        return (pltpu.make_async_copy(poolc_ref.at[page], cbuf.at[slot, p], sems.at[0, slot]),
                pltpu.make_async_copy(poolrt_ref.at[page], rbuf.at[slot, p], sems.at[1, slot]))

    def start_pages(b, ci, slot, p0, p1):
        for p in range(p0, p1):
            for cpy in page_copies(pt_ref[b, ci * cp + p], slot, p):
                cpy.start()

    def wait_chunk(slot):
        for p in range(cp):
            for cpy in page_copies(0, slot, p):
                cpy.wait()

    @pl.when(n == 0)
    def _():
        start_pages(0, 0, 0, 0, cp)

    lhs = jnp.concatenate([wukt_ref[...], qa_ref[0]], axis=0)
    qr = qr_ref[0]

    def scores(c, rope_scores):
        tk = c.shape[0]
        big = _nt(lhs, c)
        knt = big[:nw].reshape(heads, nope, tk)
        r = lax.rsqrt(jnp.sum(knt * knt, axis=1) / nope + RMS_EPS)
        return big[nw:] * jnp.concatenate([r] * t_new, axis=0) + rope_scores

    def update(s, c, carry):
        m, l, acc = carry
        m_new = jnp.maximum(m, jnp.max(s, axis=-1, keepdims=True))
        alpha = jnp.exp2(m - m_new)
        p = jnp.exp2(s - m_new)
        l = alpha * l + jnp.sum(p, axis=-1, keepdims=True)
        acc = alpha * acc + _nn(p.astype(BF16), c)
        return m_new, l, acc

    def absorb(slot, carry):
        cbs, scs = sets[slot]
        return update(scs[...], cbs[...], carry)

    def score_chunk(slot, nxt_b, nxt_ci, carry, absorb_other):
        cbs, scs = sets[slot]
        nt = cp // ppt

        def tiles(j0, j1):
            for j in range(j0, j1):
                c = cbuf[slot, j * ppt:(j + 1) * ppt].reshape(SAMPLE_SUB, lat).astype(BF16)
                krt = jnp.concatenate([rbuf[slot, j * ppt + p] for p in range(ppt)], axis=1).astype(BF16)
                cbs[j * SAMPLE_SUB:(j + 1) * SAMPLE_SUB, :] = c
                scs[:, j * SAMPLE_SUB:(j + 1) * SAMPLE_SUB] = scores(c, _nn(qr, krt))

        wait_chunk(slot)
        start_pages(nxt_b, nxt_ci, 1 - slot, 0, cp // 2)
        tiles(0, nt // 2)
        if absorb_other:
            carry = absorb(1 - slot, carry)
        start_pages(nxt_b, nxt_ci, 1 - slot, cp // 2, cp)
        tiles(nt // 2, nt)
        return carry

    def pair(k, carry):
        carry = score_chunk(1, n, 2 * k + 2, carry, True)
        return score_chunk(0, n, 2 * k + 3, carry, True)

    init = (jnp.full((rows, 1), NEG_INF, F32), jnp.zeros((rows, 1), F32), jnp.zeros((rows, lat), F32))
    carry = score_chunk(0, n, 1, init, False)
    carry = lax.fori_loop(0, (n_chunks - 2) // 2, pair, carry)
    carry = score_chunk(1, jnp.minimum(n + 1, n_batch - 1), 0, carry, True)
    carry = absorb(1, carry)

    @pl.when(n == n_batch - 1)
    def _():
        wait_chunk(0)

    c = cnew_ref[0].astype(BF16)
    tk = c.shape[0]
    t_of_row = lax.broadcasted_iota(jnp.int32, (rows, tk), 0) // heads
    col = lax.broadcasted_iota(jnp.int32, (rows, tk), 1)
    s = jnp.where(col <= t_of_row, scores(c, _nt(qr, rnew_ref[0].astype(BF16))), NEG_INF)
    m, l, acc = update(s, c, carry)
    o_ref[0] = acc / l


def _unabsorb_kernel(o_ref, wuv_ref, out_ref, *, heads):
    out_ref[...] = jnp.concatenate(
        [_nn(o_ref[h], wuv_ref[h]) for h in range(heads)], axis=-1).astype(BF16)


def _oproj_router_kernel(attp_ref, atts_ref, w_ref, hp_ref, hs_ref, agate_ref, g_ref, sh_ref, sc_ref,
                         whi_ref, wlo_ref, b_ref, h_ref, u_ref, idx_ref, gate_ref, *, n_prompt_blocks):
    is_prompt = pl.program_id(0) < n_prompt_blocks
    att = jnp.where(is_prompt, attp_ref[...], atts_ref[...])
    y = _nn(att, w_ref[...])
    tm, d = y.shape
    agate = agate_ref[...]
    nb = agate.shape[0]
    h = jnp.where(is_prompt, hp_ref[...], hs_ref[...]) + (y.reshape(tm // nb, nb, d) * agate[None]).reshape(tm, d)
    h_ref[...] = h
    u = _norm_mod(h, g_ref[...], sh_ref[...], sc_ref[...])
    uhi = u.astype(BF16)
    ulo = (u - uhi.astype(F32)).astype(BF16)
    u_ref[...] = u
    whi = whi_ref[...]
    logits = _nt(whi, uhi) + _nt(whi, ulo) + _nt(wlo_ref[...], uhi) + b_ref[...]
    ne = logits.shape[0]
    eid = lax.broadcasted_iota(jnp.int32, logits.shape, 0)
    work = logits
    vals, idxs = [], []
    for _ in range(TOP_K):
        m = jnp.max(work, axis=0, keepdims=True)
        idx = jnp.min(jnp.where(work == m, eid, ne), axis=0, keepdims=True)
        vals.append(m)
        idxs.append(idx)
        work = jnp.where(eid == idx, -jnp.inf, work)
    es = [jnp.exp(v - vals[0]) for v in vals]
    den = es[0] + es[1] + es[2] + es[3]
    idx_ref[...] = jnp.concatenate(idxs, axis=0)
    gate_ref[...] = jnp.concatenate([e / den for e in es], axis=0)


def _rank_kernel(idx_ref, rank_ref, cnt_ref, carry, *, ne):
    @pl.when(pl.program_id(0) == 0)
    def _():
        carry[...] = jnp.zeros(carry.shape, F32)

    idx = idx_ref[...]
    tb = idx.shape[1]
    eid = lax.broadcasted_iota(jnp.int32, (ne, tb), 0)
    hits = [eid == idx[k:k + 1, :] for k in range(TOP_K)]
    oh = jnp.where(hits[0], 1.0, 0.0)
    for k in range(1, TOP_K):
        oh = oh + jnp.where(hits[k], 1.0, 0.0)
    upper = jnp.where(lax.broadcasted_iota(jnp.int32, (tb, tb), 0) < lax.broadcasted_iota(jnp.int32, (tb, tb), 1),
                      1.0, 0.0).astype(BF16)
    before = _nn(oh.astype(BF16), upper) + carry[...]
    rank_ref[...] = jnp.concatenate(
        [jnp.sum(jnp.where(h, before, 0.0), axis=0, keepdims=True) for h in hits], axis=0).astype(jnp.int32)
    carry[...] = carry[...] + jnp.sum(oh, axis=1, keepdims=True)
    cnt_ref[...] = carry[...]


def _expert_kernel(be_ref, nrows_ref, nused_ref, blk_ref, x_ref, wup_ref, bup_ref, wdn_ref, bdn_ref, y_ref, wup_bf, wdn_bf,
                   *, ff):
    i = pl.program_id(0)
    n_here = nrows_ref[i]
    prev = be_ref[jnp.maximum(i - 1, 0)]
    new_expert = (i == 0) | (be_ref[i] != prev)

    @pl.when(new_expert & (n_here > 0))
    def _():
        wup_bf[...] = wup_ref[0].astype(BF16)
        wdn_bf[...] = wdn_ref[0].astype(BF16)

    for r0 in range(0, MOE_TM, MOE_SUB):
        @pl.when(n_here > r0)
        def _(r0=r0):
            x = x_ref[r0:r0 + MOE_SUB, :].astype(BF16)
            fc = 512
            acc = jnp.zeros((MOE_SUB, wdn_bf.shape[1]), F32)
            for c0 in range(0, ff, fc):
                glu = _nn(x, wup_bf[:, c0:c0 + fc]) + bup_ref[0, :, c0:c0 + fc]
                lin = _nn(x, wup_bf[:, ff + c0:ff + c0 + fc]) + bup_ref[0, :, ff + c0:ff + c0 + fc]
                glu = jnp.minimum(glu, SWIGLU_LIMIT)
                lin = jnp.clip(lin, -SWIGLU_LIMIT, SWIGLU_LIMIT)
                act = glu * jax.nn.sigmoid(SWIGLU_ALPHA * glu) * (lin + 1.0)
                acc = acc + _nn(act.astype(BF16), wdn_bf[c0:c0 + fc, :])
            y_ref[r0:r0 + MOE_SUB, :] = _pack_bf16_pairs(acc + bdn_ref[0])

        @pl.when(n_here <= r0)
        def _(r0=r0):
            y_ref[r0:r0 + MOE_SUB, :] = jnp.zeros((MOE_SUB, y_ref.shape[1]), jnp.uint32)


def _combine_kernel(h_ref, y_ref, gk_ref, gate_ref, *o_refs, n_prompt_blocks):
    gk = gk_ref[...]
    ya, yb = _unpack_bf16_pairs(y_ref[0])
    ya, yb = ya * gk[:, 0:1], yb * gk[:, 0:1]
    for k in range(1, TOP_K):
        a, b = _unpack_bf16_pairs(y_ref[k])
        ya, yb = ya + a * gk[:, k:k + 1], yb + b * gk[:, k:k + 1]
    y = jnp.concatenate([ya, yb], axis=1)
    tm, d = y.shape
    gate = gate_ref[...]
    nb = gate.shape[0]
    res = h_ref[...] + (y.reshape(tm // nb, nb, d) * gate[None]).reshape(tm, d)
    if n_prompt_blocks is None:
        o_refs[0][...] = res
    else:
        @pl.when(pl.program_id(0) < n_prompt_blocks)
        def _():
            o_refs[0][...] = res

        @pl.when(pl.program_id(0) >= n_prompt_blocks)
        def _():
            o_refs[1][...] = res


def _shared_kv_kernel(h_ref, g_ref, sh_ref, sc_ref, w_ref, gk_ref, k_ref, v_ref, khm_ref, vhm_ref, *, kvh, hd):
    u = _norm_mod(h_ref[...], g_ref[...], sh_ref[...], sc_ref[...]).astype(BF16)
    a = _nn(u, w_ref[...])
    kw = kvh * hd
    gk = gk_ref[...]
    ks = []
    for j in range(kw // LANES):
        x = a[:, j * LANES:(j + 1) * LANES]
        ks.append(x * _group64_rscale(x) * gk[:, j * LANES:(j + 1) * LANES])
    k = jnp.concatenate(ks, axis=-1)
    v = a[:, kw:2 * kw]
    k_ref[...] = k
    v_ref[...] = v
    for hh in range(kvh):
        khm_ref[hh] = k[:, hh * hd:(hh + 1) * hd].astype(BF16)
        vhm_ref[hh] = v[:, hh * hd:(hh + 1) * hd].astype(BF16)


def _swa_q_kernel(h_ref, g_ref, sh_ref, sc_ref, w_ref, gq_ref, q_ref, *, heads, hd, scale):
    u = _norm_mod(h_ref[...], g_ref[...], sh_ref[...], sc_ref[...]).astype(BF16)
    a = _nn(u, w_ref[...])
    gq = gq_ref[...]
    for j in range(heads * hd // LANES):
        x = a[:, j * LANES:(j + 1) * LANES]
        y = x * _group64_rscale(x) * gq[:, j * LANES:(j + 1) * LANES] * scale
        q_ref[2 * j] = y[:, :hd].astype(BF16)
        q_ref[2 * j + 1] = y[:, hd:].astype(BF16)


def _softmax_pv(s, v):
    m = jnp.max(s, axis=-1, keepdims=True)
    e = jnp.exp(s - m)
    return _nn(e.astype(BF16), v) / jnp.sum(e, axis=-1, keepdims=True)


def _swa_prompt_kernel(bias_ref, q_ref, kp_ref, kc_ref, vp_ref, vc_ref, o_ref, *, heads, kvh, tq):
    group = heads // kvh
    rows = group * tq
    not_first = lax.broadcasted_iota(jnp.int32, (WINDOW + tq, q_ref.shape[2]), 0) > 0
    outs = []
    for g in range(kvh):
        k = jnp.where(not_first, jnp.concatenate([kp_ref[g], kc_ref[g]], axis=0), 0)
        v = jnp.where(not_first, jnp.concatenate([vp_ref[g], vc_ref[g]], axis=0), 0)
        q = q_ref[g * group:(g + 1) * group].reshape(rows, q_ref.shape[2])
        o = _softmax_pv(_nt(q, k) + bias_ref[g], v)
        for j in range(group):
            outs.append(o[j * tq:(j + 1) * tq])
    o_ref[...] = jnp.concatenate(outs, axis=-1).astype(BF16)


def _swa_sample_kernel(bias_ref, q_ref, k_ref, v_ref, o_ref):
    bias = bias_ref[...]
    for b in range(q_ref.shape[0]):
        o_ref[b] = _softmax_pv(_nt(q_ref[b], k_ref[b]) + bias, v_ref[b]).astype(BF16)


def _rope_tables(pos, rope):
    half = rope // 2
    inv = ROPE_THETA ** (-jnp.arange(half, dtype=F32) / half)
    ang = pos.astype(F32)[:, None] * inv[None, :]
    cos, sin = jnp.cos(ang), jnp.sin(ang)
    return jnp.concatenate([cos, cos], axis=-1), jnp.concatenate([sin, sin], axis=-1)


def _rot_cols(w):
    half = w.shape[-1] // 2
    return jnp.concatenate([-w[..., half:], w[..., :half]], axis=-1)


def _swap_halves(g):
    half = g.shape[-1] // 2
    return jnp.concatenate([g[..., half:], g[..., :half]], axis=-1)


def _attn_out_moe(att_p, att_s, w_o, h_p, h_s, h_s_idx, m_rows, tab, layer, g_ffn, w_router, b_router,
                  w_up, b_up, w_down, b_down, *, seq, nb_prompt, n_prompt_blocks, split_rows=None):
    d = h_p.shape[1]
    ne = w_router.shape[1]
    ff = w_down.shape[2]
    nb = tab.shape[2]
    n_tok_blocks = m_rows // TM
    col0 = 3

    def tab_spec(col):
        return pl.BlockSpec((None, None, nb, d),
                            lambda i: (layer, jnp.minimum(i * TM // seq, nb_prompt), 0, col))

    def prompt_blk(i):
        return (jnp.minimum(i, n_prompt_blocks - 1), 0)

    def sample_blk(i):
        return (jnp.maximum(i - n_prompt_blocks, 0), 0)

    wr_t = w_router.T
    wr_hi = wr_t.astype(BF16)
    wr_lo = (wr_t - wr_hi.astype(F32)).astype(BF16)
    h, u, idx_t, gate_t = pl.pallas_call(
        functools.partial(_oproj_router_kernel, n_prompt_blocks=n_prompt_blocks),
        grid=(n_tok_blocks,),
        in_specs=[pl.BlockSpec((TM, w_o.shape[0]), prompt_blk), pl.BlockSpec((TM, w_o.shape[0]), sample_blk),
                  pl.BlockSpec(w_o.shape, lambda i: (0, 0)),
                  pl.BlockSpec((TM, d), prompt_blk), pl.BlockSpec((TM, d), h_s_idx), tab_spec(2),
                  pl.BlockSpec((1, d), lambda i: (0, 0)),
                  tab_spec(col0), tab_spec(col0 + 1),
                  pl.BlockSpec((ne, d), lambda i: (0, 0)),
                  pl.BlockSpec((ne, d), lambda i: (0, 0)),
                  pl.BlockSpec((ne, 1), lambda i: (0, 0))],
        out_specs=[pl.BlockSpec((TM, d), lambda i: (i, 0)),
                   pl.BlockSpec((TM, d), lambda i: (i, 0)),
                   pl.BlockSpec((TOP_K, TM), lambda i: (0, i)),
                   pl.BlockSpec((TOP_K, TM), lambda i: (0, i))],
        out_shape=[jax.ShapeDtypeStruct((m_rows, d), F32),
                   jax.ShapeDtypeStruct((m_rows, d), F32),
                   jax.ShapeDtypeStruct((TOP_K, m_rows), jnp.int32),
                   jax.ShapeDtypeStruct((TOP_K, m_rows), F32)],
        compiler_params=_params("parallel"),
        name="oproj_router",
    )(att_p, att_s, w_o.astype(BF16), h_p, h_s, tab, g_ffn.reshape(1, d), tab, tab, wr_hi, wr_lo,
      b_router.reshape(ne, 1))

    a = m_rows * TOP_K
    rank_t, cnt = pl.pallas_call(
        functools.partial(_rank_kernel, ne=ne),
        grid=(n_tok_blocks,),
        in_specs=[pl.BlockSpec((TOP_K, TM), lambda i: (0, i))],
        out_specs=[pl.BlockSpec((TOP_K, TM), lambda i: (0, i)), pl.BlockSpec((ne, 1), lambda i: (0, 0))],
        out_shape=[jax.ShapeDtypeStruct((TOP_K, m_rows), jnp.int32), jax.ShapeDtypeStruct((ne, 1), F32)],
        scratch_shapes=[pltpu.VMEM((ne, 1), F32)],
        compiler_params=_params("arbitrary"),
        name="moe_rank",
    )(idx_t)
    counts = cnt[:, 0].astype(jnp.int32)
    padded = (counts + MOE_TM - 1) // MOE_TM * MOE_TM
    pad_end = jnp.cumsum(padded)
    start = pad_end - padded
    experts = jnp.arange(ne, dtype=jnp.int32)
    pos_t = rank_t + jnp.sum(jnp.where(idx_t[..., None] == experts, start, 0), axis=-1)
    order = jnp.argsort(idx_t.T.reshape(a)).astype(jnp.int32)
    shift = start - (jnp.cumsum(counts) - counts)
    n_blk = -(-a // MOE_TM) + ne
    n_slots = n_blk * MOE_TM
    blk_start = jnp.arange(n_blk, dtype=jnp.int32) * MOE_TM
    blk_expert = jnp.minimum(jnp.sum(pad_end[None, :] <= blk_start[:, None], axis=1), ne - 1).astype(jnp.int32)
    n_used = (pad_end[-1] // MOE_TM).astype(jnp.int32).reshape(1)
    blk_rows = jnp.clip((pad_end - padded + counts)[blk_expert] - blk_start, 0, MOE_TM)
    blk_rows = jnp.where(blk_start < pad_end[-1], blk_rows, 0).astype(jnp.int32)
    slot_sorted = jnp.arange(n_slots, dtype=jnp.int32) - jnp.repeat(shift[blk_expert], MOE_TM)
    slot_tok = order[jnp.clip(slot_sorted, 0, a - 1)] // TOP_K
    xs = jnp.take(u, slot_tok, axis=0, mode="clip")
    steps = jnp.arange(n_blk, dtype=jnp.int32)
    blk_of_step = jnp.where(steps < n_used[0], (start // MOE_TM + pad_end // MOE_TM - 1)[blk_expert] - steps, steps)
    blk_of_step = blk_of_step.astype(jnp.int32)
    step_rows = blk_rows[blk_of_step]

    def live(i, nused):
        return jnp.minimum(i, nused[0] - 1)

    yb = pl.pallas_call(
        functools.partial(_expert_kernel, ff=ff),
        grid_spec=pltpu.PrefetchScalarGridSpec(
            num_scalar_prefetch=4,
            grid=(n_blk,),
            in_specs=[pl.BlockSpec((MOE_TM, d), lambda i, be, nr, nu, bs: (bs[live(i, nu)], 0)),
                      pl.BlockSpec((None, 1, d, 2 * ff), lambda i, be, nr, nu, bs: (layer, be[live(i, nu)], 0, 0)),
                      pl.BlockSpec((None, 1, 1, 2 * ff), lambda i, be, nr, nu, bs: (layer, be[live(i, nu)], 0, 0)),
                      pl.BlockSpec((None, 1, ff, d), lambda i, be, nr, nu, bs: (layer, be[live(i, nu)], 0, 0)),
                      pl.BlockSpec((None, 1, 1, d), lambda i, be, nr, nu, bs: (layer, be[live(i, nu)], 0, 0))],
            out_specs=pl.BlockSpec((MOE_TM, d // 2), lambda i, be, nr, nu, bs: (bs[i], 0)),
            scratch_shapes=[pltpu.VMEM((d, 2 * ff), BF16), pltpu.VMEM((ff, d), BF16)]),
        out_shape=jax.ShapeDtypeStruct((n_slots, d // 2), jnp.uint32),
        compiler_params=_params("arbitrary"),
        name="experts",
    )(blk_expert, step_rows, n_used, blk_of_step, xs, w_up, b_up.reshape(b_up.shape[0], ne, 1, 2 * ff), w_down,
      b_down.reshape(b_down.shape[0], ne, 1, d))

    ysel = jnp.take(yb, pos_t.reshape(a), axis=0, mode="clip").reshape(TOP_K, m_rows, d // 2)
    if split_rows is None:
        out_specs = pl.BlockSpec((TM, d), lambda i: (i, 0))
        out_shape = jax.ShapeDtypeStruct((m_rows, d), F32)
        npb = None
    else:
        npb = split_rows // TM
        out_specs = [pl.BlockSpec((TM, d), lambda i: (jnp.minimum(i, npb - 1), 0)),
                     pl.BlockSpec((TM, d), lambda i: (jnp.maximum(i - npb, 0), 0))]
        out_shape = [jax.ShapeDtypeStruct((split_rows, d), F32), jax.ShapeDtypeStruct((m_rows - split_rows, d), F32)]
    return pl.pallas_call(
        functools.partial(_combine_kernel, n_prompt_blocks=npb),
        grid=(n_tok_blocks,),
        in_specs=[pl.BlockSpec((TM, d), lambda i: (i, 0)),
                  pl.BlockSpec((TOP_K, TM, d // 2), lambda i: (0, i, 0)),
                  pl.BlockSpec((TM, TOP_K), lambda i: (i, 0)),
                  tab_spec(col0 + 2)],
        out_specs=out_specs,
        out_shape=out_shape,
        compiler_params=_params("arbitrary"),
        name="moe_combine",
    )(h, ysel, gate_t.T, tab)


def kernel(x_prompt, x_sample, c_prompt, c_sample, cache_mla_latent, cache_mla_krope, state_win_k, state_win_v, page_table, w_mod, b_mod, g_attn, g_ffn, w_mla_down, g_mla_q_lora, g_mla_kv_lora, w_mla_uq, w_mla_uk, w_mla_uv, g_mla_qn, g_mla_qr, g_mla_kn, g_mla_kr, w_mla_o, w_kvmod, b_kvmod, g_kv_norm, w_kv, g_swa_k, w_swa_q, g_swa_q, swa_sinks, w_swa_o, w_router, b_router, w_up, b_up, w_down, b_down):
    nbp, seq, d = x_prompt.shape
    nbs, t_new, _ = x_sample.shape
    q_lora = g_mla_q_lora.shape[1]
    kv_lora = g_mla_kv_lora.shape[1]
    heads, nope = w_mla_uk.shape[2], w_mla_uk.shape[3]
    rope = g_mla_qr.shape[1]
    vdim = w_mla_uv.shape[3]
    n_pages = page_table.shape[1]
    past = n_pages * PAGE
    swa_heads = swa_sinks.shape[1]
    hd = g_swa_k.shape[0]
    kvh = w_kv.shape[1] // (2 * hd)
    assert nope + 2 * rope == LANES and 2 * vdim == LANES and 2 * hd == LANES and TM % FLASH_ROWS == 0
    assert seq % TM == 0 and TM % nbs == 0 and (nbs * t_new) % TM == 0 and nbs % 8 == 0
    assert w_mod.shape[0] == 2 and n_pages % (2 * SAMPLE_PAGES) == 0

    mp = nbp * seq
    ms = nbs * t_new
    m_rows = mp + ms
    n_tok_blocks = m_rows // TM
    n_prompt_blocks = mp // TM
    blocks_per_seq = seq // TM

    def tab_idx(i):
        return jnp.minimum(i * TM // seq, nbp)

    def tok(i):
        return (i, 0)

    def const2(i):
        return (0, 0)

    h0_p = x_prompt.reshape(mp, d)
    h0_s = x_sample.transpose(1, 0, 2).reshape(ms, d)

    def prompt_blk(i):
        return (jnp.minimum(i, n_prompt_blocks - 1), 0)

    def sample_blk(i):
        return (jnp.maximum(i - n_prompt_blocks, 0), 0)

    c_all = jnp.concatenate([c_prompt, c_sample], axis=0)
    tab = _adaln_table(c_all, w_mod, b_mod, nbp)
    tab_kv = _adaln_table(c_all, w_kvmod[None], b_kvmod[None], nbp)

    def tab_spec(layer, col):
        return pl.BlockSpec((None, None, nbs, d), lambda i: (layer, tab_idx(i), 0, col))

    cos_p, sin_p = _rope_tables(jnp.arange(seq), rope)
    cos_s, sin_s = _rope_tables(past + jnp.arange(t_new), rope)
    cos_tab = jnp.concatenate([cos_p, jnp.repeat(cos_s, nbs, axis=0)], axis=0)
    sin_tab = jnp.concatenate([sin_p, jnp.repeat(sin_s, nbs, axis=0)], axis=0)

    def pos_blk(i):
        return (jnp.where(i < n_prompt_blocks, i % blocks_per_seq, blocks_per_seq + i - n_prompt_blocks), 0)

    wd = w_mla_down[0]
    w_down_ext = jnp.concatenate([wd, _rot_cols(wd[:, q_lora + kv_lora:])], axis=1).astype(BF16)
    nd = w_down_ext.shape[1]
    cq, ckv_p, ckv_s, kr_p, kr_s = pl.pallas_call(
        functools.partial(_mla_down_kernel, q_lora=q_lora, kv_lora=kv_lora, rope=rope,
                          n_prompt_blocks=n_prompt_blocks),
        grid=(n_tok_blocks,),
        in_specs=[pl.BlockSpec((TM, d), prompt_blk), pl.BlockSpec((TM, d), sample_blk),
                  pl.BlockSpec((1, d), const2), tab_spec(0, 0), tab_spec(0, 1),
                  pl.BlockSpec((d, nd), const2), pl.BlockSpec((1, q_lora), const2),
                  pl.BlockSpec((1, kv_lora), const2), pl.BlockSpec((1, rope), const2),
                  pl.BlockSpec((1, rope), const2), pl.BlockSpec((TM, rope), pos_blk),
                  pl.BlockSpec((TM, rope), pos_blk)],
        out_specs=[pl.BlockSpec((TM, q_lora), tok),
                   pl.BlockSpec((TM, kv_lora), prompt_blk), pl.BlockSpec((TM, kv_lora), sample_blk),
                   pl.BlockSpec((TM, rope), prompt_blk), pl.BlockSpec((TM, rope), sample_blk)],
        out_shape=[jax.ShapeDtypeStruct((m_rows, q_lora), BF16),
                   jax.ShapeDtypeStruct((mp, kv_lora), F32), jax.ShapeDtypeStruct((ms, kv_lora), F32),
                   jax.ShapeDtypeStruct((mp, rope), F32), jax.ShapeDtypeStruct((ms, rope), F32)],
        compiler_params=_params("arbitrary"),
        name="mla_down",
    )(h0_p, h0_s, g_attn[0].reshape(1, d), tab, tab, w_down_ext, g_mla_q_lora, g_mla_kv_lora,
      g_mla_kr, _swap_halves(g_mla_kr), cos_tab, sin_tab)

    wq = w_mla_uq[0]
    w_q_cat = jnp.concatenate([wq, _rot_cols(wq[..., nope:])], axis=-1).reshape(q_lora, heads * LANES).astype(BF16)
    gq_cat = jnp.concatenate([g_mla_qn[0], g_mla_qr[0], _swap_halves(g_mla_qr[0])]).reshape(1, LANES)
    q_tab = jnp.concatenate([jnp.ones((cos_tab.shape[0], nope), F32), cos_tab, sin_tab], axis=1)
    mla_scale = (nope + rope) ** -0.5 * LOG2_E
    q_cat = pl.pallas_call(
        functools.partial(_q_up_kernel, heads=heads, nope=nope, rope=rope, scale=mla_scale),
        grid=(n_tok_blocks,),
        in_specs=[pl.BlockSpec((TM, q_lora), tok), pl.BlockSpec((q_lora, heads * LANES), const2),
                  pl.BlockSpec((1, LANES), const2), pl.BlockSpec((TM, LANES), pos_blk)],
        out_specs=pl.BlockSpec((heads, TM, LANES), lambda i: (0, i, 0)),
        out_shape=jax.ShapeDtypeStruct((heads, m_rows, LANES), BF16),
        compiler_params=_params("parallel"),
        name="q_up",
    )(cq, w_q_cat, gq_cat, q_tab)

    wuk = w_mla_uk[0]
    wuv = w_mla_uv[0]
    wuk_t = wuk.reshape(kv_lora, heads * nope).T.astype(BF16)
    tk = TM
    nk = seq // tk
    kt, v_p = pl.pallas_call(
        functools.partial(_kv_up_kernel, heads=heads, nope=nope, rope=rope),
        grid=(n_prompt_blocks,),
        in_specs=[pl.BlockSpec((TM, kv_lora), tok), pl.BlockSpec((TM, rope), tok),
                  pl.BlockSpec((heads * nope, kv_lora), const2), pl.BlockSpec((nope, 1), const2),
                  pl.BlockSpec((kv_lora, heads * vdim), const2), pl.BlockSpec((rope, rope), const2)],
        out_specs=[pl.BlockSpec((1, heads, 1, LANES, tk), lambda i: (i // nk, 0, i % nk, 0, 0)),
                   pl.BlockSpec((TM, heads * vdim), tok)],
        out_shape=[jax.ShapeDtypeStruct((nbp, heads, nk, LANES, tk), BF16),
                   jax.ShapeDtypeStruct((mp, heads * vdim), BF16)],
        compiler_params=_params("parallel"),
        name="kv_up",
    )(ckv_p, kr_p, wuk_t, g_mla_kn[0].reshape(nope, 1), wuv.reshape(kv_lora, heads * vdim).astype(BF16),
      jnp.eye(rope, dtype=BF16))

    tq = tk
    nq = seq // tq
    att_p = pl.pallas_call(
        functools.partial(_mla_flash_kernel, tq=tq, tk=tk, vdim=vdim),
        grid=(nbp, heads // 2, nq),
        in_specs=[pl.BlockSpec((2, tq, LANES), lambda b, hp, qi: (hp, b * nq + qi, 0)),
                  pl.BlockSpec((1, 2, nk, LANES, tk), lambda b, hp, qi: (b, hp, 0, 0, 0)),
                  pl.BlockSpec((seq, LANES), lambda b, hp, qi: (b, hp))],
        out_specs=pl.BlockSpec((tq, LANES), lambda b, hp, qi: (b * nq + qi, hp)),
        out_shape=jax.ShapeDtypeStruct((mp, heads * vdim), BF16),
        compiler_params=_params("parallel", "parallel", "arbitrary"),
        name="mla_prompt_attention",
    )(q_cat, kt, v_p)

    first_sample_blk = mp // ms
    qa, qr = pl.pallas_call(
        functools.partial(_q_absorb_kernel, heads=heads, nope=nope, rope=rope),
        grid=(1,),
        in_specs=[pl.BlockSpec((heads, ms, LANES), lambda i: (0, first_sample_blk, 0)),
                  pl.BlockSpec((heads, nope, kv_lora), lambda i: (0, 0, 0)),
                  pl.BlockSpec((1, nope), const2)],
        out_specs=[pl.BlockSpec((heads, ms, kv_lora), lambda i: (0, 0, 0)),
                   pl.BlockSpec((heads, ms, rope), lambda i: (0, 0, 0))],
        out_shape=[jax.ShapeDtypeStruct((heads, ms, kv_lora), BF16),
                   jax.ShapeDtypeStruct((heads, ms, rope), BF16)],
        compiler_params=_params("arbitrary"),
        name="q_absorb",
    )(q_cat, wuk.transpose(1, 2, 0).astype(BF16), g_mla_kn)
    rows = t_new * heads

    def per_batch(x):
        return x.reshape(heads, t_new, nbs, -1).transpose(2, 1, 0, 3).reshape(nbs, rows, -1)

    def new_rows(x):
        x = x.reshape(t_new, nbs, -1).transpose(1, 0, 2)
        return jnp.pad(x, ((0, 0), (0, 8 - t_new), (0, 0)))

    n_chunks = n_pages // SAMPLE_PAGES
    sample_tokens = SAMPLE_PAGES * PAGE
    o_lat = pl.pallas_call(
        functools.partial(_mla_sample_kernel, heads=heads, nope=nope, n_chunks=n_chunks, n_batch=nbs,
                          t_new=t_new),
        grid_spec=pltpu.PrefetchScalarGridSpec(
            num_scalar_prefetch=1,
            grid=(nbs,),
            in_specs=[pl.BlockSpec((1, rows, kv_lora), lambda n, pt: (n, 0, 0)),
                      pl.BlockSpec((1, rows, rope), lambda n, pt: (n, 0, 0)),
                      pl.BlockSpec((heads * nope, kv_lora), lambda n, pt: (0, 0)),
                      pl.BlockSpec((1, 8, kv_lora), lambda n, pt: (n, 0, 0)),
                      pl.BlockSpec((1, 8, rope), lambda n, pt: (n, 0, 0)),
                      pl.BlockSpec(memory_space=pl.ANY),
                      pl.BlockSpec(memory_space=pl.ANY)],
            out_specs=pl.BlockSpec((1, rows, kv_lora), lambda n, pt: (n, 0, 0)),
            scratch_shapes=[pltpu.VMEM((2, SAMPLE_PAGES, PAGE, kv_lora), F32),
                            pltpu.VMEM((2, SAMPLE_PAGES, rope, PAGE), F32),
                            pltpu.VMEM((sample_tokens, kv_lora), BF16),
                            pltpu.VMEM((sample_tokens, kv_lora), BF16),
                            pltpu.VMEM((rows, sample_tokens), F32),
                            pltpu.VMEM((rows, sample_tokens), F32),
                            pltpu.SemaphoreType.DMA((2, 2))]),
        out_shape=jax.ShapeDtypeStruct((nbs, rows, kv_lora), F32),
        compiler_params=_params("arbitrary"),
        name="mla_sample_attention",
    )(page_table, per_batch(qa), per_batch(qr), wuk_t, new_rows(ckv_s), new_rows(kr_s),
      cache_mla_latent[0], cache_mla_krope[0].transpose(0, 2, 1))
    o_lat = o_lat.reshape(nbs, t_new, heads, kv_lora).transpose(2, 1, 0, 3).reshape(heads, ms, kv_lora)
    att_s = pl.pallas_call(
        functools.partial(_unabsorb_kernel, heads=heads),
        grid=(1,),
        in_specs=[pl.BlockSpec((heads, ms, kv_lora), lambda i: (0, 0, 0)),
                  pl.BlockSpec((heads, kv_lora, vdim), lambda i: (0, 0, 0))],
        out_specs=pl.BlockSpec((ms, heads * vdim), const2),
        out_shape=jax.ShapeDtypeStruct((ms, heads * vdim), BF16),
        compiler_params=_params("arbitrary"),
        name="unabsorb",
    )(o_lat.astype(BF16), wuv.transpose(1, 0, 2).astype(BF16))

    def sample_blk_of_all(i):
        return (n_prompt_blocks + jnp.maximum(i - n_prompt_blocks, 0), 0)

    h1 = _attn_out_moe(att_p, att_s, w_mla_o[0], h0_p, h0_s, sample_blk, m_rows, tab, 0, g_ffn[0],
                       w_router[0], b_router[0], w_up, b_up, w_down, b_down,
                       seq=seq, nb_prompt=nbp, n_prompt_blocks=n_prompt_blocks)

    kw = kvh * hd
    k_all, v_all, k_hm, v_hm = pl.pallas_call(
        functools.partial(_shared_kv_kernel, kvh=kvh, hd=hd),
        grid=(n_tok_blocks,),
        in_specs=[pl.BlockSpec((TM, d), tok), pl.BlockSpec((1, d), const2), tab_spec(0, 0), tab_spec(0, 1),
                  pl.BlockSpec((d, 2 * kw), const2), pl.BlockSpec((1, kw), const2)],
        out_specs=[pl.BlockSpec((TM, kw), tok), pl.BlockSpec((TM, kw), tok),
                   pl.BlockSpec((kvh, TM, hd), lambda i: (0, i, 0)),
                   pl.BlockSpec((kvh, TM, hd), lambda i: (0, i, 0))],
        out_shape=[jax.ShapeDtypeStruct((m_rows, kw), F32), jax.ShapeDtypeStruct((m_rows, kw), F32),
                   jax.ShapeDtypeStruct((kvh, m_rows, hd), BF16),
                   jax.ShapeDtypeStruct((kvh, m_rows, hd), BF16)],
        compiler_params=_params("parallel"),
        name="shared_kv",
    )(h1, g_kv_norm.reshape(1, d), tab_kv, tab_kv, w_kv.astype(BF16), jnp.tile(g_swa_k, kvh).reshape(1, kw))

    swa_scale = hd ** -0.5
    q_hm = pl.pallas_call(
        functools.partial(_swa_q_kernel, heads=swa_heads, hd=hd, scale=swa_scale),
        grid=(n_tok_blocks,),
        in_specs=[pl.BlockSpec((TM, d), tok), pl.BlockSpec((1, d), const2), tab_spec(1, 0), tab_spec(1, 1),
                  pl.BlockSpec((d, swa_heads * hd), const2), pl.BlockSpec((1, swa_heads * hd), const2)],
        out_specs=pl.BlockSpec((swa_heads, TM, hd), lambda i: (0, i, 0)),
        out_shape=jax.ShapeDtypeStruct((swa_heads, m_rows, hd), BF16),
        compiler_params=_params("parallel"),
        name="swa_q",
    )(h1, g_attn[1].reshape(1, d), tab, tab, w_swa_q[0].astype(BF16),
      jnp.tile(g_swa_q[0], swa_heads).reshape(1, swa_heads * hd))

    group = swa_heads // kvh
    slopes = 2.0 ** (-8.0 * jnp.arange(1, swa_heads + 1, dtype=F32) / swa_heads)
    sinks = swa_sinks[0].astype(F32)

    def bias_table(dist, valid, sink_col, head_of_row):
        b = jnp.where(valid, -slopes[head_of_row][:, None] * dist.astype(F32), NEG_INF)
        return jnp.where(sink_col, sinks[head_of_row][:, None], b)

    tq = WINDOW
    wblocks = seq // tq
    r = jnp.arange(group * tq)
    col = jnp.arange(WINDOW + tq)
    dist = (r % tq)[:, None] + WINDOW - col[None, :]
    inside = (dist >= 0) & (dist < WINDOW)
    bias_p = jnp.stack([
        jnp.stack([bias_table(dist, inside & ok[None, :], (col == 0)[None, :], g * group + r // tq)
                   for g in range(kvh)])
        for ok in (col >= 0, col >= WINDOW)])
    att_p = pl.pallas_call(
        functools.partial(_swa_prompt_kernel, heads=swa_heads, kvh=kvh, tq=tq),
        grid=(nbp, wblocks),
        in_specs=[pl.BlockSpec((None, kvh, group * tq, WINDOW + tq), lambda b, i: (jnp.where(i == 0, 1, 0), 0, 0, 0)),
                  pl.BlockSpec((swa_heads, tq, hd), lambda b, i: (0, b * wblocks + i, 0)),
                  pl.BlockSpec((kvh, WINDOW, hd), lambda b, i: (0, jnp.maximum(b * wblocks + i - 1, 0), 0)),
                  pl.BlockSpec((kvh, tq, hd), lambda b, i: (0, b * wblocks + i, 0)),
                  pl.BlockSpec((kvh, WINDOW, hd), lambda b, i: (0, jnp.maximum(b * wblocks + i - 1, 0), 0)),
                  pl.BlockSpec((kvh, tq, hd), lambda b, i: (0, b * wblocks + i, 0))],
        out_specs=pl.BlockSpec((tq, swa_heads * hd), lambda b, i: (b * wblocks + i, 0)),
        out_shape=jax.ShapeDtypeStruct((mp, swa_heads * hd), BF16),
        compiler_params=_params("parallel", "arbitrary"),
        name="swa_prompt_attention",
    )(bias_p, q_hm, k_hm, k_hm, v_hm, v_hm)

    tpad = 8
    nkg = WINDOW + tpad

    def stack_keys(win, new_hm):
        new = new_hm[:, mp:].reshape(kvh, t_new, nbs, hd).transpose(2, 0, 1, 3)
        new = jnp.pad(new, ((0, 0), (0, 0), (0, tpad - t_new), (0, 0)))
        return jnp.concatenate([win.transpose(0, 2, 1, 3).astype(BF16), new], axis=2).reshape(nbs, kvh * nkg, hd)

    rs = jnp.arange(t_new * swa_heads)
    t_of, h_of = rs // swa_heads, rs % swa_heads
    cs = jnp.arange(kvh * nkg)
    j = cs % nkg
    dist_s = jnp.where(j < WINDOW, WINDOW + t_of[:, None] - j[None, :], t_of[:, None] - (j[None, :] - WINDOW))
    own = (cs // nkg)[None, :] == (h_of // group)[:, None]
    ok_s = own & (dist_s >= 0) & (dist_s < WINDOW) & ((j < WINDOW + t_new)[None, :])
    bias_s = bias_table(dist_s, ok_s, own & ((j == WINDOW + t_new)[None, :]), h_of)
    q_s = q_hm[:, mp:].reshape(swa_heads, t_new, nbs, hd).transpose(2, 1, 0, 3).reshape(nbs, t_new * swa_heads, hd)
    nbi = 8
    att_s = pl.pallas_call(
        _swa_sample_kernel,
        grid=(nbs // nbi,),
        in_specs=[pl.BlockSpec(bias_s.shape, const2),
                  pl.BlockSpec((nbi, t_new * swa_heads, hd), lambda i: (i, 0, 0)),
                  pl.BlockSpec((nbi, kvh * nkg, hd), lambda i: (i, 0, 0)),
                  pl.BlockSpec((nbi, kvh * nkg, hd), lambda i: (i, 0, 0))],
        out_specs=pl.BlockSpec((nbi, t_new * swa_heads, hd), lambda i: (i, 0, 0)),
        out_shape=jax.ShapeDtypeStruct((nbs, t_new * swa_heads, hd), BF16),
        compiler_params=_params("parallel"),
        name="swa_sample_attention",
    )(bias_s, q_s, stack_keys(state_win_k, k_hm), stack_keys(state_win_v, v_hm))
    att_s = att_s.reshape(nbs, t_new, swa_heads * hd).transpose(1, 0, 2).reshape(ms, swa_heads * hd)

    y_p, y_s = _attn_out_moe(att_p, att_s, w_swa_o[0], h1, h1, sample_blk_of_all, m_rows, tab, 1, g_ffn[1],
                             w_router[1], b_router[1], w_up, b_up, w_down, b_down,
                             seq=seq, nb_prompt=nbp, n_prompt_blocks=n_prompt_blocks, split_rows=mp)

    def sample_major(x):
        return x.reshape(t_new, nbs, -1).transpose(1, 0, 2)

    def last_window(x):
        return x[:mp].reshape(nbp, seq, kvh, hd)[:, -WINDOW:]

    y_prompt = y_p.reshape(nbp, seq, d)
    y_sample = sample_major(y_s)
    lat_p = ckv_p.reshape(1, nbp, seq, kv_lora)
    krope_p = kr_p.reshape(1, nbp, seq, rope)
    lat_s = sample_major(ckv_s)[None]
    krope_s = sample_major(kr_s)[None]
    k_n = sample_major(k_all[mp:]).reshape(nbs, t_new, kvh, hd)
    v_n = sample_major(v_all[mp:]).reshape(nbs, t_new, kvh, hd)
    win_k_s = jnp.concatenate([state_win_k, k_n], axis=1)[:, -WINDOW:]
    win_v_s = jnp.concatenate([state_win_v, v_n], axis=1)[:, -WINDOW:]
    return (y_prompt, y_sample, lat_p, krope_p, lat_s, krope_s,
            last_window(k_all), last_window(v_all), win_k_s, win_v_s)
```

```python
import functools

import jax
import jax.numpy as jnp
from jax import lax
from jax.experimental import pallas as pl
from jax.experimental.pallas import tpu as pltpu

F32 = jnp.float32
BF16 = jnp.bfloat16

RMS_EPS = 1e-6
NEG_INF = -1e30
ROPE_THETA = 10000.0
LOG2_E = 1.4426950408889634
PAGE = 128
WINDOW = 128
TOP_K = 4
SWIGLU_ALPHA = 1.702
SWIGLU_LIMIT = 7.0
LANES = 128
VMEM_LIMIT = 56 * 1024 * 1024

TM = 512
MOE_TM = 512
MOE_SUB = 256
FLASH_ROWS = 256
SAMPLE_PAGES = 16
SAMPLE_SUB = 256


def _nn(a, b):
    return jnp.dot(a, b, preferred_element_type=F32)


def _nt(a, b):
    return lax.dot_general(a, b, (((1,), (1,)), ((), ())), preferred_element_type=F32)


def _params(*sem):
    return pltpu.CompilerParams(dimension_semantics=sem, vmem_limit_bytes=VMEM_LIMIT)


def _norm_mod(h, g, shift, scale):
    tm, d = h.shape
    nb = shift.shape[0]
    y = h * lax.rsqrt(jnp.mean(h * h, axis=-1, keepdims=True) + RMS_EPS) * g
    y = y.reshape(tm // nb, nb, d) * (1.0 + scale[None]) + shift[None]
    return y.reshape(tm, d)


def _pack_bf16_pairs(x):
    k = x.shape[1] // 2
    r = x.astype(BF16).astype(F32)
    hi = lax.bitcast_convert_type(r[:, :k], jnp.uint32) & jnp.uint32(0xFFFF0000)
    lo = lax.bitcast_convert_type(r[:, k:], jnp.uint32) >> 16
    return hi | lo


def _unpack_bf16_pairs(w):
    return (lax.bitcast_convert_type(w & jnp.uint32(0xFFFF0000), F32),
            lax.bitcast_convert_type(w << 16, F32))


def _group64_rscale(x):
    lane = lax.broadcasted_iota(jnp.int32, x.shape, 1)
    x2 = x * x
    lo = jnp.sum(jnp.where(lane < 64, x2, 0.0), axis=-1, keepdims=True)
    hi = jnp.sum(jnp.where(lane >= 64, x2, 0.0), axis=-1, keepdims=True)
    return jnp.where(lane < 64, lax.rsqrt(lo / 64.0 + RMS_EPS), lax.rsqrt(hi / 64.0 + RMS_EPS))


def _adaln_kernel(c_ref, w_ref, b_ref, o_ref, *, nb_prompt):
    c = c_ref[...]
    a = (c * jax.nn.sigmoid(c)).astype(BF16)
    res = _nn(a, w_ref[...].astype(BF16)) + b_ref[...]
    nb = res.shape[0] - nb_prompt
    for b in range(nb_prompt):
        o_ref[b] = jnp.broadcast_to(res[b:b + 1], (nb, res.shape[1]))
    o_ref[nb_prompt] = res[nb_prompt:]


def _adaln_table(c, w, b, nb_prompt):
    n, d = c.shape
    nl, _, nout = w.shape
    nb = n - nb_prompt
    tn = 1024
    return pl.pallas_call(
        functools.partial(_adaln_kernel, nb_prompt=nb_prompt),
        grid=(nl, nout // tn),
        in_specs=[pl.BlockSpec((n, d), lambda l, j: (0, 0)),
                  pl.BlockSpec((None, d, tn), lambda l, j: (l, 0, j)),
                  pl.BlockSpec((None, 1, tn), lambda l, j: (l, 0, j))],
        out_specs=pl.BlockSpec((None, nb_prompt + 1, nb, tn), lambda l, j: (l, 0, 0, j)),
        out_shape=jax.ShapeDtypeStruct((nl, nb_prompt + 1, nb, nout), F32),
        compiler_params=_params("parallel", "parallel"),
        name="adaln",
    )(c, w, b.reshape(nl, 1, nout))


def _mla_down_kernel(hp_ref, hs_ref, g_ref, sh_ref, sc_ref, w_ref, gql_ref, gkvl_ref, gkr_ref, gkrp_ref,
                     cos_ref, sin_ref, cq_ref, ckvp_ref, ckvs_ref, krp_ref, krs_ref,
                     *, q_lora, kv_lora, rope, n_prompt_blocks):
    is_prompt = pl.program_id(0) < n_prompt_blocks
    h = jnp.where(is_prompt, hp_ref[...], hs_ref[...])
    u = _norm_mod(h, g_ref[...], sh_ref[...], sc_ref[...]).astype(BF16)
    a = _nn(u, w_ref[...])
    q = a[:, :q_lora]
    cq_ref[...] = (q * lax.rsqrt(jnp.mean(q * q, axis=-1, keepdims=True) + RMS_EPS)
                   * gql_ref[...]).astype(BF16)
    c = a[:, q_lora:q_lora + kv_lora]
    ckv = c * lax.rsqrt(jnp.mean(c * c, axis=-1, keepdims=True) + RMS_EPS) * gkvl_ref[...]
    o = q_lora + kv_lora
    raw = a[:, o:o + rope]
    rot = a[:, o + rope:o + 2 * rope]
    r = lax.rsqrt(jnp.mean(raw * raw, axis=-1, keepdims=True) + RMS_EPS)
    kr = r * (raw * gkr_ref[...] * cos_ref[...] + rot * gkrp_ref[...] * sin_ref[...])

    @pl.when(is_prompt)
    def _():
        ckvp_ref[...] = ckv
        krp_ref[...] = kr

    @pl.when(jnp.logical_not(is_prompt))
    def _():
        ckvs_ref[...] = ckv
        krs_ref[...] = kr


def _q_up_kernel(cq_ref, w_ref, gq_ref, tab_ref, o_ref, *, heads, nope, rope, scale):
    a = _nn(cq_ref[...], w_ref[...])
    tab = tab_ref[...]
    gq = gq_ref[...]
    for h in range(heads):
        x = a[:, h * LANES:(h + 1) * LANES]
        lane = lax.broadcasted_iota(jnp.int32, x.shape, 1)
        x2 = x * x
        ssn = jnp.sum(jnp.where(lane < nope, x2, 0.0), axis=-1, keepdims=True)
        ssr = jnp.sum(jnp.where((lane >= nope) & (lane < nope + rope), x2, 0.0), axis=-1, keepdims=True)
        r = jnp.where(lane < nope, lax.rsqrt(ssn / nope + RMS_EPS), lax.rsqrt(ssr / rope + RMS_EPS))
        o_ref[h] = (x * r * gq * tab * scale).astype(BF16)


def _kv_up_kernel(ckv_ref, kr_ref, wukt_ref, gkn_ref, wuv_ref, eye_ref, kt_ref, v_ref, *, heads, nope, rope):
    c = ckv_ref[...].astype(BF16)
    tm = c.shape[0]
    knt = _nt(wukt_ref[...], c).reshape(heads, nope, tm)
    ss = jnp.sum(knt * knt, axis=1, keepdims=True)
    kn = knt * lax.rsqrt(ss / nope + RMS_EPS) * gkn_ref[...][None]
    krt = _nt(eye_ref[...], kr_ref[...].astype(BF16)).astype(BF16)
    krt = jnp.broadcast_to(krt[None], (heads, rope, tm))
    kt_ref[0, :, 0, 0:nope, :] = kn.astype(BF16)
    kt_ref[0, :, 0, nope:nope + rope, :] = krt
    kt_ref[0, :, 0, nope + rope:nope + 2 * rope, :] = krt
    v_ref[...] = _nn(c, wuv_ref[...]).astype(BF16)


def _mla_flash_kernel(q_ref, kt_ref, v_ref, o_ref, *, tq, tk, vdim):
    qi = pl.program_id(2)
    n_full = (qi * tq) // tk
    parts = tq // FLASH_ROWS
    qs = [[q_ref[hh, r * FLASH_ROWS:(r + 1) * FLASH_ROWS, :] for r in range(parts)] for hh in range(2)]

    def step(j, carry, masked):
        start = pl.multiple_of(j * tk, tk)
        v = v_ref[pl.ds(start, tk), :]
        out = []
        for hh in range(2):
            kt = kt_ref[0, hh, j]
            for r in range(parts):
                m, l, acc = carry[hh * parts + r]
                nk = (r + 1) * FLASH_ROWS if masked else tk
                s = _nn(qs[hh][r], kt[:, :nk])
                if masked:
                    row = r * FLASH_ROWS + lax.broadcasted_iota(jnp.int32, s.shape, 0)
                    col = lax.broadcasted_iota(jnp.int32, s.shape, 1)
                    s = jnp.where(col <= row, s, NEG_INF)
                m_new = jnp.maximum(m, jnp.max(s, axis=-1, keepdims=True))
                alpha = jnp.exp2(m - m_new)
                p = jnp.exp2(s - m_new)
                l = alpha * l + jnp.sum(p, axis=-1, keepdims=True)
                acc = alpha * acc + _nn(p.astype(BF16), v[:nk])
                out.append((m_new, l, acc))
        return tuple(out)

    init = tuple((jnp.full((FLASH_ROWS, 1), NEG_INF, F32), jnp.zeros((FLASH_ROWS, 1), F32),
                  jnp.zeros((FLASH_ROWS, LANES), F32)) for _ in range(2 * parts))
    carry = lax.fori_loop(0, n_full, lambda j, c: step(j, c, False), init)
    carry = step(n_full, carry, True)
    lane = lax.broadcasted_iota(jnp.int32, (FLASH_ROWS, LANES), 1)
    for r in range(parts):
        (_, l0, acc0), (_, l1, acc1) = carry[r], carry[parts + r]
        o_ref[r * FLASH_ROWS:(r + 1) * FLASH_ROWS, :] = jnp.where(lane < vdim, acc0 / l0, acc1 / l1).astype(BF16)


def _q_absorb_kernel(q_ref, wuk_ref, gkn_ref, qa_ref, qr_ref, *, heads, nope, rope):
    for h in range(heads):
        q = q_ref[h].astype(F32)
        qg = (q[:, :nope] * gkn_ref[...]).astype(BF16)
        qa_ref[h] = _nn(qg, wuk_ref[h]).astype(BF16)
        qr_ref[h] = (q[:, nope:nope + rope] + q[:, nope + rope:nope + 2 * rope]).astype(BF16)


def _mla_sample_kernel(pt_ref, qa_ref, qr_ref, wukt_ref, cnew_ref, rnew_ref, poolc_ref, poolrt_ref,
                       o_ref, cbuf, rbuf, cb0, cb1, sc0, sc1, sems, *, heads, nope, n_chunks, n_batch, t_new):
    n = pl.program_id(0)
    cp = cbuf.shape[1]
    rows = qa_ref.shape[1]
    lat = qa_ref.shape[2]
    nw = heads * nope
    ppt = SAMPLE_SUB // PAGE
    sets = ((cb0, sc0), (cb1, sc1))

    def page_copies(page, slot, p):
        return (pltpu.make_async_copy(poolc_ref.at[page], cbuf.at[slot, p], sems.at[0, slot]),
                pltpu.make_async_copy(poolrt_ref.at[page], rbuf.at[slot, p], sems.at[1, slot]))

    def start_pages(b, ci, slot, p0, p1):
        for p in range(p0, p1):
            for cpy in page_copies(pt_ref[b, ci * cp + p], slot, p):
                cpy.start()

    def wait_chunk(slot):
        for p in range(cp):
            for cpy in page_copies(0, slot, p):
                cpy.wait()

    @pl.when(n == 0)
    def _():
        start_pages(0, 0, 0, 0, cp)

    lhs = jnp.concatenate([wukt_ref[...], qa_ref[0]], axis=0)
    qr = qr_ref[0]

    def scores(c, rope_scores):
        tk = c.shape[0]
        big = _nt(lhs, c)
        knt = big[:nw].reshape(heads, nope, tk)
        r = lax.rsqrt(jnp.sum(knt * knt, axis=1) / nope + RMS_EPS)
        return big[nw:] * jnp.concatenate([r] * t_new, axis=0) + rope_scores

    def update(s, c, carry):
        m, l, acc = carry
        m_new = jnp.maximum(m, jnp.max(s, axis=-1, keepdims=True))
        alpha = jnp.exp2(m - m_new)
        p = jnp.exp2(s - m_new)
        l = alpha * l + jnp.sum(p, axis=-1, keepdims=True)
        acc = alpha * acc + _nn(p.astype(BF16), c)
        return m_new, l, acc

    def absorb(slot, carry):
        cbs, scs = sets[slot]
        return update(scs[...], cbs[...], carry)

    def score_chunk(slot, nxt_b, nxt_ci, carry, absorb_other):
        cbs, scs = sets[slot]
        nt = cp // ppt

        def tiles(j0, j1):
            for j in range(j0, j1):
                c = cbuf[slot, j * ppt:(j + 1) * ppt].reshape(SAMPLE_SUB, lat).astype(BF16)
                krt = jnp.concatenate([rbuf[slot, j * ppt + p] for p in range(ppt)], axis=1).astype(BF16)
                cbs[j * SAMPLE_SUB:(j + 1) * SAMPLE_SUB, :] = c
                scs[:, j * SAMPLE_SUB:(j + 1) * SAMPLE_SUB] = scores(c, _nn(qr, krt))

        wait_chunk(slot)
        start_pages(nxt_b, nxt_ci, 1 - slot, 0, cp // 2)
        tiles(0, nt // 2)
        if absorb_other:
            carry = absorb(1 - slot, carry)
        start_pages(nxt_b, nxt_ci, 1 - slot, cp // 2, cp)
        tiles(nt // 2, nt)
        return carry

    def pair(k, carry):
        carry = score_chunk(1, n, 2 * k + 2, carry, True)
        return score_chunk(0, n, 2 * k + 3, carry, True)

    init = (jnp.full((rows, 1), NEG_INF, F32), jnp.zeros((rows, 1), F32), jnp.zeros((rows, lat), F32))
    carry = score_chunk(0, n, 1, init, False)
    carry = lax.fori_loop(0, (n_chunks - 2) // 2, pair, carry)
    carry = score_chunk(1, jnp.minimum(n + 1, n_batch - 1), 0, carry, True)
    carry = absorb(1, carry)

    @pl.when(n == n_batch - 1)
    def _():
        wait_chunk(0)

    c = cnew_ref[0].astype(BF16)
    tk = c.shape[0]
    t_of_row = lax.broadcasted_iota(jnp.int32, (rows, tk), 0) // heads
    col = lax.broadcasted_iota(jnp.int32, (rows, tk), 1)
    s = jnp.where(col <= t_of_row, scores(c, _nt(qr, rnew_ref[0].astype(BF16))), NEG_INF)
    m, l, acc = update(s, c, carry)
    o_ref[0] = acc / l


def _unabsorb_kernel(o_ref, wuv_ref, out_ref, *, heads):
    out_ref[...] = jnp.concatenate(
        [_nn(o_ref[h], wuv_ref[h]) for h in range(heads)], axis=-1).astype(BF16)


def _oproj_router_kernel(attp_ref, atts_ref, w_ref, hp_ref, hs_ref, agate_ref, g_ref, sh_ref, sc_ref,
                         whi_ref, wlo_ref, b_ref, h_ref, u_ref, idx_ref, gate_ref, *, n_prompt_blocks):
    is_prompt = pl.program_id(0) < n_prompt_blocks
    att = jnp.where(is_prompt, attp_ref[...], atts_ref[...])
    y = _nn(att, w_ref[...])
    tm, d = y.shape
    agate = agate_ref[...]
    nb = agate.shape[0]
    h = jnp.where(is_prompt, hp_ref[...], hs_ref[...]) + (y.reshape(tm // nb, nb, d) * agate[None]).reshape(tm, d)
    h_ref[...] = h
    u = _norm_mod(h, g_ref[...], sh_ref[...], sc_ref[...])
    uhi = u.astype(BF16)
    ulo = (u - uhi.astype(F32)).astype(BF16)
    u_ref[...] = u
    whi = whi_ref[...]
    logits = _nt(whi, uhi) + _nt(whi, ulo) + _nt(wlo_ref[...], uhi) + b_ref[...]
    ne = logits.shape[0]
    eid = lax.broadcasted_iota(jnp.int32, logits.shape, 0)
    work = logits
    vals, idxs = [], []
    for _ in range(TOP_K):
        m = jnp.max(work, axis=0, keepdims=True)
        idx = jnp.min(jnp.where(work == m, eid, ne), axis=0, keepdims=True)
        vals.append(m)
        idxs.append(idx)
        work = jnp.where(eid == idx, -jnp.inf, work)
    es = [jnp.exp(v - vals[0]) for v in vals]
    den = es[0] + es[1] + es[2] + es[3]
    idx_ref[...] = jnp.concatenate(idxs, axis=0)
    gate_ref[...] = jnp.concatenate([e / den for e in es], axis=0)


def _rank_kernel(idx_ref, rank_ref, cnt_ref, carry, *, ne):
    @pl.when(pl.program_id(0) == 0)
    def _():
        carry[...] = jnp.zeros(carry.shape, F32)

    idx = idx_ref[...]
    tb = idx.shape[1]
    eid = lax.broadcasted_iota(jnp.int32, (ne, tb), 0)
    hits = [eid == idx[k:k + 1, :] for k in range(TOP_K)]
    oh = jnp.where(hits[0], 1.0, 0.0)
    for k in range(1, TOP_K):
        oh = oh + jnp.where(hits[k], 1.0, 0.0)
    upper = jnp.where(lax.broadcasted_iota(jnp.int32, (tb, tb), 0) < lax.broadcasted_iota(jnp.int32, (tb, tb), 1),
                      1.0, 0.0).astype(BF16)
    before = _nn(oh.astype(BF16), upper) + carry[...]
    rank_ref[...] = jnp.concatenate(
        [jnp.sum(jnp.where(h, before, 0.0), axis=0, keepdims=True) for h in hits], axis=0).astype(jnp.int32)
    carry[...] = carry[...] + jnp.sum(oh, axis=1, keepdims=True)
    cnt_ref[...] = carry[...]


def _expert_kernel(be_ref, nrows_ref, nused_ref, blk_ref, x_ref, wup_ref, bup_ref, wdn_ref, bdn_ref, y_ref, wup_bf, wdn_bf,
                   *, ff):
    i = pl.program_id(0)
    n_here = nrows_ref[i]
    prev = be_ref[jnp.maximum(i - 1, 0)]
    new_expert = (i == 0) | (be_ref[i] != prev)

    @pl.when(new_expert & (n_here > 0))
    def _():
        wup_bf[...] = wup_ref[0].astype(BF16)
        wdn_bf[...] = wdn_ref[0].astype(BF16)

    for r0 in range(0, MOE_TM, MOE_SUB):
        @pl.when(n_here > r0)
        def _(r0=r0):
            x = x_ref[r0:r0 + MOE_SUB, :].astype(BF16)
            fc = 512
            acc = jnp.zeros((MOE_SUB, wdn_bf.shape[1]), F32)
            for c0 in range(0, ff, fc):
                glu = _nn(x, wup_bf[:, c0:c0 + fc]) + bup_ref[0, :, c0:c0 + fc]
                lin = _nn(x, wup_bf[:, ff + c0:ff + c0 + fc]) + bup_ref[0, :, ff + c0:ff + c0 + fc]
                glu = jnp.minimum(glu, SWIGLU_LIMIT)
                lin = jnp.clip(lin, -SWIGLU_LIMIT, SWIGLU_LIMIT)
                act = glu * jax.nn.sigmoid(SWIGLU_ALPHA * glu) * (lin + 1.0)
                acc = acc + _nn(act.astype(BF16), wdn_bf[c0:c0 + fc, :])
            y_ref[r0:r0 + MOE_SUB, :] = _pack_bf16_pairs(acc + bdn_ref[0])

        @pl.when(n_here <= r0)
        def _(r0=r0):
            y_ref[r0:r0 + MOE_SUB, :] = jnp.zeros((MOE_SUB, y_ref.shape[1]), jnp.uint32)


def _combine_value(h_ref, y_ref, gk_ref, gate_ref):
    gk = gk_ref[...]
    ya, yb = _unpack_bf16_pairs(y_ref[0])
    ya, yb = ya * gk[:, 0:1], yb * gk[:, 0:1]
    for k in range(1, TOP_K):
        a, b = _unpack_bf16_pairs(y_ref[k])
        ya, yb = ya + a * gk[:, k:k + 1], yb + b * gk[:, k:k + 1]
    y = jnp.concatenate([ya, yb], axis=1)
    tm, d = y.shape
    gate = gate_ref[...]
    nb = gate.shape[0]
    return h_ref[...] + (y.reshape(tm // nb, nb, d) * gate[None]).reshape(tm, d)


def _combine_split_kernel(h_ref, y_ref, gk_ref, gate_ref, op_ref, os_ref, *, n_prompt_blocks):
    res = _combine_value(h_ref, y_ref, gk_ref, gate_ref)

    @pl.when(pl.program_id(0) < n_prompt_blocks)
    def _():
        op_ref[...] = res

    @pl.when(pl.program_id(0) >= n_prompt_blocks)
    def _():
        os_ref[...] = res


def _combine_kv_q_kernel(h_ref, y_ref, gk_ref, gate_ref, gkv_ref, kvsh_ref, kvsc_ref, wkv_ref, gk64_ref,
                         ga_ref, ash_ref, asc_ref, wq_ref, gq_ref,
                         h_out, k_ref, v_ref, khm_ref, vhm_ref, q_ref, *, kvh, hd, heads, scale):
    res = _combine_value(h_ref, y_ref, gk_ref, gate_ref)
    h_out[...] = res
    _shared_kv(res, gkv_ref, kvsh_ref, kvsc_ref, wkv_ref, gk64_ref, k_ref, v_ref, khm_ref, vhm_ref, kvh=kvh, hd=hd)
    _swa_q(res, ga_ref, ash_ref, asc_ref, wq_ref, gq_ref, q_ref, heads=heads, hd=hd, scale=scale)


def _shared_kv(h, g_ref, sh_ref, sc_ref, w_ref, gk_ref, k_ref, v_ref, khm_ref, vhm_ref, *, kvh, hd):
    u = _norm_mod(h, g_ref[...], sh_ref[...], sc_ref[...]).astype(BF16)
    a = _nn(u, w_ref[...])
    kw = kvh * hd
    gk = gk_ref[...]
    ks = []
    for j in range(kw // LANES):
        x = a[:, j * LANES:(j + 1) * LANES]
        ks.append(x * _group64_rscale(x) * gk[:, j * LANES:(j + 1) * LANES])
    k = jnp.concatenate(ks, axis=-1)
    v = a[:, kw:2 * kw]
    k_ref[...] = k
    v_ref[...] = v
    for hh in range(kvh):
        khm_ref[hh] = k[:, hh * hd:(hh + 1) * hd].astype(BF16)
        vhm_ref[hh] = v[:, hh * hd:(hh + 1) * hd].astype(BF16)


def _swa_q(h, g_ref, sh_ref, sc_ref, w_ref, gq_ref, q_ref, *, heads, hd, scale):
    u = _norm_mod(h, g_ref[...], sh_ref[...], sc_ref[...]).astype(BF16)
    a = _nn(u, w_ref[...])
    gq = gq_ref[...]
    for j in range(heads * hd // LANES):
        x = a[:, j * LANES:(j + 1) * LANES]
        y = x * _group64_rscale(x) * gq[:, j * LANES:(j + 1) * LANES] * scale
        q_ref[2 * j] = y[:, :hd].astype(BF16)
        q_ref[2 * j + 1] = y[:, hd:].astype(BF16)


def _softmax_pv(s, v):
    m = jnp.max(s, axis=-1, keepdims=True)
    e = jnp.exp(s - m)
    return _nn(e.astype(BF16), v) / jnp.sum(e, axis=-1, keepdims=True)


def _swa_prompt_kernel(bias_ref, q_ref, kp_ref, kc_ref, vp_ref, vc_ref, o_ref, *, heads, kvh, tq):
    group = heads // kvh
    rows = group * tq
    not_first = lax.broadcasted_iota(jnp.int32, (WINDOW + tq, q_ref.shape[2]), 0) > 0
    outs = []
    for g in range(kvh):
        k = jnp.where(not_first, jnp.concatenate([kp_ref[g], kc_ref[g]], axis=0), 0)
        v = jnp.where(not_first, jnp.concatenate([vp_ref[g], vc_ref[g]], axis=0), 0)
        q = q_ref[g * group:(g + 1) * group].reshape(rows, q_ref.shape[2])
        o = _softmax_pv(_nt(q, k) + bias_ref[g], v)
        for j in range(group):
            outs.append(o[j * tq:(j + 1) * tq])
    o_ref[...] = jnp.concatenate(outs, axis=-1).astype(BF16)


def _swa_sample_kernel(bias_ref, q_ref, k_ref, v_ref, o_ref):
    bias = bias_ref[...]
    for b in range(q_ref.shape[0]):
        o_ref[b] = _softmax_pv(_nt(q_ref[b], k_ref[b]) + bias, v_ref[b]).astype(BF16)


def _rope_tables(pos, rope):
    half = rope // 2
    inv = ROPE_THETA ** (-jnp.arange(half, dtype=F32) / half)
    ang = pos.astype(F32)[:, None] * inv[None, :]
    cos, sin = jnp.cos(ang), jnp.sin(ang)
    return jnp.concatenate([cos, cos], axis=-1), jnp.concatenate([sin, sin], axis=-1)


def _rot_cols(w):
    half = w.shape[-1] // 2
    return jnp.concatenate([-w[..., half:], w[..., :half]], axis=-1)


def _swap_halves(g):
    half = g.shape[-1] // 2
    return jnp.concatenate([g[..., half:], g[..., :half]], axis=-1)


def _attn_out_moe(att_p, att_s, w_o, h_p, h_s, h_s_idx, m_rows, tab, layer, g_ffn, w_router, b_router,
                  w_up, b_up, w_down, b_down, *, seq, nb_prompt, n_prompt_blocks, split_rows=None,
                  next_layer=None):
    d = h_p.shape[1]
    ne = w_router.shape[1]
    ff = w_down.shape[2]
    nb = tab.shape[2]
    n_tok_blocks = m_rows // TM
    col0 = 3

    def tab_spec(col):
        return pl.BlockSpec((None, None, nb, d),
                            lambda i: (layer, jnp.minimum(i * TM // seq, nb_prompt), 0, col))

    def prompt_blk(i):
        return (jnp.minimum(i, n_prompt_blocks - 1), 0)

    def sample_blk(i):
        return (jnp.maximum(i - n_prompt_blocks, 0), 0)

    wr_t = w_router.T
    wr_hi = wr_t.astype(BF16)
    wr_lo = (wr_t - wr_hi.astype(F32)).astype(BF16)
    h, u, idx_t, gate_t = pl.pallas_call(
        functools.partial(_oproj_router_kernel, n_prompt_blocks=n_prompt_blocks),
        grid=(n_tok_blocks,),
        in_specs=[pl.BlockSpec((TM, w_o.shape[0]), prompt_blk), pl.BlockSpec((TM, w_o.shape[0]), sample_blk),
                  pl.BlockSpec(w_o.shape, lambda i: (0, 0)),
                  pl.BlockSpec((TM, d), prompt_blk), pl.BlockSpec((TM, d), h_s_idx), tab_spec(2),
                  pl.BlockSpec((1, d), lambda i: (0, 0)),
                  tab_spec(col0), tab_spec(col0 + 1),
                  pl.BlockSpec((ne, d), lambda i: (0, 0)),
                  pl.BlockSpec((ne, d), lambda i: (0, 0)),
                  pl.BlockSpec((ne, 1), lambda i: (0, 0))],
        out_specs=[pl.BlockSpec((TM, d), lambda i: (i, 0)),
                   pl.BlockSpec((TM, d), lambda i: (i, 0)),
                   pl.BlockSpec((TOP_K, TM), lambda i: (0, i)),
                   pl.BlockSpec((TOP_K, TM), lambda i: (0, i))],
        out_shape=[jax.ShapeDtypeStruct((m_rows, d), F32),
                   jax.ShapeDtypeStruct((m_rows, d), F32),
                   jax.ShapeDtypeStruct((TOP_K, m_rows), jnp.int32),
                   jax.ShapeDtypeStruct((TOP_K, m_rows), F32)],
        compiler_params=_params("parallel"),
        name="oproj_router",
    )(att_p, att_s, w_o.astype(BF16), h_p, h_s, tab, g_ffn.reshape(1, d), tab, tab, wr_hi, wr_lo,
      b_router.reshape(ne, 1))

    a = m_rows * TOP_K
    rank_t, cnt = pl.pallas_call(
        functools.partial(_rank_kernel, ne=ne),
        grid=(n_tok_blocks,),
        in_specs=[pl.BlockSpec((TOP_K, TM), lambda i: (0, i))],
        out_specs=[pl.BlockSpec((TOP_K, TM), lambda i: (0, i)), pl.BlockSpec((ne, 1), lambda i: (0, 0))],
        out_shape=[jax.ShapeDtypeStruct((TOP_K, m_rows), jnp.int32), jax.ShapeDtypeStruct((ne, 1), F32)],
        scratch_shapes=[pltpu.VMEM((ne, 1), F32)],
        compiler_params=_params("arbitrary"),
        name="moe_rank",
    )(idx_t)
    counts = cnt[:, 0].astype(jnp.int32)
    padded = (counts + MOE_TM - 1) // MOE_TM * MOE_TM
    pad_end = jnp.cumsum(padded)
    start = pad_end - padded
    experts = jnp.arange(ne, dtype=jnp.int32)
    pos_t = rank_t + jnp.sum(jnp.where(idx_t[..., None] == experts, start, 0), axis=-1)
    fbits = (a - 1).bit_length()
    assert ne << fbits < 2 ** 31
    keys = (idx_t.T.reshape(a) << fbits) | jnp.arange(a, dtype=jnp.int32)
    order = jnp.sort(keys) & ((1 << fbits) - 1)
    shift = start - (jnp.cumsum(counts) - counts)
    n_blk = -(-a // MOE_TM) + ne
    n_slots = n_blk * MOE_TM
    blk_start = jnp.arange(n_blk, dtype=jnp.int32) * MOE_TM
    blk_expert = jnp.minimum(jnp.sum(pad_end[None, :] <= blk_start[:, None], axis=1), ne - 1).astype(jnp.int32)
    n_used = (pad_end[-1] // MOE_TM).astype(jnp.int32).reshape(1)
    blk_rows = jnp.clip((pad_end - padded + counts)[blk_expert] - blk_start, 0, MOE_TM)
    blk_rows = jnp.where(blk_start < pad_end[-1], blk_rows, 0).astype(jnp.int32)
    slot_sorted = jnp.arange(n_slots, dtype=jnp.int32) - jnp.repeat(shift[blk_expert], MOE_TM)
    slot_tok = order[jnp.clip(slot_sorted, 0, a - 1)] // TOP_K
    xs = jnp.take(u, slot_tok, axis=0, mode="clip")
    steps = jnp.arange(n_blk, dtype=jnp.int32)
    blk_of_step = jnp.where(steps < n_used[0], (start // MOE_TM + pad_end // MOE_TM - 1)[blk_expert] - steps, steps)
    blk_of_step = blk_of_step.astype(jnp.int32)
    step_rows = blk_rows[blk_of_step]

    def live(i, nused):
        return jnp.minimum(i, nused[0] - 1)

    yb = pl.pallas_call(
        functools.partial(_expert_kernel, ff=ff),
        grid_spec=pltpu.PrefetchScalarGridSpec(
            num_scalar_prefetch=4,
            grid=(n_blk,),
            in_specs=[pl.BlockSpec((MOE_TM, d), lambda i, be, nr, nu, bs: (bs[live(i, nu)], 0)),
                      pl.BlockSpec((None, 1, d, 2 * ff), lambda i, be, nr, nu, bs: (layer, be[live(i, nu)], 0, 0)),
                      pl.BlockSpec((None, 1, 1, 2 * ff), lambda i, be, nr, nu, bs: (layer, be[live(i, nu)], 0, 0)),
                      pl.BlockSpec((None, 1, ff, d), lambda i, be, nr, nu, bs: (layer, be[live(i, nu)], 0, 0)),
                      pl.BlockSpec((None, 1, 1, d), lambda i, be, nr, nu, bs: (layer, be[live(i, nu)], 0, 0))],
            out_specs=pl.BlockSpec((MOE_TM, d // 2), lambda i, be, nr, nu, bs: (bs[i], 0)),
            scratch_shapes=[pltpu.VMEM((d, 2 * ff), BF16), pltpu.VMEM((ff, d), BF16)]),
        out_shape=jax.ShapeDtypeStruct((n_slots, d // 2), jnp.uint32),
        compiler_params=_params("arbitrary"),
        name="experts",
    )(blk_expert, step_rows, n_used, blk_of_step, xs, w_up, b_up.reshape(b_up.shape[0], ne, 1, 2 * ff), w_down,
      b_down.reshape(b_down.shape[0], ne, 1, d))

    ysel = jnp.take(yb, pos_t.reshape(a), axis=0, mode="clip").reshape(TOP_K, m_rows, d // 2)
    combine_specs = [pl.BlockSpec((TM, d), lambda i: (i, 0)),
                     pl.BlockSpec((TOP_K, TM, d // 2), lambda i: (0, i, 0)),
                     pl.BlockSpec((TM, TOP_K), lambda i: (i, 0)),
                     tab_spec(col0 + 2)]
    combine_args = (h, ysel, gate_t.T, tab)
    if split_rows is not None:
        return pl.pallas_call(
            functools.partial(_combine_split_kernel, n_prompt_blocks=n_prompt_blocks),
            grid=(n_tok_blocks,),
            in_specs=combine_specs,
            out_specs=[pl.BlockSpec((TM, d), prompt_blk), pl.BlockSpec((TM, d), sample_blk)],
            out_shape=[jax.ShapeDtypeStruct((split_rows, d), F32),
                       jax.ShapeDtypeStruct((m_rows - split_rows, d), F32)],
            compiler_params=_params("arbitrary"),
            name="moe_combine",
        )(*combine_args)
    kvh, hd, heads = next_layer["kvh"], next_layer["hd"], next_layer["heads"]
    kw = kvh * hd
    tab_kv = next_layer["tab_kv"]

    def next_tab_spec(which_layer, col):
        return pl.BlockSpec((None, None, nb, d),
                            lambda i: (which_layer, jnp.minimum(i * TM // seq, nb_prompt), 0, col))

    row = pl.BlockSpec((TM, d), lambda i: (i, 0))
    return pl.pallas_call(
        functools.partial(_combine_kv_q_kernel, kvh=kvh, hd=hd, heads=heads, scale=hd ** -0.5),
        grid=(n_tok_blocks,),
        in_specs=combine_specs + [
            pl.BlockSpec((1, d), lambda i: (0, 0)), next_tab_spec(0, 0), next_tab_spec(0, 1),
            pl.BlockSpec((d, 2 * kw), lambda i: (0, 0)), pl.BlockSpec((1, kw), lambda i: (0, 0)),
            pl.BlockSpec((1, d), lambda i: (0, 0)), next_tab_spec(layer + 1, 0), next_tab_spec(layer + 1, 1),
            pl.BlockSpec((d, heads * hd), lambda i: (0, 0)), pl.BlockSpec((1, heads * hd), lambda i: (0, 0))],
        out_specs=[row, pl.BlockSpec((TM, kw), lambda i: (i, 0)), pl.BlockSpec((TM, kw), lambda i: (i, 0)),
                   pl.BlockSpec((kvh, TM, hd), lambda i: (0, i, 0)), pl.BlockSpec((kvh, TM, hd), lambda i: (0, i, 0)),
                   pl.BlockSpec((heads, TM, hd), lambda i: (0, i, 0))],
        out_shape=[jax.ShapeDtypeStruct((m_rows, d), F32),
                   jax.ShapeDtypeStruct((m_rows, kw), F32), jax.ShapeDtypeStruct((m_rows, kw), F32),
                   jax.ShapeDtypeStruct((kvh, m_rows, hd), BF16), jax.ShapeDtypeStruct((kvh, m_rows, hd), BF16),
                   jax.ShapeDtypeStruct((heads, m_rows, hd), BF16)],
        compiler_params=_params("parallel"),
        name="moe_combine_kv_q",
    )(*combine_args, next_layer["g_kv_norm"].reshape(1, d), tab_kv, tab_kv, next_layer["w_kv"].astype(BF16),
      jnp.tile(next_layer["g_k"], kvh).reshape(1, kw),
      next_layer["g_attn"].reshape(1, d), tab, tab, next_layer["w_q"].astype(BF16),
      jnp.tile(next_layer["g_q"], heads).reshape(1, heads * hd))


def kernel(x_prompt, x_sample, c_prompt, c_sample, cache_mla_latent, cache_mla_krope, state_win_k, state_win_v, page_table, w_mod, b_mod, g_attn, g_ffn, w_mla_down, g_mla_q_lora, g_mla_kv_lora, w_mla_uq, w_mla_uk, w_mla_uv, g_mla_qn, g_mla_qr, g_mla_kn, g_mla_kr, w_mla_o, w_kvmod, b_kvmod, g_kv_norm, w_kv, g_swa_k, w_swa_q, g_swa_q, swa_sinks, w_swa_o, w_router, b_router, w_up, b_up, w_down, b_down):
    nbp, seq, d = x_prompt.shape
    nbs, t_new, _ = x_sample.shape
    q_lora = g_mla_q_lora.shape[1]
    kv_lora = g_mla_kv_lora.shape[1]
    heads, nope = w_mla_uk.shape[2], w_mla_uk.shape[3]
    rope = g_mla_qr.shape[1]
    vdim = w_mla_uv.shape[3]
    n_pages = page_table.shape[1]
    past = n_pages * PAGE
    swa_heads = swa_sinks.shape[1]
    hd = g_swa_k.shape[0]
    kvh = w_kv.shape[1] // (2 * hd)
    assert nope + 2 * rope == LANES and 2 * vdim == LANES and 2 * hd == LANES and TM % FLASH_ROWS == 0
    assert seq % TM == 0 and TM % nbs == 0 and (nbs * t_new) % TM == 0 and nbs % 8 == 0
    assert w_mod.shape[0] == 2 and n_pages % (2 * SAMPLE_PAGES) == 0

    mp = nbp * seq
    ms = nbs * t_new
    m_rows = mp + ms
    n_tok_blocks = m_rows // TM
    n_prompt_blocks = mp // TM
    blocks_per_seq = seq // TM

    def tab_idx(i):
        return jnp.minimum(i * TM // seq, nbp)

    def tok(i):
        return (i, 0)

    def const2(i):
        return (0, 0)

    h0_p = x_prompt.reshape(mp, d)
    h0_s = x_sample.transpose(1, 0, 2).reshape(ms, d)

    def prompt_blk(i):
        return (jnp.minimum(i, n_prompt_blocks - 1), 0)

    def sample_blk(i):
        return (jnp.maximum(i - n_prompt_blocks, 0), 0)

    c_all = jnp.concatenate([c_prompt, c_sample], axis=0)
    tab = _adaln_table(c_all, w_mod, b_mod, nbp)
    tab_kv = _adaln_table(c_all, w_kvmod[None], b_kvmod[None], nbp)

    def tab_spec(layer, col):
        return pl.BlockSpec((None, None, nbs, d), lambda i: (layer, tab_idx(i), 0, col))

    cos_p, sin_p = _rope_tables(jnp.arange(seq), rope)
    cos_s, sin_s = _rope_tables(past + jnp.arange(t_new), rope)
    cos_tab = jnp.concatenate([cos_p, jnp.repeat(cos_s, nbs, axis=0)], axis=0)
    sin_tab = jnp.concatenate([sin_p, jnp.repeat(sin_s, nbs, axis=0)], axis=0)

    def pos_blk(i):
        return (jnp.where(i < n_prompt_blocks, i % blocks_per_seq, blocks_per_seq + i - n_prompt_blocks), 0)

    wd = w_mla_down[0]
    w_down_ext = jnp.concatenate([wd, _rot_cols(wd[:, q_lora + kv_lora:])], axis=1).astype(BF16)
    nd = w_down_ext.shape[1]
    cq, ckv_p, ckv_s, kr_p, kr_s = pl.pallas_call(
        functools.partial(_mla_down_kernel, q_lora=q_lora, kv_lora=kv_lora, rope=rope,
                          n_prompt_blocks=n_prompt_blocks),
        grid=(n_tok_blocks,),
        in_specs=[pl.BlockSpec((TM, d), prompt_blk), pl.BlockSpec((TM, d), sample_blk),
                  pl.BlockSpec((1, d), const2), tab_spec(0, 0), tab_spec(0, 1),
                  pl.BlockSpec((d, nd), const2), pl.BlockSpec((1, q_lora), const2),
                  pl.BlockSpec((1, kv_lora), const2), pl.BlockSpec((1, rope), const2),
                  pl.BlockSpec((1, rope), const2), pl.BlockSpec((TM, rope), pos_blk),
                  pl.BlockSpec((TM, rope), pos_blk)],
        out_specs=[pl.BlockSpec((TM, q_lora), tok),
                   pl.BlockSpec((TM, kv_lora), prompt_blk), pl.BlockSpec((TM, kv_lora), sample_blk),
                   pl.BlockSpec((TM, rope), prompt_blk), pl.BlockSpec((TM, rope), sample_blk)],
        out_shape=[jax.ShapeDtypeStruct((m_rows, q_lora), BF16),
                   jax.ShapeDtypeStruct((mp, kv_lora), F32), jax.ShapeDtypeStruct((ms, kv_lora), F32),
                   jax.ShapeDtypeStruct((mp, rope), F32), jax.ShapeDtypeStruct((ms, rope), F32)],
        compiler_params=_params("arbitrary"),
        name="mla_down",
    )(h0_p, h0_s, g_attn[0].reshape(1, d), tab, tab, w_down_ext, g_mla_q_lora, g_mla_kv_lora,
      g_mla_kr, _swap_halves(g_mla_kr), cos_tab, sin_tab)

    wq = w_mla_uq[0]
    w_q_cat = jnp.concatenate([wq, _rot_cols(wq[..., nope:])], axis=-1).reshape(q_lora, heads * LANES).astype(BF16)
    gq_cat = jnp.concatenate([g_mla_qn[0], g_mla_qr[0], _swap_halves(g_mla_qr[0])]).reshape(1, LANES)
    q_tab = jnp.concatenate([jnp.ones((cos_tab.shape[0], nope), F32), cos_tab, sin_tab], axis=1)
    mla_scale = (nope + rope) ** -0.5 * LOG2_E
    q_cat = pl.pallas_call(
        functools.partial(_q_up_kernel, heads=heads, nope=nope, rope=rope, scale=mla_scale),
        grid=(n_tok_blocks,),
        in_specs=[pl.BlockSpec((TM, q_lora), tok), pl.BlockSpec((q_lora, heads * LANES), const2),
                  pl.BlockSpec((1, LANES), const2), pl.BlockSpec((TM, LANES), pos_blk)],
        out_specs=pl.BlockSpec((heads, TM, LANES), lambda i: (0, i, 0)),
        out_shape=jax.ShapeDtypeStruct((heads, m_rows, LANES), BF16),
        compiler_params=_params("parallel"),
        name="q_up",
    )(cq, w_q_cat, gq_cat, q_tab)

    wuk = w_mla_uk[0]
    wuv = w_mla_uv[0]
    wuk_t = wuk.reshape(kv_lora, heads * nope).T.astype(BF16)
    tk = TM
    nk = seq // tk
    kt, v_p = pl.pallas_call(
        functools.partial(_kv_up_kernel, heads=heads, nope=nope, rope=rope),
        grid=(n_prompt_blocks,),
        in_specs=[pl.BlockSpec((TM, kv_lora), tok), pl.BlockSpec((TM, rope), tok),
                  pl.BlockSpec((heads * nope, kv_lora), const2), pl.BlockSpec((nope, 1), const2),
                  pl.BlockSpec((kv_lora, heads * vdim), const2), pl.BlockSpec((rope, rope), const2)],
        out_specs=[pl.BlockSpec((1, heads, 1, LANES, tk), lambda i: (i // nk, 0, i % nk, 0, 0)),
                   pl.BlockSpec((TM, heads * vdim), tok)],
        out_shape=[jax.ShapeDtypeStruct((nbp, heads, nk, LANES, tk), BF16),
                   jax.ShapeDtypeStruct((mp, heads * vdim), BF16)],
        compiler_params=_params("parallel"),
        name="kv_up",
    )(ckv_p, kr_p, wuk_t, g_mla_kn[0].reshape(nope, 1), wuv.reshape(kv_lora, heads * vdim).astype(BF16),
      jnp.eye(rope, dtype=BF16))

    tq = tk
    nq = seq // tq
    att_p = pl.pallas_call(
        functools.partial(_mla_flash_kernel, tq=tq, tk=tk, vdim=vdim),
        grid=(nbp, heads // 2, nq),
        in_specs=[pl.BlockSpec((2, tq, LANES), lambda b, hp, qi: (hp, b * nq + qi, 0)),
                  pl.BlockSpec((1, 2, nk, LANES, tk), lambda b, hp, qi: (b, hp, 0, 0, 0)),
                  pl.BlockSpec((seq, LANES), lambda b, hp, qi: (b, hp))],
        out_specs=pl.BlockSpec((tq, LANES), lambda b, hp, qi: (b * nq + qi, hp)),
        out_shape=jax.ShapeDtypeStruct((mp, heads * vdim), BF16),
        compiler_params=_params("parallel", "parallel", "arbitrary"),
        name="mla_prompt_attention",
    )(q_cat, kt, v_p)

    first_sample_blk = mp // ms
    qa, qr = pl.pallas_call(
        functools.partial(_q_absorb_kernel, heads=heads, nope=nope, rope=rope),
        grid=(1,),
        in_specs=[pl.BlockSpec((heads, ms, LANES), lambda i: (0, first_sample_blk, 0)),
                  pl.BlockSpec((heads, nope, kv_lora), lambda i: (0, 0, 0)),
                  pl.BlockSpec((1, nope), const2)],
        out_specs=[pl.BlockSpec((heads, ms, kv_lora), lambda i: (0, 0, 0)),
                   pl.BlockSpec((heads, ms, rope), lambda i: (0, 0, 0))],
        out_shape=[jax.ShapeDtypeStruct((heads, ms, kv_lora), BF16),
                   jax.ShapeDtypeStruct((heads, ms, rope), BF16)],
        compiler_params=_params("arbitrary"),
        name="q_absorb",
    )(q_cat, wuk.transpose(1, 2, 0).astype(BF16), g_mla_kn)
    rows = t_new * heads

    def per_batch(x):
        return x.reshape(heads, t_new, nbs, -1).transpose(2, 1, 0, 3).reshape(nbs, rows, -1)

    def new_rows(x):
        x = x.reshape(t_new, nbs, -1).transpose(1, 0, 2)
        return jnp.pad(x, ((0, 0), (0, 8 - t_new), (0, 0)))

    n_chunks = n_pages // SAMPLE_PAGES
    sample_tokens = SAMPLE_PAGES * PAGE
    o_lat = pl.pallas_call(
        functools.partial(_mla_sample_kernel, heads=heads, nope=nope, n_chunks=n_chunks, n_batch=nbs,
                          t_new=t_new),
        grid_spec=pltpu.PrefetchScalarGridSpec(
            num_scalar_prefetch=1,
            grid=(nbs,),
            in_specs=[pl.BlockSpec((1, rows, kv_lora), lambda n, pt: (n, 0, 0)),
                      pl.BlockSpec((1, rows, rope), lambda n, pt: (n, 0, 0)),
                      pl.BlockSpec((heads * nope, kv_lora), lambda n, pt: (0, 0)),
                      pl.BlockSpec((1, 8, kv_lora), lambda n, pt: (n, 0, 0)),
                      pl.BlockSpec((1, 8, rope), lambda n, pt: (n, 0, 0)),
                      pl.BlockSpec(memory_space=pl.ANY),
                      pl.BlockSpec(memory_space=pl.ANY)],
            out_specs=pl.BlockSpec((1, rows, kv_lora), lambda n, pt: (n, 0, 0)),
            scratch_shapes=[pltpu.VMEM((2, SAMPLE_PAGES, PAGE, kv_lora), F32),
                            pltpu.VMEM((2, SAMPLE_PAGES, rope, PAGE), F32),
                            pltpu.VMEM((sample_tokens, kv_lora), BF16),
                            pltpu.VMEM((sample_tokens, kv_lora), BF16),
                            pltpu.VMEM((rows, sample_tokens), F32),
                            pltpu.VMEM((rows, sample_tokens), F32),
                            pltpu.SemaphoreType.DMA((2, 2))]),
        out_shape=jax.ShapeDtypeStruct((nbs, rows, kv_lora), F32),
        compiler_params=_params("arbitrary"),
        name="mla_sample_attention",
    )(page_table, per_batch(qa), per_batch(qr), wuk_t, new_rows(ckv_s), new_rows(kr_s),
      cache_mla_latent[0], cache_mla_krope[0].transpose(0, 2, 1))
    o_lat = o_lat.reshape(nbs, t_new, heads, kv_lora).transpose(2, 1, 0, 3).reshape(heads, ms, kv_lora)
    att_s = pl.pallas_call(
        functools.partial(_unabsorb_kernel, heads=heads),
        grid=(1,),
        in_specs=[pl.BlockSpec((heads, ms, kv_lora), lambda i: (0, 0, 0)),
                  pl.BlockSpec((heads, kv_lora, vdim), lambda i: (0, 0, 0))],
        out_specs=pl.BlockSpec((ms, heads * vdim), const2),
        out_shape=jax.ShapeDtypeStruct((ms, heads * vdim), BF16),
        compiler_params=_params("arbitrary"),
        name="unabsorb",
    )(o_lat.astype(BF16), wuv.transpose(1, 0, 2).astype(BF16))

    def sample_blk_of_all(i):
        return (n_prompt_blocks + jnp.maximum(i - n_prompt_blocks, 0), 0)

    h1, k_all, v_all, k_hm, v_hm, q_hm = _attn_out_moe(
        att_p, att_s, w_mla_o[0], h0_p, h0_s, sample_blk, m_rows, tab, 0, g_ffn[0],
        w_router[0], b_router[0], w_up, b_up, w_down, b_down,
        seq=seq, nb_prompt=nbp, n_prompt_blocks=n_prompt_blocks,
        next_layer=dict(kvh=kvh, hd=hd, heads=swa_heads, tab_kv=tab_kv, g_kv_norm=g_kv_norm, w_kv=w_kv,
                        g_k=g_swa_k, g_attn=g_attn[1], w_q=w_swa_q[0], g_q=g_swa_q[0]))


    group = swa_heads // kvh
    slopes = 2.0 ** (-8.0 * jnp.arange(1, swa_heads + 1, dtype=F32) / swa_heads)
    sinks = swa_sinks[0].astype(F32)

    def bias_table(dist, valid, sink_col, head_of_row):
        b = jnp.where(valid, -slopes[head_of_row][:, None] * dist.astype(F32), NEG_INF)
        return jnp.where(sink_col, sinks[head_of_row][:, None], b)

    tq = WINDOW
    wblocks = seq // tq
    r = jnp.arange(group * tq)
    col = jnp.arange(WINDOW + tq)
    dist = (r % tq)[:, None] + WINDOW - col[None, :]
    inside = (dist >= 0) & (dist < WINDOW)
    bias_p = jnp.stack([
        jnp.stack([bias_table(dist, inside & ok[None, :], (col == 0)[None, :], g * group + r // tq)
                   for g in range(kvh)])
        for ok in (col >= 0, col >= WINDOW)])
    att_p = pl.pallas_call(
        functools.partial(_swa_prompt_kernel, heads=swa_heads, kvh=kvh, tq=tq),
        grid=(nbp, wblocks),
        in_specs=[pl.BlockSpec((None, kvh, group * tq, WINDOW + tq), lambda b, i: (jnp.where(i == 0, 1, 0), 0, 0, 0)),
                  pl.BlockSpec((swa_heads, tq, hd), lambda b, i: (0, b * wblocks + i, 0)),
                  pl.BlockSpec((kvh, WINDOW, hd), lambda b, i: (0, jnp.maximum(b * wblocks + i - 1, 0), 0)),
                  pl.BlockSpec((kvh, tq, hd), lambda b, i: (0, b * wblocks + i, 0)),
                  pl.BlockSpec((kvh, WINDOW, hd), lambda b, i: (0, jnp.maximum(b * wblocks + i - 1, 0), 0)),
                  pl.BlockSpec((kvh, tq, hd), lambda b, i: (0, b * wblocks + i, 0))],
        out_specs=pl.BlockSpec((tq, swa_heads * hd), lambda b, i: (b * wblocks + i, 0)),
        out_shape=jax.ShapeDtypeStruct((mp, swa_heads * hd), BF16),
        compiler_params=_params("parallel", "arbitrary"),
        name="swa_prompt_attention",
    )(bias_p, q_hm, k_hm, k_hm, v_hm, v_hm)

    tpad = 8
    nkg = WINDOW + tpad

    def stack_keys(win, new_hm):
        new = new_hm[:, mp:].reshape(kvh, t_new, nbs, hd).transpose(2, 0, 1, 3)
        new = jnp.pad(new, ((0, 0), (0, 0), (0, tpad - t_new), (0, 0)))
        return jnp.concatenate([win.transpose(0, 2, 1, 3).astype(BF16), new], axis=2).reshape(nbs, kvh * nkg, hd)

    rs = jnp.arange(t_new * swa_heads)
    t_of, h_of = rs // swa_heads, rs % swa_heads
    cs = jnp.arange(kvh * nkg)
    j = cs % nkg
    dist_s = jnp.where(j < WINDOW, WINDOW + t_of[:, None] - j[None, :], t_of[:, None] - (j[None, :] - WINDOW))
    own = (cs // nkg)[None, :] == (h_of // group)[:, None]
    ok_s = own & (dist_s >= 0) & (dist_s < WINDOW) & ((j < WINDOW + t_new)[None, :])
    bias_s = bias_table(dist_s, ok_s, own & ((j == WINDOW + t_new)[None, :]), h_of)
    q_s = q_hm[:, mp:].reshape(swa_heads, t_new, nbs, hd).transpose(2, 1, 0, 3).reshape(nbs, t_new * swa_heads, hd)
    nbi = 8
    att_s = pl.pallas_call(
        _swa_sample_kernel,
        grid=(nbs // nbi,),
        in_specs=[pl.BlockSpec(bias_s.shape, const2),
                  pl.BlockSpec((nbi, t_new * swa_heads, hd), lambda i: (i, 0, 0)),
                  pl.BlockSpec((nbi, kvh * nkg, hd), lambda i: (i, 0, 0)),
                  pl.BlockSpec((nbi, kvh * nkg, hd), lambda i: (i, 0, 0))],
        out_specs=pl.BlockSpec((nbi, t_new * swa_heads, hd), lambda i: (i, 0, 0)),
        out_shape=jax.ShapeDtypeStruct((nbs, t_new * swa_heads, hd), BF16),
        compiler_params=_params("parallel"),
        name="swa_sample_attention",
    )(bias_s, q_s, stack_keys(state_win_k, k_hm), stack_keys(state_win_v, v_hm))
    att_s = att_s.reshape(nbs, t_new, swa_heads * hd).transpose(1, 0, 2).reshape(ms, swa_heads * hd)

    y_p, y_s = _attn_out_moe(att_p, att_s, w_swa_o[0], h1, h1, sample_blk_of_all, m_rows, tab, 1, g_ffn[1],
                             w_router[1], b_router[1], w_up, b_up, w_down, b_down,
                             seq=seq, nb_prompt=nbp, n_prompt_blocks=n_prompt_blocks, split_rows=mp)

    def sample_major(x):
        return x.reshape(t_new, nbs, -1).transpose(1, 0, 2)

    def last_window(x):
        return x[:mp].reshape(nbp, seq, kvh, hd)[:, -WINDOW:]

    y_prompt = y_p.reshape(nbp, seq, d)
    y_sample = sample_major(y_s)
    lat_p = ckv_p.reshape(1, nbp, seq, kv_lora)
    krope_p = kr_p.reshape(1, nbp, seq, rope)
    lat_s = sample_major(ckv_s)[None]
    krope_s = sample_major(kr_s)[None]
    k_n = sample_major(k_all[mp:]).reshape(nbs, t_new, kvh, hd)
    v_n = sample_major(v_all[mp:]).reshape(nbs, t_new, kvh, hd)
    win_k_s = jnp.concatenate([state_win_k, k_n], axis=1)[:, -WINDOW:]
    win_v_s = jnp.concatenate([state_win_v, v_n], axis=1)[:, -WINDOW:]
    return (y_prompt, y_sample, lat_p, krope_p, lat_s, krope_s,
            last_window(k_all), last_window(v_all), win_k_s, win_v_s)
```

```python
import functools

import jax
import jax.numpy as jnp
from jax import lax
from jax.experimental import pallas as pl
from jax.experimental.pallas import tpu as pltpu

F32 = jnp.float32
BF16 = jnp.bfloat16

RMS_EPS = 1e-6
NEG_INF = -1e30
ROPE_THETA = 10000.0
LOG2_E = 1.4426950408889634
PAGE = 128
WINDOW = 128
TOP_K = 4
SWIGLU_ALPHA = 1.702
SWIGLU_LIMIT = 7.0
LANES = 128
VMEM_LIMIT = 56 * 1024 * 1024

TM = 512
MOE_TM = 512
MOE_SUB = 256
FLASH_ROWS = 256
SAMPLE_PAGES = 16
SAMPLE_SUB = 512


def _nn(a, b):
    return jnp.dot(a, b, preferred_element_type=F32)


def _nt(a, b):
    return lax.dot_general(a, b, (((1,), (1,)), ((), ())), preferred_element_type=F32)


def _params(*sem):
    return pltpu.CompilerParams(dimension_semantics=sem, vmem_limit_bytes=VMEM_LIMIT)


def _norm_mod(h, g, shift, scale):
    tm, d = h.shape
    nb = shift.shape[0]
    y = h * lax.rsqrt(jnp.mean(h * h, axis=-1, keepdims=True) + RMS_EPS) * g
    y = y.reshape(tm // nb, nb, d) * (1.0 + scale[None]) + shift[None]
    return y.reshape(tm, d)


def _pack_bf16_pairs(x):
    k = x.shape[1] // 2
    r = x.astype(BF16).astype(F32)
    hi = lax.bitcast_convert_type(r[:, :k], jnp.uint32) & jnp.uint32(0xFFFF0000)
    lo = lax.bitcast_convert_type(r[:, k:], jnp.uint32) >> 16
    return hi | lo


def _unpack_bf16_pairs(w):
    return (lax.bitcast_convert_type(w & jnp.uint32(0xFFFF0000), F32),
            lax.bitcast_convert_type(w << 16, F32))


def _group64_rscale(x):
    lane = lax.broadcasted_iota(jnp.int32, x.shape, 1)
    x2 = x * x
    lo = jnp.sum(jnp.where(lane < 64, x2, 0.0), axis=-1, keepdims=True)
    hi = jnp.sum(jnp.where(lane >= 64, x2, 0.0), axis=-1, keepdims=True)
    return jnp.where(lane < 64, lax.rsqrt(lo / 64.0 + RMS_EPS), lax.rsqrt(hi / 64.0 + RMS_EPS))


def _adaln_kernel(c_ref, w_ref, b_ref, o_ref, *, nb_prompt):
    c = c_ref[...]
    a = (c * jax.nn.sigmoid(c)).astype(BF16)
    res = _nn(a, w_ref[...].astype(BF16)) + b_ref[...]
    nb = res.shape[0] - nb_prompt
    for b in range(nb_prompt):
        o_ref[b] = jnp.broadcast_to(res[b:b + 1], (nb, res.shape[1]))
    o_ref[nb_prompt] = res[nb_prompt:]


def _adaln_table(c, w, b, nb_prompt):
    n, d = c.shape
    nl, _, nout = w.shape
    nb = n - nb_prompt
    tn = 1024
    return pl.pallas_call(
        functools.partial(_adaln_kernel, nb_prompt=nb_prompt),
        grid=(nl, nout // tn),
        in_specs=[pl.BlockSpec((n, d), lambda l, j: (0, 0)),
                  pl.BlockSpec((None, d, tn), lambda l, j: (l, 0, j)),
                  pl.BlockSpec((None, 1, tn), lambda l, j: (l, 0, j))],
        out_specs=pl.BlockSpec((None, nb_prompt + 1, nb, tn), lambda l, j: (l, 0, 0, j)),
        out_shape=jax.ShapeDtypeStruct((nl, nb_prompt + 1, nb, nout), F32),
        compiler_params=_params("parallel", "parallel"),
        name="adaln",
    )(c, w, b.reshape(nl, 1, nout))


def _mla_down_kernel(hp_ref, hs_ref, g_ref, sh_ref, sc_ref, w_ref, gql_ref, gkvl_ref, gkr_ref, gkrp_ref,
                     cos_ref, sin_ref, cq_ref, ckvp_ref, ckvs_ref, krp_ref, krs_ref,
                     *, q_lora, kv_lora, rope, n_prompt_blocks):
    is_prompt = pl.program_id(0) < n_prompt_blocks
    h = jnp.where(is_prompt, hp_ref[...], hs_ref[...])
    u = _norm_mod(h, g_ref[...], sh_ref[...], sc_ref[...]).astype(BF16)
    a = _nn(u, w_ref[...])
    q = a[:, :q_lora]
    cq_ref[...] = (q * lax.rsqrt(jnp.mean(q * q, axis=-1, keepdims=True) + RMS_EPS)
                   * gql_ref[...]).astype(BF16)
    c = a[:, q_lora:q_lora + kv_lora]
    ckv = c * lax.rsqrt(jnp.mean(c * c, axis=-1, keepdims=True) + RMS_EPS) * gkvl_ref[...]
    o = q_lora + kv_lora
    raw = a[:, o:o + rope]
    rot = a[:, o + rope:o + 2 * rope]
    r = lax.rsqrt(jnp.mean(raw * raw, axis=-1, keepdims=True) + RMS_EPS)
    kr = r * (raw * gkr_ref[...] * cos_ref[...] + rot * gkrp_ref[...] * sin_ref[...])

    @pl.when(is_prompt)
    def _():
        ckvp_ref[...] = ckv
        krp_ref[...] = kr

    @pl.when(jnp.logical_not(is_prompt))
    def _():
        ckvs_ref[...] = ckv
        krs_ref[...] = kr


def _q_up_kernel(cq_ref, w_ref, gq_ref, tab_ref, o_ref, *, heads, nope, rope, scale):
    a = _nn(cq_ref[...], w_ref[...])
    tab = tab_ref[...]
    gq = gq_ref[...]
    for h in range(heads):
        x = a[:, h * LANES:(h + 1) * LANES]
        lane = lax.broadcasted_iota(jnp.int32, x.shape, 1)
        x2 = x * x
        ssn = jnp.sum(jnp.where(lane < nope, x2, 0.0), axis=-1, keepdims=True)
        ssr = jnp.sum(jnp.where((lane >= nope) & (lane < nope + rope), x2, 0.0), axis=-1, keepdims=True)
        r = jnp.where(lane < nope, lax.rsqrt(ssn / nope + RMS_EPS), lax.rsqrt(ssr / rope + RMS_EPS))
        o_ref[h] = (x * r * gq * tab * scale).astype(BF16)


def _kv_up_kernel(ckv_ref, kr_ref, wukt_ref, gkn_ref, wuv_ref, eye_ref, kt_ref, v_ref, *, heads, nope, rope):
    c = ckv_ref[...].astype(BF16)
    tm = c.shape[0]
    knt = _nt(wukt_ref[...], c).reshape(heads, nope, tm)
    ss = jnp.sum(knt * knt, axis=1, keepdims=True)
    kn = knt * lax.rsqrt(ss / nope + RMS_EPS) * gkn_ref[...][None]
    krt = _nt(eye_ref[...], kr_ref[...].astype(BF16)).astype(BF16)
    krt = jnp.broadcast_to(krt[None], (heads, rope, tm))
    kt_ref[0, :, 0, 0:nope, :] = kn.astype(BF16)
    kt_ref[0, :, 0, nope:nope + rope, :] = krt
    kt_ref[0, :, 0, nope + rope:nope + 2 * rope, :] = krt
    v_ref[...] = _nn(c, wuv_ref[...]).astype(BF16)


def _mla_flash_kernel(q_ref, kt_ref, v_ref, o_ref, *, tq, tk, vdim):
    qi = pl.program_id(2)
    n_full = (qi * tq) // tk
    parts = tq // FLASH_ROWS
    qs = [[q_ref[hh, r * FLASH_ROWS:(r + 1) * FLASH_ROWS, :] for r in range(parts)] for hh in range(2)]

    def step(j, carry, masked):
        start = pl.multiple_of(j * tk, tk)
        v = v_ref[pl.ds(start, tk), :]
        out = []
        for hh in range(2):
            kt = kt_ref[0, hh, j]
            for r in range(parts):
                m, l, acc = carry[hh * parts + r]
                nk = (r + 1) * FLASH_ROWS if masked else tk
                s = _nn(qs[hh][r], kt[:, :nk])
                if masked:
                    row = r * FLASH_ROWS + lax.broadcasted_iota(jnp.int32, s.shape, 0)
                    col = lax.broadcasted_iota(jnp.int32, s.shape, 1)
                    s = jnp.where(col <= row, s, NEG_INF)
                m_new = jnp.maximum(m, jnp.max(s, axis=-1, keepdims=True))
                alpha = jnp.exp2(m - m_new)
                p = jnp.exp2(s - m_new)
                l = alpha * l + jnp.sum(p, axis=-1, keepdims=True)
                acc = alpha * acc + _nn(p.astype(BF16), v[:nk])
                out.append((m_new, l, acc))
        return tuple(out)

    init = tuple((jnp.full((FLASH_ROWS, 1), NEG_INF, F32), jnp.zeros((FLASH_ROWS, 1), F32),
                  jnp.zeros((FLASH_ROWS, LANES), F32)) for _ in range(2 * parts))
    carry = lax.fori_loop(0, n_full, lambda j, c: step(j, c, False), init)
    carry = step(n_full, carry, True)
    lane = lax.broadcasted_iota(jnp.int32, (FLASH_ROWS, LANES), 1)
    for r in range(parts):
        (_, l0, acc0), (_, l1, acc1) = carry[r], carry[parts + r]
        o_ref[r * FLASH_ROWS:(r + 1) * FLASH_ROWS, :] = jnp.where(lane < vdim, acc0 / l0, acc1 / l1).astype(BF16)


def _q_absorb_kernel(q_ref, wuk_ref, gkn_ref, qa_ref, qr_ref, *, heads, nope, rope):
    for h in range(heads):
        q = q_ref[h].astype(F32)
        qg = (q[:, :nope] * gkn_ref[...]).astype(BF16)
        qa_ref[h] = _nn(qg, wuk_ref[h]).astype(BF16)
        qr_ref[h] = (q[:, nope:nope + rope] + q[:, nope + rope:nope + 2 * rope]).astype(BF16)


def _mla_sample_kernel(pt_ref, qa_ref, qr_ref, wukt_ref, cnew_ref, rnew_ref, poolc_ref, poolrt_ref,
                       o_ref, cbuf, rbuf, cb0, cb1, sc0, sc1, sems, *, heads, nope, n_chunks, n_batch, t_new):
    n = pl.program_id(0)
    cp = cbuf.shape[1]
    rows = qa_ref.shape[1]
    lat = qa_ref.shape[2]
    nw = heads * nope
    ppt = SAMPLE_SUB // PAGE
    sets = ((cb0, sc0), (cb1, sc1))

    def page_copies(page, slot, p):
        return (pltpu.make_async_copy(poolc_ref.at[page], cbuf.at[slot, p], sems.at[0, slot]),
                pltpu.make_async_copy(poolrt_ref.at[page], rbuf.at[slot, p], sems.at[1, slot]))

    def start_pages(b, ci, slot, p0, p1):
        for p in range(p0, p1):
            for cpy in page_copies(pt_ref[b, ci * cp + p], slot, p):
                cpy.start()

    def wait_chunk(slot):
        for p in range(cp):
            for cpy in page_copies(0, slot, p):
                cpy.wait()

    @pl.when(n == 0)
    def _():
        start_pages(0, 0, 0, 0, cp)

    lhs = jnp.concatenate([wukt_ref[...], qa_ref[0]], axis=0)
    qr = qr_ref[0]

    def scores(c, rope_scores):
        tk = c.shape[0]
        big = _nt(lhs, c)
        knt = big[:nw].reshape(heads, nope, tk)
        r = lax.rsqrt(jnp.sum(knt * knt, axis=1) / nope + RMS_EPS)
        return big[nw:] * jnp.concatenate([r] * t_new, axis=0) + rope_scores

    def update(s, c, carry):
        m, l, acc = carry
        m_new = jnp.maximum(m, jnp.max(s, axis=-1, keepdims=True))
        alpha = jnp.exp2(m - m_new)
        p = jnp.exp2(s - m_new)
        l = alpha * l + jnp.sum(p, axis=-1, keepdims=True)
        acc = alpha * acc + _nn(p.astype(BF16), c)
        return m_new, l, acc

    def absorb(slot, carry):
        cbs, scs = sets[slot]
        return update(scs[...], cbs[...], carry)

    def score_chunk(slot, nxt_b, nxt_ci, carry, absorb_other):
        cbs, scs = sets[slot]
        nt = cp // ppt

        def tiles(j0, j1):
            for j in range(j0, j1):
                c = cbuf[slot, j * ppt:(j + 1) * ppt].reshape(SAMPLE_SUB, lat).astype(BF16)
                krt = jnp.concatenate([rbuf[slot, j * ppt + p] for p in range(ppt)], axis=1).astype(BF16)
                cbs[j * SAMPLE_SUB:(j + 1) * SAMPLE_SUB, :] = c
                scs[:, j * SAMPLE_SUB:(j + 1) * SAMPLE_SUB] = scores(c, _nn(qr, krt))

        wait_chunk(slot)
        start_pages(nxt_b, nxt_ci, 1 - slot, 0, cp // 2)
        tiles(0, nt // 2)
        if absorb_other:
            carry = absorb(1 - slot, carry)
        start_pages(nxt_b, nxt_ci, 1 - slot, cp // 2, cp)
        tiles(nt // 2, nt)
        return carry

    def pair(k, carry):
        carry = score_chunk(1, n, 2 * k + 2, carry, True)
        return score_chunk(0, n, 2 * k + 3, carry, True)

    init = (jnp.full((rows, 1), NEG_INF, F32), jnp.zeros((rows, 1), F32), jnp.zeros((rows, lat), F32))
    carry = score_chunk(0, n, 1, init, False)
    carry = lax.fori_loop(0, (n_chunks - 2) // 2, pair, carry)
    carry = score_chunk(1, jnp.minimum(n + 1, n_batch - 1), 0, carry, True)
    carry = absorb(1, carry)

    @pl.when(n == n_batch - 1)
    def _():
        wait_chunk(0)

    c = cnew_ref[0].astype(BF16)
    tk = c.shape[0]
    t_of_row = lax.broadcasted_iota(jnp.int32, (rows, tk), 0) // heads
    col = lax.broadcasted_iota(jnp.int32, (rows, tk), 1)
    s = jnp.where(col <= t_of_row, scores(c, _nt(qr, rnew_ref[0].astype(BF16))), NEG_INF)
    m, l, acc = update(s, c, carry)
    o_ref[0] = acc / l


def _unabsorb_kernel(o_ref, wuv_ref, out_ref, *, heads):
    out_ref[...] = jnp.concatenate(
        [_nn(o_ref[h], wuv_ref[h]) for h in range(heads)], axis=-1).astype(BF16)


def _oproj_router_kernel(attp_ref, atts_ref, w_ref, hp_ref, hs_ref, agate_ref, g_ref, sh_ref, sc_ref,
                         whi_ref, wlo_ref, b_ref, h_ref, u_ref, idx_ref, gate_ref, *, n_prompt_blocks):
    is_prompt = pl.program_id(0) < n_prompt_blocks
    att = jnp.where(is_prompt, attp_ref[...], atts_ref[...])
    y = _nn(att, w_ref[...])
    tm, d = y.shape
    agate = agate_ref[...]
    nb = agate.shape[0]
    h = jnp.where(is_prompt, hp_ref[...], hs_ref[...]) + (y.reshape(tm // nb, nb, d) * agate[None]).reshape(tm, d)
    h_ref[...] = h
    u = _norm_mod(h, g_ref[...], sh_ref[...], sc_ref[...])
    uhi = u.astype(BF16)
    ulo = (u - uhi.astype(F32)).astype(BF16)
    u_ref[...] = u
    whi = whi_ref[...]
    logits = _nt(whi, uhi) + _nt(whi, ulo) + _nt(wlo_ref[...], uhi) + b_ref[...]
    ne = logits.shape[0]
    eid = lax.broadcasted_iota(jnp.int32, logits.shape, 0)
    work = logits
    vals, idxs = [], []
    for _ in range(TOP_K):
        m = jnp.max(work, axis=0, keepdims=True)
        idx = jnp.min(jnp.where(work == m, eid, ne), axis=0, keepdims=True)
        vals.append(m)
        idxs.append(idx)
        work = jnp.where(eid == idx, -jnp.inf, work)
    es = [jnp.exp(v - vals[0]) for v in vals]
    den = es[0] + es[1] + es[2] + es[3]
    idx_ref[...] = jnp.concatenate(idxs, axis=0)
    gate_ref[...] = jnp.concatenate([e / den for e in es], axis=0)


def _rank_kernel(idx_ref, rank_ref, cnt_ref, carry, *, ne):
    @pl.when(pl.program_id(0) == 0)
    def _():
        carry[...] = jnp.zeros(carry.shape, F32)

    idx = idx_ref[...]
    tb = idx.shape[1]
    eid = lax.broadcasted_iota(jnp.int32, (ne, tb), 0)
    hits = [eid == idx[k:k + 1, :] for k in range(TOP_K)]
    oh = jnp.where(hits[0], 1.0, 0.0)
    for k in range(1, TOP_K):
        oh = oh + jnp.where(hits[k], 1.0, 0.0)
    upper = jnp.where(lax.broadcasted_iota(jnp.int32, (tb, tb), 0) < lax.broadcasted_iota(jnp.int32, (tb, tb), 1),
                      1.0, 0.0).astype(BF16)
    before = _nn(oh.astype(BF16), upper) + carry[...]
    rank_ref[...] = jnp.concatenate(
        [jnp.sum(jnp.where(h, before, 0.0), axis=0, keepdims=True) for h in hits], axis=0).astype(jnp.int32)
    carry[...] = carry[...] + jnp.sum(oh, axis=1, keepdims=True)
    cnt_ref[...] = carry[...]


def _expert_kernel(be_ref, nrows_ref, nused_ref, blk_ref, x_ref, wup_ref, bup_ref, wdn_ref, bdn_ref, y_ref, wup_bf, wdn_bf,
                   *, ff):
    i = pl.program_id(0)
    n_here = nrows_ref[i]
    prev = be_ref[jnp.maximum(i - 1, 0)]
    new_expert = (i == 0) | (be_ref[i] != prev)

    @pl.when(new_expert & (n_here > 0))
    def _():
        wup_bf[...] = wup_ref[0].astype(BF16)
        wdn_bf[...] = wdn_ref[0].astype(BF16)

    for r0 in range(0, MOE_TM, MOE_SUB):
        @pl.when(n_here > r0)
        def _(r0=r0):
            x = x_ref[r0:r0 + MOE_SUB, :].astype(BF16)
            fc = 512
            acc = jnp.zeros((MOE_SUB, wdn_bf.shape[1]), F32)
            for c0 in range(0, ff, fc):
                glu = _nn(x, wup_bf[:, c0:c0 + fc]) + bup_ref[0, :, c0:c0 + fc]
                lin = _nn(x, wup_bf[:, ff + c0:ff + c0 + fc]) + bup_ref[0, :, ff + c0:ff + c0 + fc]
                glu = jnp.minimum(glu, SWIGLU_LIMIT)
                lin = jnp.clip(lin, -SWIGLU_LIMIT, SWIGLU_LIMIT)
                act = glu * jax.nn.sigmoid(SWIGLU_ALPHA * glu) * (lin + 1.0)
                acc = acc + _nn(act.astype(BF16), wdn_bf[c0:c0 + fc, :])
            y_ref[r0:r0 + MOE_SUB, :] = _pack_bf16_pairs(acc + bdn_ref[0])

        @pl.when(n_here <= r0)
        def _(r0=r0):
            y_ref[r0:r0 + MOE_SUB, :] = jnp.zeros((MOE_SUB, y_ref.shape[1]), jnp.uint32)


def _combine_value(h_ref, y_ref, gk_ref, gate_ref):
    gk = gk_ref[...]
    ya, yb = _unpack_bf16_pairs(y_ref[0])
    ya, yb = ya * gk[:, 0:1], yb * gk[:, 0:1]
    for k in range(1, TOP_K):
        a, b = _unpack_bf16_pairs(y_ref[k])
        ya, yb = ya + a * gk[:, k:k + 1], yb + b * gk[:, k:k + 1]
    y = jnp.concatenate([ya, yb], axis=1)
    tm, d = y.shape
    gate = gate_ref[...]
    nb = gate.shape[0]
    return h_ref[...] + (y.reshape(tm // nb, nb, d) * gate[None]).reshape(tm, d)


def _combine_split_kernel(h_ref, y_ref, gk_ref, gate_ref, op_ref, os_ref, *, n_prompt_blocks):
    res = _combine_value(h_ref, y_ref, gk_ref, gate_ref)

    @pl.when(pl.program_id(0) < n_prompt_blocks)
    def _():
        op_ref[...] = res

    @pl.when(pl.program_id(0) >= n_prompt_blocks)
    def _():
        os_ref[...] = res


def _combine_kv_q_kernel(h_ref, y_ref, gk_ref, gate_ref, gkv_ref, kvsh_ref, kvsc_ref, wkv_ref, gk64_ref,
                         ga_ref, ash_ref, asc_ref, wq_ref, gq_ref,
                         h_out, k_ref, v_ref, khm_ref, vhm_ref, q_ref, *, kvh, hd, heads, scale):
    res = _combine_value(h_ref, y_ref, gk_ref, gate_ref)
    h_out[...] = res
    _shared_kv(res, gkv_ref, kvsh_ref, kvsc_ref, wkv_ref, gk64_ref, k_ref, v_ref, khm_ref, vhm_ref, kvh=kvh, hd=hd)
    _swa_q(res, ga_ref, ash_ref, asc_ref, wq_ref, gq_ref, q_ref, heads=heads, hd=hd, scale=scale)


def _shared_kv(h, g_ref, sh_ref, sc_ref, w_ref, gk_ref, k_ref, v_ref, khm_ref, vhm_ref, *, kvh, hd):
    u = _norm_mod(h, g_ref[...], sh_ref[...], sc_ref[...]).astype(BF16)
    a = _nn(u, w_ref[...])
    kw = kvh * hd
    gk = gk_ref[...]
    ks = []
    for j in range(kw // LANES):
        x = a[:, j * LANES:(j + 1) * LANES]
        ks.append(x * _group64_rscale(x) * gk[:, j * LANES:(j + 1) * LANES])
    k = jnp.concatenate(ks, axis=-1)
    v = a[:, kw:2 * kw]
    k_ref[...] = k
    v_ref[...] = v
    for hh in range(kvh):
        khm_ref[hh] = k[:, hh * hd:(hh + 1) * hd].astype(BF16)
        vhm_ref[hh] = v[:, hh * hd:(hh + 1) * hd].astype(BF16)


def _swa_q(h, g_ref, sh_ref, sc_ref, w_ref, gq_ref, q_ref, *, heads, hd, scale):
    u = _norm_mod(h, g_ref[...], sh_ref[...], sc_ref[...]).astype(BF16)
    a = _nn(u, w_ref[...])
    gq = gq_ref[...]
    for j in range(heads * hd // LANES):
        x = a[:, j * LANES:(j + 1) * LANES]
        y = x * _group64_rscale(x) * gq[:, j * LANES:(j + 1) * LANES] * scale
        q_ref[2 * j] = y[:, :hd].astype(BF16)
        q_ref[2 * j + 1] = y[:, hd:].astype(BF16)


def _softmax_pv(s, v):
    m = jnp.max(s, axis=-1, keepdims=True)
    e = jnp.exp(s - m)
    return _nn(e.astype(BF16), v) / jnp.sum(e, axis=-1, keepdims=True)


def _swa_prompt_kernel(bias_ref, q_ref, kp_ref, kc_ref, vp_ref, vc_ref, o_ref, *, heads, kvh, tq):
    group = heads // kvh
    rows = group * tq
    not_first = lax.broadcasted_iota(jnp.int32, (WINDOW + tq, q_ref.shape[2]), 0) > 0
    outs = []
    for g in range(kvh):
        k = jnp.where(not_first, jnp.concatenate([kp_ref[g], kc_ref[g]], axis=0), 0)
        v = jnp.where(not_first, jnp.concatenate([vp_ref[g], vc_ref[g]], axis=0), 0)
        q = q_ref[g * group:(g + 1) * group].reshape(rows, q_ref.shape[2])
        o = _softmax_pv(_nt(q, k) + bias_ref[g], v)
        for j in range(group):
            outs.append(o[j * tq:(j + 1) * tq])
    o_ref[...] = jnp.concatenate(outs, axis=-1).astype(BF16)


def _swa_sample_kernel(bias_ref, q_ref, k_ref, v_ref, o_ref):
    bias = bias_ref[...]
    for b in range(q_ref.shape[0]):
        o_ref[b] = _softmax_pv(_nt(q_ref[b], k_ref[b]) + bias, v_ref[b]).astype(BF16)


def _rope_tables(pos, rope):
    half = rope // 2
    inv = ROPE_THETA ** (-jnp.arange(half, dtype=F32) / half)
    ang = pos.astype(F32)[:, None] * inv[None, :]
    cos, sin = jnp.cos(ang), jnp.sin(ang)
    return jnp.concatenate([cos, cos], axis=-1), jnp.concatenate([sin, sin], axis=-1)


def _rot_cols(w):
    half = w.shape[-1] // 2
    return jnp.concatenate([-w[..., half:], w[..., :half]], axis=-1)


def _swap_halves(g):
    half = g.shape[-1] // 2
    return jnp.concatenate([g[..., half:], g[..., :half]], axis=-1)


def _attn_out_moe(att_p, att_s, w_o, h_p, h_s, h_s_idx, m_rows, tab, layer, g_ffn, w_router, b_router,
                  w_up, b_up, w_down, b_down, *, seq, nb_prompt, n_prompt_blocks, split_rows=None,
                  next_layer=None):
    d = h_p.shape[1]
    ne = w_router.shape[1]
    ff = w_down.shape[2]
    nb = tab.shape[2]
    n_tok_blocks = m_rows // TM
    col0 = 3

    def tab_spec(col):
        return pl.BlockSpec((None, None, nb, d),
                            lambda i: (layer, jnp.minimum(i * TM // seq, nb_prompt), 0, col))

    def prompt_blk(i):
        return (jnp.minimum(i, n_prompt_blocks - 1), 0)

    def sample_blk(i):
        return (jnp.maximum(i - n_prompt_blocks, 0), 0)

    wr_t = w_router.T
    wr_hi = wr_t.astype(BF16)
    wr_lo = (wr_t - wr_hi.astype(F32)).astype(BF16)
    h, u, idx_t, gate_t = pl.pallas_call(
        functools.partial(_oproj_router_kernel, n_prompt_blocks=n_prompt_blocks),
        grid=(n_tok_blocks,),
        in_specs=[pl.BlockSpec((TM, w_o.shape[0]), prompt_blk), pl.BlockSpec((TM, w_o.shape[0]), sample_blk),
                  pl.BlockSpec(w_o.shape, lambda i: (0, 0)),
                  pl.BlockSpec((TM, d), prompt_blk), pl.BlockSpec((TM, d), h_s_idx), tab_spec(2),
                  pl.BlockSpec((1, d), lambda i: (0, 0)),
                  tab_spec(col0), tab_spec(col0 + 1),
                  pl.BlockSpec((ne, d), lambda i: (0, 0)),
                  pl.BlockSpec((ne, d), lambda i: (0, 0)),
                  pl.BlockSpec((ne, 1), lambda i: (0, 0))],
        out_specs=[pl.BlockSpec((TM, d), lambda i: (i, 0)),
                   pl.BlockSpec((TM, d), lambda i: (i, 0)),
                   pl.BlockSpec((TOP_K, TM), lambda i: (0, i)),
                   pl.BlockSpec((TOP_K, TM), lambda i: (0, i))],
        out_shape=[jax.ShapeDtypeStruct((m_rows, d), F32),
                   jax.ShapeDtypeStruct((m_rows, d), F32),
                   jax.ShapeDtypeStruct((TOP_K, m_rows), jnp.int32),
                   jax.ShapeDtypeStruct((TOP_K, m_rows), F32)],
        compiler_params=_params("parallel"),
        name="oproj_router",
    )(att_p, att_s, w_o.astype(BF16), h_p, h_s, tab, g_ffn.reshape(1, d), tab, tab, wr_hi, wr_lo,
      b_router.reshape(ne, 1))

    a = m_rows * TOP_K
    rank_t, cnt = pl.pallas_call(
        functools.partial(_rank_kernel, ne=ne),
        grid=(n_tok_blocks,),
        in_specs=[pl.BlockSpec((TOP_K, TM), lambda i: (0, i))],
        out_specs=[pl.BlockSpec((TOP_K, TM), lambda i: (0, i)), pl.BlockSpec((ne, 1), lambda i: (0, 0))],
        out_shape=[jax.ShapeDtypeStruct((TOP_K, m_rows), jnp.int32), jax.ShapeDtypeStruct((ne, 1), F32)],
        scratch_shapes=[pltpu.VMEM((ne, 1), F32)],
        compiler_params=_params("arbitrary"),
        name="moe_rank",
    )(idx_t)
    counts = cnt[:, 0].astype(jnp.int32)
    padded = (counts + MOE_TM - 1) // MOE_TM * MOE_TM
    pad_end = jnp.cumsum(padded)
    start = pad_end - padded
    experts = jnp.arange(ne, dtype=jnp.int32)
    pos_t = rank_t + jnp.sum(jnp.where(idx_t[..., None] == experts, start, 0), axis=-1)
    fbits = (a - 1).bit_length()
    assert ne << fbits < 2 ** 31
    keys = (idx_t.T.reshape(a) << fbits) | jnp.arange(a, dtype=jnp.int32)
    order = jnp.sort(keys) & ((1 << fbits) - 1)
    shift = start - (jnp.cumsum(counts) - counts)
    n_blk = -(-a // MOE_TM) + ne
    n_slots = n_blk * MOE_TM
    blk_start = jnp.arange(n_blk, dtype=jnp.int32) * MOE_TM
    blk_expert = jnp.minimum(jnp.sum(pad_end[None, :] <= blk_start[:, None], axis=1), ne - 1).astype(jnp.int32)
    n_used = (pad_end[-1] // MOE_TM).astype(jnp.int32).reshape(1)
    blk_rows = jnp.clip((pad_end - padded + counts)[blk_expert] - blk_start, 0, MOE_TM)
    blk_rows = jnp.where(blk_start < pad_end[-1], blk_rows, 0).astype(jnp.int32)
    slot_sorted = jnp.arange(n_slots, dtype=jnp.int32) - jnp.repeat(shift[blk_expert], MOE_TM)
    slot_tok = order[jnp.clip(slot_sorted, 0, a - 1)] // TOP_K
    xs = jnp.take(u, slot_tok, axis=0, mode="clip")
    steps = jnp.arange(n_blk, dtype=jnp.int32)
    blk_of_step = jnp.where(steps < n_used[0], (start // MOE_TM + pad_end // MOE_TM - 1)[blk_expert] - steps, steps)
    blk_of_step = blk_of_step.astype(jnp.int32)
    step_rows = blk_rows[blk_of_step]

    def live(i, nused):
        return jnp.minimum(i, nused[0] - 1)

    yb = pl.pallas_call(
        functools.partial(_expert_kernel, ff=ff),
        grid_spec=pltpu.PrefetchScalarGridSpec(
            num_scalar_prefetch=4,
            grid=(n_blk,),
            in_specs=[pl.BlockSpec((MOE_TM, d), lambda i, be, nr, nu, bs: (bs[live(i, nu)], 0)),
                      pl.BlockSpec((None, 1, d, 2 * ff), lambda i, be, nr, nu, bs: (layer, be[live(i, nu)], 0, 0)),
                      pl.BlockSpec((None, 1, 1, 2 * ff), lambda i, be, nr, nu, bs: (layer, be[live(i, nu)], 0, 0)),
                      pl.BlockSpec((None, 1, ff, d), lambda i, be, nr, nu, bs: (layer, be[live(i, nu)], 0, 0)),
                      pl.BlockSpec((None, 1, 1, d), lambda i, be, nr, nu, bs: (layer, be[live(i, nu)], 0, 0))],
            out_specs=pl.BlockSpec((MOE_TM, d // 2), lambda i, be, nr, nu, bs: (bs[i], 0)),
            scratch_shapes=[pltpu.VMEM((d, 2 * ff), BF16), pltpu.VMEM((ff, d), BF16)]),
        out_shape=jax.ShapeDtypeStruct((n_slots, d // 2), jnp.uint32),
        compiler_params=_params("arbitrary"),
        name="experts",
    )(blk_expert, step_rows, n_used, blk_of_step, xs, w_up, b_up.reshape(b_up.shape[0], ne, 1, 2 * ff), w_down,
      b_down.reshape(b_down.shape[0], ne, 1, d))

    ysel = jnp.take(yb, pos_t.reshape(a), axis=0, mode="clip").reshape(TOP_K, m_rows, d // 2)
    combine_specs = [pl.BlockSpec((TM, d), lambda i: (i, 0)),
                     pl.BlockSpec((TOP_K, TM, d // 2), lambda i: (0, i, 0)),
                     pl.BlockSpec((TM, TOP_K), lambda i: (i, 0)),
                     tab_spec(col0 + 2)]
    combine_args = (h, ysel, gate_t.T, tab)
    if split_rows is not None:
        return pl.pallas_call(
            functools.partial(_combine_split_kernel, n_prompt_blocks=n_prompt_blocks),
            grid=(n_tok_blocks,),
            in_specs=combine_specs,
            out_specs=[pl.BlockSpec((TM, d), prompt_blk), pl.BlockSpec((TM, d), sample_blk)],
            out_shape=[jax.ShapeDtypeStruct((split_rows, d), F32),
                       jax.ShapeDtypeStruct((m_rows - split_rows, d), F32)],
            compiler_params=_params("arbitrary"),
            name="moe_combine",
        )(*combine_args)
    kvh, hd, heads = next_layer["kvh"], next_layer["hd"], next_layer["heads"]
    kw = kvh * hd
    tab_kv = next_layer["tab_kv"]

    def next_tab_spec(which_layer, col):
        return pl.BlockSpec((None, None, nb, d),
                            lambda i: (which_layer, jnp.minimum(i * TM // seq, nb_prompt), 0, col))

    row = pl.BlockSpec((TM, d), lambda i: (i, 0))
    return pl.pallas_call(
        functools.partial(_combine_kv_q_kernel, kvh=kvh, hd=hd, heads=heads, scale=hd ** -0.5),
        grid=(n_tok_blocks,),
        in_specs=combine_specs + [
            pl.BlockSpec((1, d), lambda i: (0, 0)), next_tab_spec(0, 0), next_tab_spec(0, 1),
            pl.BlockSpec((d, 2 * kw), lambda i: (0, 0)), pl.BlockSpec((1, kw), lambda i: (0, 0)),
            pl.BlockSpec((1, d), lambda i: (0, 0)), next_tab_spec(layer + 1, 0), next_tab_spec(layer + 1, 1),
            pl.BlockSpec((d, heads * hd), lambda i: (0, 0)), pl.BlockSpec((1, heads * hd), lambda i: (0, 0))],
        out_specs=[row, pl.BlockSpec((TM, kw), lambda i: (i, 0)), pl.BlockSpec((TM, kw), lambda i: (i, 0)),
                   pl.BlockSpec((kvh, TM, hd), lambda i: (0, i, 0)), pl.BlockSpec((kvh, TM, hd), lambda i: (0, i, 0)),
                   pl.BlockSpec((heads, TM, hd), lambda i: (0, i, 0))],
        out_shape=[jax.ShapeDtypeStruct((m_rows, d), F32),
                   jax.ShapeDtypeStruct((m_rows, kw), F32), jax.ShapeDtypeStruct((m_rows, kw), F32),
                   jax.ShapeDtypeStruct((kvh, m_rows, hd), BF16), jax.ShapeDtypeStruct((kvh, m_rows, hd), BF16),
                   jax.ShapeDtypeStruct((heads, m_rows, hd), BF16)],
        compiler_params=_params("parallel"),
        name="moe_combine_kv_q",
    )(*combine_args, next_layer["g_kv_norm"].reshape(1, d), tab_kv, tab_kv, next_layer["w_kv"].astype(BF16),
      jnp.tile(next_layer["g_k"], kvh).reshape(1, kw),
      next_layer["g_attn"].reshape(1, d), tab, tab, next_layer["w_q"].astype(BF16),
      jnp.tile(next_layer["g_q"], heads).reshape(1, heads * hd))


def kernel(x_prompt, x_sample, c_prompt, c_sample, cache_mla_latent, cache_mla_krope, state_win_k, state_win_v, page_table, w_mod, b_mod, g_attn, g_ffn, w_mla_down, g_mla_q_lora, g_mla_kv_lora, w_mla_uq, w_mla_uk, w_mla_uv, g_mla_qn, g_mla_qr, g_mla_kn, g_mla_kr, w_mla_o, w_kvmod, b_kvmod, g_kv_norm, w_kv, g_swa_k, w_swa_q, g_swa_q, swa_sinks, w_swa_o, w_router, b_router, w_up, b_up, w_down, b_down):
    nbp, seq, d = x_prompt.shape
    nbs, t_new, _ = x_sample.shape
    q_lora = g_mla_q_lora.shape[1]
    kv_lora = g_mla_kv_lora.shape[1]
    heads, nope = w_mla_uk.shape[2], w_mla_uk.shape[3]
    rope = g_mla_qr.shape[1]
    vdim = w_mla_uv.shape[3]
    n_pages = page_table.shape[1]
    past = n_pages * PAGE
    swa_heads = swa_sinks.shape[1]
    hd = g_swa_k.shape[0]
    kvh = w_kv.shape[1] // (2 * hd)
    assert nope + 2 * rope == LANES and 2 * vdim == LANES and 2 * hd == LANES and TM % FLASH_ROWS == 0
    assert seq % TM == 0 and TM % nbs == 0 and (nbs * t_new) % TM == 0 and nbs % 8 == 0
    assert w_mod.shape[0] == 2 and n_pages % (2 * SAMPLE_PAGES) == 0

    mp = nbp * seq
    ms = nbs * t_new
    m_rows = mp + ms
    n_tok_blocks = m_rows // TM
    n_prompt_blocks = mp // TM
    blocks_per_seq = seq // TM

    def tab_idx(i):
        return jnp.minimum(i * TM // seq, nbp)

    def tok(i):
        return (i, 0)

    def const2(i):
        return (0, 0)

    h0_p = x_prompt.reshape(mp, d)
    h0_s = x_sample.transpose(1, 0, 2).reshape(ms, d)

    def prompt_blk(i):
        return (jnp.minimum(i, n_prompt_blocks - 1), 0)

    def sample_blk(i):
        return (jnp.maximum(i - n_prompt_blocks, 0), 0)

    c_all = jnp.concatenate([c_prompt, c_sample], axis=0)
    tab = _adaln_table(c_all, w_mod, b_mod, nbp)
    tab_kv = _adaln_table(c_all, w_kvmod[None], b_kvmod[None], nbp)

    def tab_spec(layer, col):
        return pl.BlockSpec((None, None, nbs, d), lambda i: (layer, tab_idx(i), 0, col))

    cos_p, sin_p = _rope_tables(jnp.arange(seq), rope)
    cos_s, sin_s = _rope_tables(past + jnp.arange(t_new), rope)
    cos_tab = jnp.concatenate([cos_p, jnp.repeat(cos_s, nbs, axis=0)], axis=0)
    sin_tab = jnp.concatenate([sin_p, jnp.repeat(sin_s, nbs, axis=0)], axis=0)

    def pos_blk(i):
        return (jnp.where(i < n_prompt_blocks, i % blocks_per_seq, blocks_per_seq + i - n_prompt_blocks), 0)

    wd = w_mla_down[0]
    w_down_ext = jnp.concatenate([wd, _rot_cols(wd[:, q_lora + kv_lora:])], axis=1).astype(BF16)
    nd = w_down_ext.shape[1]
    cq, ckv_p, ckv_s, kr_p, kr_s = pl.pallas_call(
        functools.partial(_mla_down_kernel, q_lora=q_lora, kv_lora=kv_lora, rope=rope,
                          n_prompt_blocks=n_prompt_blocks),
        grid=(n_tok_blocks,),
        in_specs=[pl.BlockSpec((TM, d), prompt_blk), pl.BlockSpec((TM, d), sample_blk),
                  pl.BlockSpec((1, d), const2), tab_spec(0, 0), tab_spec(0, 1),
                  pl.BlockSpec((d, nd), const2), pl.BlockSpec((1, q_lora), const2),
                  pl.BlockSpec((1, kv_lora), const2), pl.BlockSpec((1, rope), const2),
                  pl.BlockSpec((1, rope), const2), pl.BlockSpec((TM, rope), pos_blk),
                  pl.BlockSpec((TM, rope), pos_blk)],
        out_specs=[pl.BlockSpec((TM, q_lora), tok),
                   pl.BlockSpec((TM, kv_lora), prompt_blk), pl.BlockSpec((TM, kv_lora), sample_blk),
                   pl.BlockSpec((TM, rope), prompt_blk), pl.BlockSpec((TM, rope), sample_blk)],
        out_shape=[jax.ShapeDtypeStruct((m_rows, q_lora), BF16),
                   jax.ShapeDtypeStruct((mp, kv_lora), F32), jax.ShapeDtypeStruct((ms, kv_lora), F32),
                   jax.ShapeDtypeStruct((mp, rope), F32), jax.ShapeDtypeStruct((ms, rope), F32)],
        compiler_params=_params("arbitrary"),
        name="mla_down",
    )(h0_p, h0_s, g_attn[0].reshape(1, d), tab, tab, w_down_ext, g_mla_q_lora, g_mla_kv_lora,
      g_mla_kr, _swap_halves(g_mla_kr), cos_tab, sin_tab)

    wq = w_mla_uq[0]
    w_q_cat = jnp.concatenate([wq, _rot_cols(wq[..., nope:])], axis=-1).reshape(q_lora, heads * LANES).astype(BF16)
    gq_cat = jnp.concatenate([g_mla_qn[0], g_mla_qr[0], _swap_halves(g_mla_qr[0])]).reshape(1, LANES)
    q_tab = jnp.concatenate([jnp.ones((cos_tab.shape[0], nope), F32), cos_tab, sin_tab], axis=1)
    mla_scale = (nope + rope) ** -0.5 * LOG2_E
    q_cat = pl.pallas_call(
        functools.partial(_q_up_kernel, heads=heads, nope=nope, rope=rope, scale=mla_scale),
        grid=(n_tok_blocks,),
        in_specs=[pl.BlockSpec((TM, q_lora), tok), pl.BlockSpec((q_lora, heads * LANES), const2),
                  pl.BlockSpec((1, LANES), const2), pl.BlockSpec((TM, LANES), pos_blk)],
        out_specs=pl.BlockSpec((heads, TM, LANES), lambda i: (0, i, 0)),
        out_shape=jax.ShapeDtypeStruct((heads, m_rows, LANES), BF16),
        compiler_params=_params("parallel"),
        name="q_up",
    )(cq, w_q_cat, gq_cat, q_tab)

    wuk = w_mla_uk[0]
    wuv = w_mla_uv[0]
    wuk_t = wuk.reshape(kv_lora, heads * nope).T.astype(BF16)
    tk = TM
    nk = seq // tk
    kt, v_p = pl.pallas_call(
        functools.partial(_kv_up_kernel, heads=heads, nope=nope, rope=rope),
        grid=(n_prompt_blocks,),
        in_specs=[pl.BlockSpec((TM, kv_lora), tok), pl.BlockSpec((TM, rope), tok),
                  pl.BlockSpec((heads * nope, kv_lora), const2), pl.BlockSpec((nope, 1), const2),
                  pl.BlockSpec((kv_lora, heads * vdim), const2), pl.BlockSpec((rope, rope), const2)],
        out_specs=[pl.BlockSpec((1, heads, 1, LANES, tk), lambda i: (i // nk, 0, i % nk, 0, 0)),
                   pl.BlockSpec((TM, heads * vdim), tok)],
        out_shape=[jax.ShapeDtypeStruct((nbp, heads, nk, LANES, tk), BF16),
                   jax.ShapeDtypeStruct((mp, heads * vdim), BF16)],
        compiler_params=_params("parallel"),
        name="kv_up",
    )(ckv_p, kr_p, wuk_t, g_mla_kn[0].reshape(nope, 1), wuv.reshape(kv_lora, heads * vdim).astype(BF16),
      jnp.eye(rope, dtype=BF16))

    tq = tk
    nq = seq // tq
    att_p = pl.pallas_call(
        functools.partial(_mla_flash_kernel, tq=tq, tk=tk, vdim=vdim),
        grid=(nbp, heads // 2, nq),
        in_specs=[pl.BlockSpec((2, tq, LANES), lambda b, hp, qi: (hp, b * nq + qi, 0)),
                  pl.BlockSpec((1, 2, nk, LANES, tk), lambda b, hp, qi: (b, hp, 0, 0, 0)),
                  pl.BlockSpec((seq, LANES), lambda b, hp, qi: (b, hp))],
        out_specs=pl.BlockSpec((tq, LANES), lambda b, hp, qi: (b * nq + qi, hp)),
        out_shape=jax.ShapeDtypeStruct((mp, heads * vdim), BF16),
        compiler_params=_params("parallel", "parallel", "arbitrary"),
        name="mla_prompt_attention",
    )(q_cat, kt, v_p)

    first_sample_blk = mp // ms
    qa, qr = pl.pallas_call(
        functools.partial(_q_absorb_kernel, heads=heads, nope=nope, rope=rope),
        grid=(1,),
        in_specs=[pl.BlockSpec((heads, ms, LANES), lambda i: (0, first_sample_blk, 0)),
                  pl.BlockSpec((heads, nope, kv_lora), lambda i: (0, 0, 0)),
                  pl.BlockSpec((1, nope), const2)],
        out_specs=[pl.BlockSpec((heads, ms, kv_lora), lambda i: (0, 0, 0)),
                   pl.BlockSpec((heads, ms, rope), lambda i: (0, 0, 0))],
        out_shape=[jax.ShapeDtypeStruct((heads, ms, kv_lora), BF16),
                   jax.ShapeDtypeStruct((heads, ms, rope), BF16)],
        compiler_params=_params("arbitrary"),
        name="q_absorb",
    )(q_cat, wuk.transpose(1, 2, 0).astype(BF16), g_mla_kn)
    rows = t_new * heads

    def per_batch(x):
        return x.reshape(heads, t_new, nbs, -1).transpose(2, 1, 0, 3).reshape(nbs, rows, -1)

    def new_rows(x):
        x = x.reshape(t_new, nbs, -1).transpose(1, 0, 2)
        return jnp.pad(x, ((0, 0), (0, 8 - t_new), (0, 0)))

    n_chunks = n_pages // SAMPLE_PAGES
    sample_tokens = SAMPLE_PAGES * PAGE
    o_lat = pl.pallas_call(
        functools.partial(_mla_sample_kernel, heads=heads, nope=nope, n_chunks=n_chunks, n_batch=nbs,
                          t_new=t_new),
        grid_spec=pltpu.PrefetchScalarGridSpec(
            num_scalar_prefetch=1,
            grid=(nbs,),
            in_specs=[pl.BlockSpec((1, rows, kv_lora), lambda n, pt: (n, 0, 0)),
                      pl.BlockSpec((1, rows, rope), lambda n, pt: (n, 0, 0)),
                      pl.BlockSpec((heads * nope, kv_lora), lambda n, pt: (0, 0)),
                      pl.BlockSpec((1, 8, kv_lora), lambda n, pt: (n, 0, 0)),
                      pl.BlockSpec((1, 8, rope), lambda n, pt: (n, 0, 0)),
                      pl.BlockSpec(memory_space=pl.ANY),
                      pl.BlockSpec(memory_space=pl.ANY)],
            out_specs=pl.BlockSpec((1, rows, kv_lora), lambda n, pt: (n, 0, 0)),
            scratch_shapes=[pltpu.VMEM((2, SAMPLE_PAGES, PAGE, kv_lora), F32),
                            pltpu.VMEM((2, SAMPLE_PAGES, rope, PAGE), F32),
                            pltpu.VMEM((sample_tokens, kv_lora), BF16),
                            pltpu.VMEM((sample_tokens, kv_lora), BF16),
                            pltpu.VMEM((rows, sample_tokens), F32),
                            pltpu.VMEM((rows, sample_tokens), F32),
                            pltpu.SemaphoreType.DMA((2, 2))]),
        out_shape=jax.ShapeDtypeStruct((nbs, rows, kv_lora), F32),
        compiler_params=_params("arbitrary"),
        name="mla_sample_attention",
    )(page_table, per_batch(qa), per_batch(qr), wuk_t, new_rows(ckv_s), new_rows(kr_s),
      cache_mla_latent[0], cache_mla_krope[0].transpose(0, 2, 1))
    o_lat = o_lat.reshape(nbs, t_new, heads, kv_lora).transpose(2, 1, 0, 3).reshape(heads, ms, kv_lora)
    att_s = pl.pallas_call(
        functools.partial(_unabsorb_kernel, heads=heads),
        grid=(1,),
        in_specs=[pl.BlockSpec((heads, ms, kv_lora), lambda i: (0, 0, 0)),
                  pl.BlockSpec((heads, kv_lora, vdim), lambda i: (0, 0, 0))],
        out_specs=pl.BlockSpec((ms, heads * vdim), const2),
        out_shape=jax.ShapeDtypeStruct((ms, heads * vdim), BF16),
        compiler_params=_params("arbitrary"),
        name="unabsorb",
    )(o_lat.astype(BF16), wuv.transpose(1, 0, 2).astype(BF16))

    def sample_blk_of_all(i):
        return (n_prompt_blocks + jnp.maximum(i - n_prompt_blocks, 0), 0)

    h1, k_all, v_all, k_hm, v_hm, q_hm = _attn_out_moe(
        att_p, att_s, w_mla_o[0], h0_p, h0_s, sample_blk, m_rows, tab, 0, g_ffn[0],
        w_router[0], b_router[0], w_up, b_up, w_down, b_down,
        seq=seq, nb_prompt=nbp, n_prompt_blocks=n_prompt_blocks,
        next_layer=dict(kvh=kvh, hd=hd, heads=swa_heads, tab_kv=tab_kv, g_kv_norm=g_kv_norm, w_kv=w_kv,
                        g_k=g_swa_k, g_attn=g_attn[1], w_q=w_swa_q[0], g_q=g_swa_q[0]))


    group = swa_heads // kvh
    slopes = 2.0 ** (-8.0 * jnp.arange(1, swa_heads + 1, dtype=F32) / swa_heads)
    sinks = swa_sinks[0].astype(F32)

    def bias_table(dist, valid, sink_col, head_of_row):
        b = jnp.where(valid, -slopes[head_of_row][:, None] * dist.astype(F32), NEG_INF)
        return jnp.where(sink_col, sinks[head_of_row][:, None], b)

    tq = WINDOW
    wblocks = seq // tq
    r = jnp.arange(group * tq)
    col = jnp.arange(WINDOW + tq)
    dist = (r % tq)[:, None] + WINDOW - col[None, :]
    inside = (dist >= 0) & (dist < WINDOW)
    bias_p = jnp.stack([
        jnp.stack([bias_table(dist, inside & ok[None, :], (col == 0)[None, :], g * group + r // tq)
                   for g in range(kvh)])
        for ok in (col >= 0, col >= WINDOW)])
    att_p = pl.pallas_call(
        functools.partial(_swa_prompt_kernel, heads=swa_heads, kvh=kvh, tq=tq),
        grid=(nbp, wblocks),
        in_specs=[pl.BlockSpec((None, kvh, group * tq, WINDOW + tq), lambda b, i: (jnp.where(i == 0, 1, 0), 0, 0, 0)),
                  pl.BlockSpec((swa_heads, tq, hd), lambda b, i: (0, b * wblocks + i, 0)),
                  pl.BlockSpec((kvh, WINDOW, hd), lambda b, i: (0, jnp.maximum(b * wblocks + i - 1, 0), 0)),
                  pl.BlockSpec((kvh, tq, hd), lambda b, i: (0, b * wblocks + i, 0)),
                  pl.BlockSpec((kvh, WINDOW, hd), lambda b, i: (0, jnp.maximum(b * wblocks + i - 1, 0), 0)),
                  pl.BlockSpec((kvh, tq, hd), lambda b, i: (0, b * wblocks + i, 0))],
        out_specs=pl.BlockSpec((tq, swa_heads * hd), lambda b, i: (b * wblocks + i, 0)),
        out_shape=jax.ShapeDtypeStruct((mp, swa_heads * hd), BF16),
        compiler_params=_params("parallel", "arbitrary"),
        name="swa_prompt_attention",
    )(bias_p, q_hm, k_hm, k_hm, v_hm, v_hm)

    tpad = 8
    nkg = WINDOW + tpad

    def stack_keys(win, new_hm):
        new = new_hm[:, mp:].reshape(kvh, t_new, nbs, hd).transpose(2, 0, 1, 3)
        new = jnp.pad(new, ((0, 0), (0, 0), (0, tpad - t_new), (0, 0)))
        return jnp.concatenate([win.transpose(0, 2, 1, 3).astype(BF16), new], axis=2).reshape(nbs, kvh * nkg, hd)

    rs = jnp.arange(t_new * swa_heads)
    t_of, h_of = rs // swa_heads, rs % swa_heads
    cs = jnp.arange(kvh * nkg)
    j = cs % nkg
    dist_s = jnp.where(j < WINDOW, WINDOW + t_of[:, None] - j[None, :], t_of[:, None] - (j[None, :] - WINDOW))
    own = (cs // nkg)[None, :] == (h_of // group)[:, None]
    ok_s = own & (dist_s >= 0) & (dist_s < WINDOW) & ((j < WINDOW + t_new)[None, :])
    bias_s = bias_table(dist_s, ok_s, own & ((j == WINDOW + t_new)[None, :]), h_of)
    q_s = q_hm[:, mp:].reshape(swa_heads, t_new, nbs, hd).transpose(2, 1, 0, 3).reshape(nbs, t_new * swa_heads, hd)
    nbi = 8
    att_s = pl.pallas_call(
        _swa_sample_kernel,
        grid=(nbs // nbi,),
        in_specs=[pl.BlockSpec(bias_s.shape, const2),
                  pl.BlockSpec((nbi, t_new * swa_heads, hd), lambda i: (i, 0, 0)),
                  pl.BlockSpec((nbi, kvh * nkg, hd), lambda i: (i, 0, 0)),
                  pl.BlockSpec((nbi, kvh * nkg, hd), lambda i: (i, 0, 0))],
        out_specs=pl.BlockSpec((nbi, t_new * swa_heads, hd), lambda i: (i, 0, 0)),
        out_shape=jax.ShapeDtypeStruct((nbs, t_new * swa_heads, hd), BF16),
        compiler_params=_params("parallel"),
        name="swa_sample_attention",
    )(bias_s, q_s, stack_keys(state_win_k, k_hm), stack_keys(state_win_v, v_hm))
    att_s = att_s.reshape(nbs, t_new, swa_heads * hd).transpose(1, 0, 2).reshape(ms, swa_heads * hd)

    y_p, y_s = _attn_out_moe(att_p, att_s, w_swa_o[0], h1, h1, sample_blk_of_all, m_rows, tab, 1, g_ffn[1],
                             w_router[1], b_router[1], w_up, b_up, w_down, b_down,
                             seq=seq, nb_prompt=nbp, n_prompt_blocks=n_prompt_blocks, split_rows=mp)

    def sample_major(x):
        return x.reshape(t_new, nbs, -1).transpose(1, 0, 2)

    def last_window(x):
        return x[:mp].reshape(nbp, seq, kvh, hd)[:, -WINDOW:]

    y_prompt = y_p.reshape(nbp, seq, d)
    y_sample = sample_major(y_s)
    lat_p = ckv_p.reshape(1, nbp, seq, kv_lora)
    krope_p = kr_p.reshape(1, nbp, seq, rope)
    lat_s = sample_major(ckv_s)[None]
    krope_s = sample_major(kr_s)[None]
    k_n = sample_major(k_all[mp:]).reshape(nbs, t_new, kvh, hd)
    v_n = sample_major(v_all[mp:]).reshape(nbs, t_new, kvh, hd)
    win_k_s = jnp.concatenate([state_win_k, k_n], axis=1)[:, -WINDOW:]
    win_v_s = jnp.concatenate([state_win_v, v_n], axis=1)[:, -WINDOW:]
    return (y_prompt, y_sample, lat_p, krope_p, lat_s, krope_s,
            last_window(k_all), last_window(v_all), win_k_s, win_v_s)
```

```python
import functools

import jax
import jax.numpy as jnp
from jax import lax
from jax.experimental import pallas as pl
from jax.experimental.pallas import tpu as pltpu

F32 = jnp.float32
BF16 = jnp.bfloat16

RMS_EPS = 1e-6
NEG_INF = -1e30
ROPE_THETA = 10000.0
LOG2_E = 1.4426950408889634
PAGE = 128
WINDOW = 128
TOP_K = 4
SWIGLU_ALPHA = 1.702
SWIGLU_LIMIT = 7.0
LANES = 128
VMEM_LIMIT = 56 * 1024 * 1024

TM = 512
MOE_TM = 512
MOE_SUB = 256
FLASH_ROWS = 256
SAMPLE_PAGES = 16
SAMPLE_SUB = 512


def _nn(a, b):
    return jnp.dot(a, b, preferred_element_type=F32)


def _nt(a, b):
    return lax.dot_general(a, b, (((1,), (1,)), ((), ())), preferred_element_type=F32)


def _params(*sem):
    return pltpu.CompilerParams(dimension_semantics=sem, vmem_limit_bytes=VMEM_LIMIT)


def _norm_mod(h, g, shift, scale):
    tm, d = h.shape
    nb = shift.shape[0]
    y = h * lax.rsqrt(jnp.mean(h * h, axis=-1, keepdims=True) + RMS_EPS) * g
    y = y.reshape(tm // nb, nb, d) * (1.0 + scale[None]) + shift[None]
    return y.reshape(tm, d)


def _pack_bf16_pairs(x):
    k = x.shape[1] // 2
    r = x.astype(BF16).astype(F32)
    hi = lax.bitcast_convert_type(r[:, :k], jnp.uint32) & jnp.uint32(0xFFFF0000)
    lo = lax.bitcast_convert_type(r[:, k:], jnp.uint32) >> 16
    return hi | lo


def _unpack_bf16_pairs(w):
    return (lax.bitcast_convert_type(w & jnp.uint32(0xFFFF0000), F32),
            lax.bitcast_convert_type(w << 16, F32))


def _group64_rscale(x):
    lane = lax.broadcasted_iota(jnp.int32, x.shape, 1)
    x2 = x * x
    lo = jnp.sum(jnp.where(lane < 64, x2, 0.0), axis=-1, keepdims=True)
    hi = jnp.sum(jnp.where(lane >= 64, x2, 0.0), axis=-1, keepdims=True)
    return jnp.where(lane < 64, lax.rsqrt(lo / 64.0 + RMS_EPS), lax.rsqrt(hi / 64.0 + RMS_EPS))


def _adaln_kernel(c_ref, w_ref, b_ref, o_ref, *, nb_prompt):
    c = c_ref[...]
    a = (c * jax.nn.sigmoid(c)).astype(BF16)
    res = _nn(a, w_ref[...].astype(BF16)) + b_ref[...]
    nb = res.shape[0] - nb_prompt
    for b in range(nb_prompt):
        o_ref[b] = jnp.broadcast_to(res[b:b + 1], (nb, res.shape[1]))
    o_ref[nb_prompt] = res[nb_prompt:]


def _adaln_table(c, w, b, nb_prompt):
    n, d = c.shape
    nl, _, nout = w.shape
    nb = n - nb_prompt
    tn = 1024
    return pl.pallas_call(
        functools.partial(_adaln_kernel, nb_prompt=nb_prompt),
        grid=(nl, nout // tn),
        in_specs=[pl.BlockSpec((n, d), lambda l, j: (0, 0)),
                  pl.BlockSpec((None, d, tn), lambda l, j: (l, 0, j)),
                  pl.BlockSpec((None, 1, tn), lambda l, j: (l, 0, j))],
        out_specs=pl.BlockSpec((None, nb_prompt + 1, nb, tn), lambda l, j: (l, 0, 0, j)),
        out_shape=jax.ShapeDtypeStruct((nl, nb_prompt + 1, nb, nout), F32),
        compiler_params=_params("parallel", "parallel"),
        name="adaln",
    )(c, w, b.reshape(nl, 1, nout))


def _mla_down_kernel(hp_ref, hs_ref, g_ref, sh_ref, sc_ref, w_ref, gql_ref, gkvl_ref, gkr_ref, gkrp_ref,
                     cos_ref, sin_ref, cq_ref, ckvp_ref, ckvs_ref, krp_ref, krs_ref,
                     *, q_lora, kv_lora, rope, n_prompt_blocks):
    is_prompt = pl.program_id(0) < n_prompt_blocks
    h = jnp.where(is_prompt, hp_ref[...], hs_ref[...])
    u = _norm_mod(h, g_ref[...], sh_ref[...], sc_ref[...]).astype(BF16)
    a = _nn(u, w_ref[...])
    q = a[:, :q_lora]
    cq_ref[...] = (q * lax.rsqrt(jnp.mean(q * q, axis=-1, keepdims=True) + RMS_EPS)
                   * gql_ref[...]).astype(BF16)
    c = a[:, q_lora:q_lora + kv_lora]
    ckv = c * lax.rsqrt(jnp.mean(c * c, axis=-1, keepdims=True) + RMS_EPS) * gkvl_ref[...]
    o = q_lora + kv_lora
    raw = a[:, o:o + rope]
    rot = a[:, o + rope:o + 2 * rope]
    r = lax.rsqrt(jnp.mean(raw * raw, axis=-1, keepdims=True) + RMS_EPS)
    kr = r * (raw * gkr_ref[...] * cos_ref[...] + rot * gkrp_ref[...] * sin_ref[...])

    @pl.when(is_prompt)
    def _():
        ckvp_ref[...] = ckv
        krp_ref[...] = kr

    @pl.when(jnp.logical_not(is_prompt))
    def _():
        ckvs_ref[...] = ckv
        krs_ref[...] = kr


def _q_up_kernel(cq_ref, w_ref, gq_ref, tab_ref, o_ref, *, heads, nope, rope, scale):
    a = _nn(cq_ref[...], w_ref[...])
    tab = tab_ref[...]
    gq = gq_ref[...]
    for h in range(heads):
        x = a[:, h * LANES:(h + 1) * LANES]
        lane = lax.broadcasted_iota(jnp.int32, x.shape, 1)
        x2 = x * x
        ssn = jnp.sum(jnp.where(lane < nope, x2, 0.0), axis=-1, keepdims=True)
        ssr = jnp.sum(jnp.where((lane >= nope) & (lane < nope + rope), x2, 0.0), axis=-1, keepdims=True)
        r = jnp.where(lane < nope, lax.rsqrt(ssn / nope + RMS_EPS), lax.rsqrt(ssr / rope + RMS_EPS))
        o_ref[h] = (x * r * gq * tab * scale).astype(BF16)


def _kv_up_kernel(ckv_ref, kr_ref, wukt_ref, gkn_ref, wuv_ref, eye_ref, kt_ref, v_ref, *, heads, nope, rope):
    c = ckv_ref[...].astype(BF16)
    tm = c.shape[0]
    knt = _nt(wukt_ref[...], c).reshape(heads, nope, tm)
    ss = jnp.sum(knt * knt, axis=1, keepdims=True)
    kn = knt * lax.rsqrt(ss / nope + RMS_EPS) * gkn_ref[...][None]
    krt = _nt(eye_ref[...], kr_ref[...].astype(BF16)).astype(BF16)
    krt = jnp.broadcast_to(krt[None], (heads, rope, tm))
    kt_ref[0, :, 0, 0:nope, :] = kn.astype(BF16)
    kt_ref[0, :, 0, nope:nope + rope, :] = krt
    kt_ref[0, :, 0, nope + rope:nope + 2 * rope, :] = krt
    v_ref[...] = _nn(c, wuv_ref[...]).astype(BF16)


def _mla_flash_kernel(q_ref, kt_ref, v_ref, o_ref, *, tq, tk, vdim):
    qi = pl.program_id(2)
    n_full = (qi * tq) // tk
    parts = tq // FLASH_ROWS
    qs = [[q_ref[hh, r * FLASH_ROWS:(r + 1) * FLASH_ROWS, :] for r in range(parts)] for hh in range(2)]

    def step(j, carry, masked):
        start = pl.multiple_of(j * tk, tk)
        v = v_ref[pl.ds(start, tk), :]
        out = []
        for hh in range(2):
            kt = kt_ref[0, hh, j]
            for r in range(parts):
                m, l, acc = carry[hh * parts + r]
                nk = (r + 1) * FLASH_ROWS if masked else tk
                s = _nn(qs[hh][r], kt[:, :nk])
                if masked:
                    row = r * FLASH_ROWS + lax.broadcasted_iota(jnp.int32, s.shape, 0)
                    col = lax.broadcasted_iota(jnp.int32, s.shape, 1)
                    s = jnp.where(col <= row, s, NEG_INF)
                m_new = jnp.maximum(m, jnp.max(s, axis=-1, keepdims=True))
                alpha = jnp.exp2(m - m_new)
                p = jnp.exp2(s - m_new)
                l = alpha * l + jnp.sum(p, axis=-1, keepdims=True)
                acc = alpha * acc + _nn(p.astype(BF16), v[:nk])
                out.append((m_new, l, acc))
        return tuple(out)

    init = tuple((jnp.full((FLASH_ROWS, 1), NEG_INF, F32), jnp.zeros((FLASH_ROWS, 1), F32),
                  jnp.zeros((FLASH_ROWS, LANES), F32)) for _ in range(2 * parts))
    carry = lax.fori_loop(0, n_full, lambda j, c: step(j, c, False), init)
    carry = step(n_full, carry, True)
    lane = lax.broadcasted_iota(jnp.int32, (FLASH_ROWS, LANES), 1)
    for r in range(parts):
        (_, l0, acc0), (_, l1, acc1) = carry[r], carry[parts + r]
        o_ref[r * FLASH_ROWS:(r + 1) * FLASH_ROWS, :] = jnp.where(lane < vdim, acc0 / l0, acc1 / l1).astype(BF16)


def _q_absorb_kernel(q_ref, wuk_ref, gkn_ref, qa_ref, qr_ref, *, heads, nope, rope):
    for h in range(heads):
        q = q_ref[h].astype(F32)
        qg = (q[:, :nope] * gkn_ref[...]).astype(BF16)
        qa_ref[h] = _nn(qg, wuk_ref[h]).astype(BF16)
        qr_ref[h] = (q[:, nope:nope + rope] + q[:, nope + rope:nope + 2 * rope]).astype(BF16)


def _mla_sample_kernel(pt_ref, qa_ref, qr_ref, wukt_ref, cnew_ref, rnew_ref, poolc_ref, poolrt_ref,
                       o_ref, cbuf, rbuf, cb0, cb1, sc0, sc1, sems, *, heads, nope, n_chunks, n_batch, t_new):
    n = pl.program_id(0)
    cp = cbuf.shape[1]
    rows = qa_ref.shape[1]
    lat = qa_ref.shape[2]
    nw = heads * nope
    ppt = SAMPLE_SUB // PAGE
    sets = ((cb0, sc0), (cb1, sc1))

    def page_copies(page, slot, p):
        return (pltpu.make_async_copy(poolc_ref.at[page], cbuf.at[slot, p], sems.at[0, slot]),
                pltpu.make_async_copy(poolrt_ref.at[page], rbuf.at[slot, p], sems.at[1, slot]))

    def start_pages(b, ci, slot, p0, p1):
        for p in range(p0, p1):
            for cpy in page_copies(pt_ref[b, ci * cp + p], slot, p):
                cpy.start(priority=p % 2)

    def wait_chunk(slot):
        for p in range(cp):
            for cpy in page_copies(0, slot, p):
                cpy.wait()

    @pl.when(n == 0)
    def _():
        start_pages(0, 0, 0, 0, cp)

    lhs = jnp.concatenate([wukt_ref[...], qa_ref[0]], axis=0)
    qr = qr_ref[0]

    def scores(c, rope_scores):
        tk = c.shape[0]
        big = _nt(lhs, c)
        knt = big[:nw].reshape(heads, nope, tk)
        r = lax.rsqrt(jnp.sum(knt * knt, axis=1) / nope + RMS_EPS)
        return big[nw:] * jnp.concatenate([r] * t_new, axis=0) + rope_scores

    def update(s, c, carry):
        m, l, acc = carry
        m_new = jnp.maximum(m, jnp.max(s, axis=-1, keepdims=True))
        alpha = jnp.exp2(m - m_new)
        p = jnp.exp2(s - m_new)
        l = alpha * l + jnp.sum(p, axis=-1, keepdims=True)
        acc = alpha * acc + _nn(p.astype(BF16), c)
        return m_new, l, acc

    def absorb(slot, carry):
        cbs, scs = sets[slot]
        return update(scs[...], cbs[...], carry)

    def score_chunk(slot, nxt_b, nxt_ci, carry, absorb_other):
        cbs, scs = sets[slot]
        nt = cp // ppt

        def tiles(j0, j1):
            for j in range(j0, j1):
                c = cbuf[slot, j * ppt:(j + 1) * ppt].reshape(SAMPLE_SUB, lat).astype(BF16)
                krt = jnp.concatenate([rbuf[slot, j * ppt + p] for p in range(ppt)], axis=1).astype(BF16)
                cbs[j * SAMPLE_SUB:(j + 1) * SAMPLE_SUB, :] = c
                scs[:, j * SAMPLE_SUB:(j + 1) * SAMPLE_SUB] = scores(c, _nn(qr, krt))

        wait_chunk(slot)
        start_pages(nxt_b, nxt_ci, 1 - slot, 0, cp // 2)
        tiles(0, nt // 2)
        if absorb_other:
            carry = absorb(1 - slot, carry)
        start_pages(nxt_b, nxt_ci, 1 - slot, cp // 2, cp)
        tiles(nt // 2, nt)
        return carry

    def pair(k, carry):
        carry = score_chunk(1, n, 2 * k + 2, carry, True)
        return score_chunk(0, n, 2 * k + 3, carry, True)

    init = (jnp.full((rows, 1), NEG_INF, F32), jnp.zeros((rows, 1), F32), jnp.zeros((rows, lat), F32))
    carry = score_chunk(0, n, 1, init, False)
    carry = lax.fori_loop(0, (n_chunks - 2) // 2, pair, carry)
    carry = score_chunk(1, jnp.minimum(n + 1, n_batch - 1), 0, carry, True)
    carry = absorb(1, carry)

    @pl.when(n == n_batch - 1)
    def _():
        wait_chunk(0)

    c = cnew_ref[0].astype(BF16)
    tk = c.shape[0]
    t_of_row = lax.broadcasted_iota(jnp.int32, (rows, tk), 0) // heads
    col = lax.broadcasted_iota(jnp.int32, (rows, tk), 1)
    s = jnp.where(col <= t_of_row, scores(c, _nt(qr, rnew_ref[0].astype(BF16))), NEG_INF)
    m, l, acc = update(s, c, carry)
    o_ref[0] = acc / l


def _unabsorb_kernel(o_ref, wuv_ref, out_ref, *, heads):
    out_ref[...] = jnp.concatenate(
        [_nn(o_ref[h], wuv_ref[h]) for h in range(heads)], axis=-1).astype(BF16)


def _oproj_router_kernel(attp_ref, atts_ref, w_ref, hp_ref, hs_ref, agate_ref, g_ref, sh_ref, sc_ref,
                         whi_ref, wlo_ref, b_ref, h_ref, u_ref, idx_ref, gate_ref, *, n_prompt_blocks):
    is_prompt = pl.program_id(0) < n_prompt_blocks
    att = jnp.where(is_prompt, attp_ref[...], atts_ref[...])
    y = _nn(att, w_ref[...])
    tm, d = y.shape
    agate = agate_ref[...]
    nb = agate.shape[0]
    h = jnp.where(is_prompt, hp_ref[...], hs_ref[...]) + (y.reshape(tm // nb, nb, d) * agate[None]).reshape(tm, d)
    h_ref[...] = h
    u = _norm_mod(h, g_ref[...], sh_ref[...], sc_ref[...])
    uhi = u.astype(BF16)
    ulo = (u - uhi.astype(F32)).astype(BF16)
    u_ref[...] = u
    whi = whi_ref[...]
    logits = _nt(whi, uhi) + _nt(whi, ulo) + _nt(wlo_ref[...], uhi) + b_ref[...]
    ne = logits.shape[0]
    eid = lax.broadcasted_iota(jnp.int32, logits.shape, 0)
    work = logits
    vals, idxs = [], []
    for _ in range(TOP_K):
        m = jnp.max(work, axis=0, keepdims=True)
        idx = jnp.min(jnp.where(work == m, eid, ne), axis=0, keepdims=True)
        vals.append(m)
        idxs.append(idx)
        work = jnp.where(eid == idx, -jnp.inf, work)
    es = [jnp.exp(v - vals[0]) for v in vals]
    den = es[0] + es[1] + es[2] + es[3]
    idx_ref[...] = jnp.concatenate(idxs, axis=0)
    gate_ref[...] = jnp.concatenate([e / den for e in es], axis=0)


def _rank_kernel(idx_ref, rank_ref, cnt_ref, carry, *, ne):
    @pl.when(pl.program_id(0) == 0)
    def _():
        carry[...] = jnp.zeros(carry.shape, F32)

    idx = idx_ref[...]
    tb = idx.shape[1]
    eid = lax.broadcasted_iota(jnp.int32, (ne, tb), 0)
    hits = [eid == idx[k:k + 1, :] for k in range(TOP_K)]
    oh = jnp.where(hits[0], 1.0, 0.0)
    for k in range(1, TOP_K):
        oh = oh + jnp.where(hits[k], 1.0, 0.0)
    upper = jnp.where(lax.broadcasted_iota(jnp.int32, (tb, tb), 0) < lax.broadcasted_iota(jnp.int32, (tb, tb), 1),
                      1.0, 0.0).astype(BF16)
    before = _nn(oh.astype(BF16), upper) + carry[...]
    rank_ref[...] = jnp.concatenate(
        [jnp.sum(jnp.where(h, before, 0.0), axis=0, keepdims=True) for h in hits], axis=0).astype(jnp.int32)
    carry[...] = carry[...] + jnp.sum(oh, axis=1, keepdims=True)
    cnt_ref[...] = carry[...]


def _expert_kernel(be_ref, nrows_ref, nused_ref, blk_ref, x_ref, wup_ref, bup_ref, wdn_ref, bdn_ref, y_ref, wup_bf, wdn_bf,
                   *, ff):
    i = pl.program_id(0)
    n_here = nrows_ref[i]
    prev = be_ref[jnp.maximum(i - 1, 0)]
    new_expert = (i == 0) | (be_ref[i] != prev)

    @pl.when(new_expert & (n_here > 0))
    def _():
        wup_bf[...] = wup_ref[0].astype(BF16)
        wdn_bf[...] = wdn_ref[0].astype(BF16)

    for r0 in range(0, MOE_TM, MOE_SUB):
        @pl.when(n_here > r0)
        def _(r0=r0):
            x = x_ref[r0:r0 + MOE_SUB, :].astype(BF16)
            fc = 512
            acc = jnp.zeros((MOE_SUB, wdn_bf.shape[1]), F32)
            for c0 in range(0, ff, fc):
                glu = _nn(x, wup_bf[:, c0:c0 + fc]) + bup_ref[0, :, c0:c0 + fc]
                lin = _nn(x, wup_bf[:, ff + c0:ff + c0 + fc]) + bup_ref[0, :, ff + c0:ff + c0 + fc]
                glu = jnp.minimum(glu, SWIGLU_LIMIT)
                lin = jnp.clip(lin, -SWIGLU_LIMIT, SWIGLU_LIMIT)
                act = glu * jax.nn.sigmoid(SWIGLU_ALPHA * glu) * (lin + 1.0)
                acc = acc + _nn(act.astype(BF16), wdn_bf[c0:c0 + fc, :])
            y_ref[r0:r0 + MOE_SUB, :] = _pack_bf16_pairs(acc + bdn_ref[0])

        @pl.when(n_here <= r0)
        def _(r0=r0):
            y_ref[r0:r0 + MOE_SUB, :] = jnp.zeros((MOE_SUB, y_ref.shape[1]), jnp.uint32)


def _combine_value(h_ref, y_ref, gk_ref, gate_ref):
    gk = gk_ref[...]
    ya, yb = _unpack_bf16_pairs(y_ref[0])
    ya, yb = ya * gk[:, 0:1], yb * gk[:, 0:1]
    for k in range(1, TOP_K):
        a, b = _unpack_bf16_pairs(y_ref[k])
        ya, yb = ya + a * gk[:, k:k + 1], yb + b * gk[:, k:k + 1]
    y = jnp.concatenate([ya, yb], axis=1)
    tm, d = y.shape
    gate = gate_ref[...]
    nb = gate.shape[0]
    return h_ref[...] + (y.reshape(tm // nb, nb, d) * gate[None]).reshape(tm, d)


def _combine_split_kernel(h_ref, y_ref, gk_ref, gate_ref, op_ref, os_ref, *, n_prompt_blocks):
    res = _combine_value(h_ref, y_ref, gk_ref, gate_ref)

    @pl.when(pl.program_id(0) < n_prompt_blocks)
    def _():
        op_ref[...] = res

    @pl.when(pl.program_id(0) >= n_prompt_blocks)
    def _():
        os_ref[...] = res


def _combine_kv_q_kernel(h_ref, y_ref, gk_ref, gate_ref, gkv_ref, kvsh_ref, kvsc_ref, wkv_ref, gk64_ref,
                         ga_ref, ash_ref, asc_ref, wq_ref, gq_ref,
                         h_out, k_ref, v_ref, khm_ref, vhm_ref, q_ref, *, kvh, hd, heads, scale):
    res = _combine_value(h_ref, y_ref, gk_ref, gate_ref)
    h_out[...] = res
    _shared_kv(res, gkv_ref, kvsh_ref, kvsc_ref, wkv_ref, gk64_ref, k_ref, v_ref, khm_ref, vhm_ref, kvh=kvh, hd=hd)
    _swa_q(res, ga_ref, ash_ref, asc_ref, wq_ref, gq_ref, q_ref, heads=heads, hd=hd, scale=scale)


def _shared_kv(h, g_ref, sh_ref, sc_ref, w_ref, gk_ref, k_ref, v_ref, khm_ref, vhm_ref, *, kvh, hd):
    u = _norm_mod(h, g_ref[...], sh_ref[...], sc_ref[...]).astype(BF16)
    a = _nn(u, w_ref[...])
    kw = kvh * hd
    gk = gk_ref[...]
    ks = []
    for j in range(kw // LANES):
        x = a[:, j * LANES:(j + 1) * LANES]
        ks.append(x * _group64_rscale(x) * gk[:, j * LANES:(j + 1) * LANES])
    k = jnp.concatenate(ks, axis=-1)
    v = a[:, kw:2 * kw]
    k_ref[...] = k
    v_ref[...] = v
    for hh in range(kvh):
        khm_ref[hh] = k[:, hh * hd:(hh + 1) * hd].astype(BF16)
        vhm_ref[hh] = v[:, hh * hd:(hh + 1) * hd].astype(BF16)


def _swa_q(h, g_ref, sh_ref, sc_ref, w_ref, gq_ref, q_ref, *, heads, hd, scale):
    u = _norm_mod(h, g_ref[...], sh_ref[...], sc_ref[...]).astype(BF16)
    a = _nn(u, w_ref[...])
    gq = gq_ref[...]
    for j in range(heads * hd // LANES):
        x = a[:, j * LANES:(j + 1) * LANES]
        y = x * _group64_rscale(x) * gq[:, j * LANES:(j + 1) * LANES] * scale
        q_ref[2 * j] = y[:, :hd].astype(BF16)
        q_ref[2 * j + 1] = y[:, hd:].astype(BF16)


def _softmax_pv(s, v):
    m = jnp.max(s, axis=-1, keepdims=True)
    e = jnp.exp(s - m)
    return _nn(e.astype(BF16), v) / jnp.sum(e, axis=-1, keepdims=True)


def _swa_prompt_kernel(bias_ref, q_ref, kp_ref, kc_ref, vp_ref, vc_ref, o_ref, *, heads, kvh, tq):
    group = heads // kvh
    rows = group * tq
    not_first = lax.broadcasted_iota(jnp.int32, (WINDOW + tq, q_ref.shape[2]), 0) > 0
    outs = []
    for g in range(kvh):
        k = jnp.where(not_first, jnp.concatenate([kp_ref[g], kc_ref[g]], axis=0), 0)
        v = jnp.where(not_first, jnp.concatenate([vp_ref[g], vc_ref[g]], axis=0), 0)
        q = q_ref[g * group:(g + 1) * group].reshape(rows, q_ref.shape[2])
        o = _softmax_pv(_nt(q, k) + bias_ref[g], v)
        for j in range(group):
            outs.append(o[j * tq:(j + 1) * tq])
    o_ref[...] = jnp.concatenate(outs, axis=-1).astype(BF16)


def _swa_sample_kernel(bias_ref, q_ref, k_ref, v_ref, o_ref):
    bias = bias_ref[...]
    for b in range(q_ref.shape[0]):
        o_ref[b] = _softmax_pv(_nt(q_ref[b], k_ref[b]) + bias, v_ref[b]).astype(BF16)


def _rope_tables(pos, rope):
    half = rope // 2
    inv = ROPE_THETA ** (-jnp.arange(half, dtype=F32) / half)
    ang = pos.astype(F32)[:, None] * inv[None, :]
    cos, sin = jnp.cos(ang), jnp.sin(ang)
    return jnp.concatenate([cos, cos], axis=-1), jnp.concatenate([sin, sin], axis=-1)


def _rot_cols(w):
    half = w.shape[-1] // 2
    return jnp.concatenate([-w[..., half:], w[..., :half]], axis=-1)


def _swap_halves(g):
    half = g.shape[-1] // 2
    return jnp.concatenate([g[..., half:], g[..., :half]], axis=-1)


def _attn_out_moe(att_p, att_s, w_o, h_p, h_s, h_s_idx, m_rows, tab, layer, g_ffn, w_router, b_router,
                  w_up, b_up, w_down, b_down, *, seq, nb_prompt, n_prompt_blocks, split_rows=None,
                  next_layer=None):
    d = h_p.shape[1]
    ne = w_router.shape[1]
    ff = w_down.shape[2]
    nb = tab.shape[2]
    n_tok_blocks = m_rows // TM
    col0 = 3

    def tab_spec(col):
        return pl.BlockSpec((None, None, nb, d),
                            lambda i: (layer, jnp.minimum(i * TM // seq, nb_prompt), 0, col))

    def prompt_blk(i):
        return (jnp.minimum(i, n_prompt_blocks - 1), 0)

    def sample_blk(i):
        return (jnp.maximum(i - n_prompt_blocks, 0), 0)

    wr_t = w_router.T
    wr_hi = wr_t.astype(BF16)
    wr_lo = (wr_t - wr_hi.astype(F32)).astype(BF16)
    h, u, idx_t, gate_t = pl.pallas_call(
        functools.partial(_oproj_router_kernel, n_prompt_blocks=n_prompt_blocks),
        grid=(n_tok_blocks,),
        in_specs=[pl.BlockSpec((TM, w_o.shape[0]), prompt_blk), pl.BlockSpec((TM, w_o.shape[0]), sample_blk),
                  pl.BlockSpec(w_o.shape, lambda i: (0, 0)),
                  pl.BlockSpec((TM, d), prompt_blk), pl.BlockSpec((TM, d), h_s_idx), tab_spec(2),
                  pl.BlockSpec((1, d), lambda i: (0, 0)),
                  tab_spec(col0), tab_spec(col0 + 1),
                  pl.BlockSpec((ne, d), lambda i: (0, 0)),
                  pl.BlockSpec((ne, d), lambda i: (0, 0)),
                  pl.BlockSpec((ne, 1), lambda i: (0, 0))],
        out_specs=[pl.BlockSpec((TM, d), lambda i: (i, 0)),
                   pl.BlockSpec((TM, d), lambda i: (i, 0)),
                   pl.BlockSpec((TOP_K, TM), lambda i: (0, i)),
                   pl.BlockSpec((TOP_K, TM), lambda i: (0, i))],
        out_shape=[jax.ShapeDtypeStruct((m_rows, d), F32),
                   jax.ShapeDtypeStruct((m_rows, d), F32),
                   jax.ShapeDtypeStruct((TOP_K, m_rows), jnp.int32),
                   jax.ShapeDtypeStruct((TOP_K, m_rows), F32)],
        compiler_params=_params("parallel"),
        name="oproj_router",
    )(att_p, att_s, w_o.astype(BF16), h_p, h_s, tab, g_ffn.reshape(1, d), tab, tab, wr_hi, wr_lo,
      b_router.reshape(ne, 1))

    a = m_rows * TOP_K
    rank_t, cnt = pl.pallas_call(
        functools.partial(_rank_kernel, ne=ne),
        grid=(n_tok_blocks,),
        in_specs=[pl.BlockSpec((TOP_K, TM), lambda i: (0, i))],
        out_specs=[pl.BlockSpec((TOP_K, TM), lambda i: (0, i)), pl.BlockSpec((ne, 1), lambda i: (0, 0))],
        out_shape=[jax.ShapeDtypeStruct((TOP_K, m_rows), jnp.int32), jax.ShapeDtypeStruct((ne, 1), F32)],
        scratch_shapes=[pltpu.VMEM((ne, 1), F32)],
        compiler_params=_params("arbitrary"),
        name="moe_rank",
    )(idx_t)
    counts = cnt[:, 0].astype(jnp.int32)
    padded = (counts + MOE_TM - 1) // MOE_TM * MOE_TM
    pad_end = jnp.cumsum(padded)
    start = pad_end - padded
    experts = jnp.arange(ne, dtype=jnp.int32)
    pos_t = rank_t + jnp.sum(jnp.where(idx_t[..., None] == experts, start, 0), axis=-1)
    fbits = (a - 1).bit_length()
    assert ne << fbits < 2 ** 31
    keys = (idx_t.T.reshape(a) << fbits) | jnp.arange(a, dtype=jnp.int32)
    order = jnp.sort(keys) & ((1 << fbits) - 1)
    shift = start - (jnp.cumsum(counts) - counts)
    n_blk = -(-a // MOE_TM) + ne
    n_slots = n_blk * MOE_TM
    blk_start = jnp.arange(n_blk, dtype=jnp.int32) * MOE_TM
    blk_expert = jnp.minimum(jnp.sum(pad_end[None, :] <= blk_start[:, None], axis=1), ne - 1).astype(jnp.int32)
    n_used = (pad_end[-1] // MOE_TM).astype(jnp.int32).reshape(1)
    blk_rows = jnp.clip((pad_end - padded + counts)[blk_expert] - blk_start, 0, MOE_TM)
    blk_rows = jnp.where(blk_start < pad_end[-1], blk_rows, 0).astype(jnp.int32)
    slot_sorted = jnp.arange(n_slots, dtype=jnp.int32) - jnp.repeat(shift[blk_expert], MOE_TM)
    slot_tok = order[jnp.clip(slot_sorted, 0, a - 1)] // TOP_K
    xs = jnp.take(u, slot_tok, axis=0, mode="clip")
    steps = jnp.arange(n_blk, dtype=jnp.int32)
    blk_of_step = jnp.where(steps < n_used[0], (start // MOE_TM + pad_end // MOE_TM - 1)[blk_expert] - steps, steps)
    blk_of_step = blk_of_step.astype(jnp.int32)
    step_rows = blk_rows[blk_of_step]

    def live(i, nused):
        return jnp.minimum(i, nused[0] - 1)

    yb = pl.pallas_call(
        functools.partial(_expert_kernel, ff=ff),
        grid_spec=pltpu.PrefetchScalarGridSpec(
            num_scalar_prefetch=4,
            grid=(n_blk,),
            in_specs=[pl.BlockSpec((MOE_TM, d), lambda i, be, nr, nu, bs: (bs[live(i, nu)], 0)),
                      pl.BlockSpec((None, 1, d, 2 * ff), lambda i, be, nr, nu, bs: (layer, be[live(i, nu)], 0, 0)),
                      pl.BlockSpec((None, 1, 1, 2 * ff), lambda i, be, nr, nu, bs: (layer, be[live(i, nu)], 0, 0)),
                      pl.BlockSpec((None, 1, ff, d), lambda i, be, nr, nu, bs: (layer, be[live(i, nu)], 0, 0)),
                      pl.BlockSpec((None, 1, 1, d), lambda i, be, nr, nu, bs: (layer, be[live(i, nu)], 0, 0))],
            out_specs=pl.BlockSpec((MOE_TM, d // 2), lambda i, be, nr, nu, bs: (bs[i], 0)),
            scratch_shapes=[pltpu.VMEM((d, 2 * ff), BF16), pltpu.VMEM((ff, d), BF16)]),
        out_shape=jax.ShapeDtypeStruct((n_slots, d // 2), jnp.uint32),
        compiler_params=_params("arbitrary"),
        name="experts",
    )(blk_expert, step_rows, n_used, blk_of_step, xs, w_up, b_up.reshape(b_up.shape[0], ne, 1, 2 * ff), w_down,
      b_down.reshape(b_down.shape[0], ne, 1, d))

    ysel = jnp.take(yb, pos_t.reshape(a), axis=0, mode="clip").reshape(TOP_K, m_rows, d // 2)
    combine_specs = [pl.BlockSpec((TM, d), lambda i: (i, 0)),
                     pl.BlockSpec((TOP_K, TM, d // 2), lambda i: (0, i, 0)),
                     pl.BlockSpec((TM, TOP_K), lambda i: (i, 0)),
                     tab_spec(col0 + 2)]
    combine_args = (h, ysel, gate_t.T, tab)
    if split_rows is not None:
        return pl.pallas_call(
            functools.partial(_combine_split_kernel, n_prompt_blocks=n_prompt_blocks),
            grid=(n_tok_blocks,),
            in_specs=combine_specs,
            out_specs=[pl.BlockSpec((TM, d), prompt_blk), pl.BlockSpec((TM, d), sample_blk)],
            out_shape=[jax.ShapeDtypeStruct((split_rows, d), F32),
                       jax.ShapeDtypeStruct((m_rows - split_rows, d), F32)],
            compiler_params=_params("arbitrary"),
            name="moe_combine",
        )(*combine_args)
    kvh, hd, heads = next_layer["kvh"], next_layer["hd"], next_layer["heads"]
    kw = kvh * hd
    tab_kv = next_layer["tab_kv"]

    def next_tab_spec(which_layer, col):
        return pl.BlockSpec((None, None, nb, d),
                            lambda i: (which_layer, jnp.minimum(i * TM // seq, nb_prompt), 0, col))

    row = pl.BlockSpec((TM, d), lambda i: (i, 0))
    return pl.pallas_call(
        functools.partial(_combine_kv_q_kernel, kvh=kvh, hd=hd, heads=heads, scale=hd ** -0.5),
        grid=(n_tok_blocks,),
        in_specs=combine_specs + [
            pl.BlockSpec((1, d), lambda i: (0, 0)), next_tab_spec(0, 0), next_tab_spec(0, 1),
            pl.BlockSpec((d, 2 * kw), lambda i: (0, 0)), pl.BlockSpec((1, kw), lambda i: (0, 0)),
            pl.BlockSpec((1, d), lambda i: (0, 0)), next_tab_spec(layer + 1, 0), next_tab_spec(layer + 1, 1),
            pl.BlockSpec((d, heads * hd), lambda i: (0, 0)), pl.BlockSpec((1, heads * hd), lambda i: (0, 0))],
        out_specs=[row, pl.BlockSpec((TM, kw), lambda i: (i, 0)), pl.BlockSpec((TM, kw), lambda i: (i, 0)),
                   pl.BlockSpec((kvh, TM, hd), lambda i: (0, i, 0)), pl.BlockSpec((kvh, TM, hd), lambda i: (0, i, 0)),
                   pl.BlockSpec((heads, TM, hd), lambda i: (0, i, 0))],
        out_shape=[jax.ShapeDtypeStruct((m_rows, d), F32),
                   jax.ShapeDtypeStruct((m_rows, kw), F32), jax.ShapeDtypeStruct((m_rows, kw), F32),
                   jax.ShapeDtypeStruct((kvh, m_rows, hd), BF16), jax.ShapeDtypeStruct((kvh, m_rows, hd), BF16),
                   jax.ShapeDtypeStruct((heads, m_rows, hd), BF16)],
        compiler_params=_params("parallel"),
        name="moe_combine_kv_q",
    )(*combine_args, next_layer["g_kv_norm"].reshape(1, d), tab_kv, tab_kv, next_layer["w_kv"].astype(BF16),
      jnp.tile(next_layer["g_k"], kvh).reshape(1, kw),
      next_layer["g_attn"].reshape(1, d), tab, tab, next_layer["w_q"].astype(BF16),
      jnp.tile(next_layer["g_q"], heads).reshape(1, heads * hd))


def kernel(x_prompt, x_sample, c_prompt, c_sample, cache_mla_latent, cache_mla_krope, state_win_k, state_win_v, page_table, w_mod, b_mod, g_attn, g_ffn, w_mla_down, g_mla_q_lora, g_mla_kv_lora, w_mla_uq, w_mla_uk, w_mla_uv, g_mla_qn, g_mla_qr, g_mla_kn, g_mla_kr, w_mla_o, w_kvmod, b_kvmod, g_kv_norm, w_kv, g_swa_k, w_swa_q, g_swa_q, swa_sinks, w_swa_o, w_router, b_router, w_up, b_up, w_down, b_down):
    nbp, seq, d = x_prompt.shape
    nbs, t_new, _ = x_sample.shape
    q_lora = g_mla_q_lora.shape[1]
    kv_lora = g_mla_kv_lora.shape[1]
    heads, nope = w_mla_uk.shape[2], w_mla_uk.shape[3]
    rope = g_mla_qr.shape[1]
    vdim = w_mla_uv.shape[3]
    n_pages = page_table.shape[1]
    past = n_pages * PAGE
    swa_heads = swa_sinks.shape[1]
    hd = g_swa_k.shape[0]
    kvh = w_kv.shape[1] // (2 * hd)
    assert nope + 2 * rope == LANES and 2 * vdim == LANES and 2 * hd == LANES and TM % FLASH_ROWS == 0
    assert seq % TM == 0 and TM % nbs == 0 and (nbs * t_new) % TM == 0 and nbs % 8 == 0
    assert w_mod.shape[0] == 2 and n_pages % (2 * SAMPLE_PAGES) == 0

    mp = nbp * seq
    ms = nbs * t_new
    m_rows = mp + ms
    n_tok_blocks = m_rows // TM
    n_prompt_blocks = mp // TM
    blocks_per_seq = seq // TM

    def tab_idx(i):
        return jnp.minimum(i * TM // seq, nbp)

    def tok(i):
        return (i, 0)

    def const2(i):
        return (0, 0)

    h0_p = x_prompt.reshape(mp, d)
    h0_s = x_sample.transpose(1, 0, 2).reshape(ms, d)

    def prompt_blk(i):
        return (jnp.minimum(i, n_prompt_blocks - 1), 0)

    def sample_blk(i):
        return (jnp.maximum(i - n_prompt_blocks, 0), 0)

    c_all = jnp.concatenate([c_prompt, c_sample], axis=0)
    tab = _adaln_table(c_all, w_mod, b_mod, nbp)
    tab_kv = _adaln_table(c_all, w_kvmod[None], b_kvmod[None], nbp)

    def tab_spec(layer, col):
        return pl.BlockSpec((None, None, nbs, d), lambda i: (layer, tab_idx(i), 0, col))

    cos_p, sin_p = _rope_tables(jnp.arange(seq), rope)
    cos_s, sin_s = _rope_tables(past + jnp.arange(t_new), rope)
    cos_tab = jnp.concatenate([cos_p, jnp.repeat(cos_s, nbs, axis=0)], axis=0)
    sin_tab = jnp.concatenate([sin_p, jnp.repeat(sin_s, nbs, axis=0)], axis=0)

    def pos_blk(i):
        return (jnp.where(i < n_prompt_blocks, i % blocks_per_seq, blocks_per_seq + i - n_prompt_blocks), 0)

    wd = w_mla_down[0]
    w_down_ext = jnp.concatenate([wd, _rot_cols(wd[:, q_lora + kv_lora:])], axis=1).astype(BF16)
    nd = w_down_ext.shape[1]
    cq, ckv_p, ckv_s, kr_p, kr_s = pl.pallas_call(
        functools.partial(_mla_down_kernel, q_lora=q_lora, kv_lora=kv_lora, rope=rope,
                          n_prompt_blocks=n_prompt_blocks),
        grid=(n_tok_blocks,),
        in_specs=[pl.BlockSpec((TM, d), prompt_blk), pl.BlockSpec((TM, d), sample_blk),
                  pl.BlockSpec((1, d), const2), tab_spec(0, 0), tab_spec(0, 1),
                  pl.BlockSpec((d, nd), const2), pl.BlockSpec((1, q_lora), const2),
                  pl.BlockSpec((1, kv_lora), const2), pl.BlockSpec((1, rope), const2),
                  pl.BlockSpec((1, rope), const2), pl.BlockSpec((TM, rope), pos_blk),
                  pl.BlockSpec((TM, rope), pos_blk)],
        out_specs=[pl.BlockSpec((TM, q_lora), tok),
                   pl.BlockSpec((TM, kv_lora), prompt_blk), pl.BlockSpec((TM, kv_lora), sample_blk),
                   pl.BlockSpec((TM, rope), prompt_blk), pl.BlockSpec((TM, rope), sample_blk)],
        out_shape=[jax.ShapeDtypeStruct((m_rows, q_lora), BF16),
                   jax.ShapeDtypeStruct((mp, kv_lora), F32), jax.ShapeDtypeStruct((ms, kv_lora), F32),
                   jax.ShapeDtypeStruct((mp, rope), F32), jax.ShapeDtypeStruct((ms, rope), F32)],
        compiler_params=_params("arbitrary"),
        name="mla_down",
    )(h0_p, h0_s, g_attn[0].reshape(1, d), tab, tab, w_down_ext, g_mla_q_lora, g_mla_kv_lora,
      g_mla_kr, _swap_halves(g_mla_kr), cos_tab, sin_tab)

    wq = w_mla_uq[0]
    w_q_cat = jnp.concatenate([wq, _rot_cols(wq[..., nope:])], axis=-1).reshape(q_lora, heads * LANES).astype(BF16)
    gq_cat = jnp.concatenate([g_mla_qn[0], g_mla_qr[0], _swap_halves(g_mla_qr[0])]).reshape(1, LANES)
    q_tab = jnp.concatenate([jnp.ones((cos_tab.shape[0], nope), F32), cos_tab, sin_tab], axis=1)
    mla_scale = (nope + rope) ** -0.5 * LOG2_E
    q_cat = pl.pallas_call(
        functools.partial(_q_up_kernel, heads=heads, nope=nope, rope=rope, scale=mla_scale),
        grid=(n_tok_blocks,),
        in_specs=[pl.BlockSpec((TM, q_lora), tok), pl.BlockSpec((q_lora, heads * LANES), const2),
                  pl.BlockSpec((1, LANES), const2), pl.BlockSpec((TM, LANES), pos_blk)],
        out_specs=pl.BlockSpec((heads, TM, LANES), lambda i: (0, i, 0)),
        out_shape=jax.ShapeDtypeStruct((heads, m_rows, LANES), BF16),
        compiler_params=_params("parallel"),
        name="q_up",
    )(cq, w_q_cat, gq_cat, q_tab)

    wuk = w_mla_uk[0]
    wuv = w_mla_uv[0]
    wuk_t = wuk.reshape(kv_lora, heads * nope).T.astype(BF16)
    tk = TM
    nk = seq // tk
    kt, v_p = pl.pallas_call(
        functools.partial(_kv_up_kernel, heads=heads, nope=nope, rope=rope),
        grid=(n_prompt_blocks,),
        in_specs=[pl.BlockSpec((TM, kv_lora), tok), pl.BlockSpec((TM, rope), tok),
                  pl.BlockSpec((heads * nope, kv_lora), const2), pl.BlockSpec((nope, 1), const2),
                  pl.BlockSpec((kv_lora, heads * vdim), const2), pl.BlockSpec((rope, rope), const2)],
        out_specs=[pl.BlockSpec((1, heads, 1, LANES, tk), lambda i: (i // nk, 0, i % nk, 0, 0)),
                   pl.BlockSpec((TM, heads * vdim), tok)],
        out_shape=[jax.ShapeDtypeStruct((nbp, heads, nk, LANES, tk), BF16),
                   jax.ShapeDtypeStruct((mp, heads * vdim), BF16)],
        compiler_params=_params("parallel"),
        name="kv_up",
    )(ckv_p, kr_p, wuk_t, g_mla_kn[0].reshape(nope, 1), wuv.reshape(kv_lora, heads * vdim).astype(BF16),
      jnp.eye(rope, dtype=BF16))

    tq = tk
    nq = seq // tq
    att_p = pl.pallas_call(
        functools.partial(_mla_flash_kernel, tq=tq, tk=tk, vdim=vdim),
        grid=(nbp, heads // 2, nq),
        in_specs=[pl.BlockSpec((2, tq, LANES), lambda b, hp, qi: (hp, b * nq + qi, 0)),
                  pl.BlockSpec((1, 2, nk, LANES, tk), lambda b, hp, qi: (b, hp, 0, 0, 0)),
                  pl.BlockSpec((seq, LANES), lambda b, hp, qi: (b, hp))],
        out_specs=pl.BlockSpec((tq, LANES), lambda b, hp, qi: (b * nq + qi, hp)),
        out_shape=jax.ShapeDtypeStruct((mp, heads * vdim), BF16),
        compiler_params=_params("parallel", "parallel", "arbitrary"),
        name="mla_prompt_attention",
    )(q_cat, kt, v_p)

    first_sample_blk = mp // ms
    qa, qr = pl.pallas_call(
        functools.partial(_q_absorb_kernel, heads=heads, nope=nope, rope=rope),
        grid=(1,),
        in_specs=[pl.BlockSpec((heads, ms, LANES), lambda i: (0, first_sample_blk, 0)),
                  pl.BlockSpec((heads, nope, kv_lora), lambda i: (0, 0, 0)),
                  pl.BlockSpec((1, nope), const2)],
        out_specs=[pl.BlockSpec((heads, ms, kv_lora), lambda i: (0, 0, 0)),
                   pl.BlockSpec((heads, ms, rope), lambda i: (0, 0, 0))],
        out_shape=[jax.ShapeDtypeStruct((heads, ms, kv_lora), BF16),
                   jax.ShapeDtypeStruct((heads, ms, rope), BF16)],
        compiler_params=_params("arbitrary"),
        name="q_absorb",
    )(q_cat, wuk.transpose(1, 2, 0).astype(BF16), g_mla_kn)
    rows = t_new * heads

    def per_batch(x):
        return x.reshape(heads, t_new, nbs, -1).transpose(2, 1, 0, 3).reshape(nbs, rows, -1)

    def new_rows(x):
        x = x.reshape(t_new, nbs, -1).transpose(1, 0, 2)
        return jnp.pad(x, ((0, 0), (0, 8 - t_new), (0, 0)))

    n_chunks = n_pages // SAMPLE_PAGES
    sample_tokens = SAMPLE_PAGES * PAGE
    o_lat = pl.pallas_call(
        functools.partial(_mla_sample_kernel, heads=heads, nope=nope, n_chunks=n_chunks, n_batch=nbs,
                          t_new=t_new),
        grid_spec=pltpu.PrefetchScalarGridSpec(
            num_scalar_prefetch=1,
            grid=(nbs,),
            in_specs=[pl.BlockSpec((1, rows, kv_lora), lambda n, pt: (n, 0, 0)),
                      pl.BlockSpec((1, rows, rope), lambda n, pt: (n, 0, 0)),
                      pl.BlockSpec((heads * nope, kv_lora), lambda n, pt: (0, 0)),
                      pl.BlockSpec((1, 8, kv_lora), lambda n, pt: (n, 0, 0)),
                      pl.BlockSpec((1, 8, rope), lambda n, pt: (n, 0, 0)),
                      pl.BlockSpec(memory_space=pl.ANY),
                      pl.BlockSpec(memory_space=pl.ANY)],
            out_specs=pl.BlockSpec((1, rows, kv_lora), lambda n, pt: (n, 0, 0)),
            scratch_shapes=[pltpu.VMEM((2, SAMPLE_PAGES, PAGE, kv_lora), F32),
                            pltpu.VMEM((2, SAMPLE_PAGES, rope, PAGE), F32),
                            pltpu.VMEM((sample_tokens, kv_lora), BF16),
                            pltpu.VMEM((sample_tokens, kv_lora), BF16),
                            pltpu.VMEM((rows, sample_tokens), F32),
                            pltpu.VMEM((rows, sample_tokens), F32),
                            pltpu.SemaphoreType.DMA((2, 2))]),
        out_shape=jax.ShapeDtypeStruct((nbs, rows, kv_lora), F32),
        compiler_params=_params("arbitrary"),
        name="mla_sample_attention",
    )(page_table, per_batch(qa), per_batch(qr), wuk_t, new_rows(ckv_s), new_rows(kr_s),
      cache_mla_latent[0], cache_mla_krope[0].transpose(0, 2, 1))
    o_lat = o_lat.reshape(nbs, t_new, heads, kv_lora).transpose(2, 1, 0, 3).reshape(heads, ms, kv_lora)
    att_s = pl.pallas_call(
        functools.partial(_unabsorb_kernel, heads=heads),
        grid=(1,),
        in_specs=[pl.BlockSpec((heads, ms, kv_lora), lambda i: (0, 0, 0)),
                  pl.BlockSpec((heads, kv_lora, vdim), lambda i: (0, 0, 0))],
        out_specs=pl.BlockSpec((ms, heads * vdim), const2),
        out_shape=jax.ShapeDtypeStruct((ms, heads * vdim), BF16),
        compiler_params=_params("arbitrary"),
        name="unabsorb",
    )(o_lat.astype(BF16), wuv.transpose(1, 0, 2).astype(BF16))

    def sample_blk_of_all(i):
        return (n_prompt_blocks + jnp.maximum(i - n_prompt_blocks, 0), 0)

    h1, k_all, v_all, k_hm, v_hm, q_hm = _attn_out_moe(
        att_p, att_s, w_mla_o[0], h0_p, h0_s, sample_blk, m_rows, tab, 0, g_ffn[0],
        w_router[0], b_router[0], w_up, b_up, w_down, b_down,
        seq=seq, nb_prompt=nbp, n_prompt_blocks=n_prompt_blocks,
        next_layer=dict(kvh=kvh, hd=hd, heads=swa_heads, tab_kv=tab_kv, g_kv_norm=g_kv_norm, w_kv=w_kv,
                        g_k=g_swa_k, g_attn=g_attn[1], w_q=w_swa_q[0], g_q=g_swa_q[0]))


    group = swa_heads // kvh
    slopes = 2.0 ** (-8.0 * jnp.arange(1, swa_heads + 1, dtype=F32) / swa_heads)
    sinks = swa_sinks[0].astype(F32)

    def bias_table(dist, valid, sink_col, head_of_row):
        b = jnp.where(valid, -slopes[head_of_row][:, None] * dist.astype(F32), NEG_INF)
        return jnp.where(sink_col, sinks[head_of_row][:, None], b)

    tq = WINDOW
    wblocks = seq // tq
    r = jnp.arange(group * tq)
    col = jnp.arange(WINDOW + tq)
    dist = (r % tq)[:, None] + WINDOW - col[None, :]
    inside = (dist >= 0) & (dist < WINDOW)
    bias_p = jnp.stack([
        jnp.stack([bias_table(dist, inside & ok[None, :], (col == 0)[None, :], g * group + r // tq)
                   for g in range(kvh)])
        for ok in (col >= 0, col >= WINDOW)])
    att_p = pl.pallas_call(
        functools.partial(_swa_prompt_kernel, heads=swa_heads, kvh=kvh, tq=tq),
        grid=(nbp, wblocks),
        in_specs=[pl.BlockSpec((None, kvh, group * tq, WINDOW + tq), lambda b, i: (jnp.where(i == 0, 1, 0), 0, 0, 0)),
                  pl.BlockSpec((swa_heads, tq, hd), lambda b, i: (0, b * wblocks + i, 0)),
                  pl.BlockSpec((kvh, WINDOW, hd), lambda b, i: (0, jnp.maximum(b * wblocks + i - 1, 0), 0)),
                  pl.BlockSpec((kvh, tq, hd), lambda b, i: (0, b * wblocks + i, 0)),
                  pl.BlockSpec((kvh, WINDOW, hd), lambda b, i: (0, jnp.maximum(b * wblocks + i - 1, 0), 0)),
                  pl.BlockSpec((kvh, tq, hd), lambda b, i: (0, b * wblocks + i, 0))],
        out_specs=pl.BlockSpec((tq, swa_heads * hd), lambda b, i: (b * wblocks + i, 0)),
        out_shape=jax.ShapeDtypeStruct((mp, swa_heads * hd), BF16),
        compiler_params=_params("parallel", "arbitrary"),
        name="swa_prompt_attention",
    )(bias_p, q_hm, k_hm, k_hm, v_hm, v_hm)

    tpad = 8
    nkg = WINDOW + tpad

    def stack_keys(win, new_hm):
        new = new_hm[:, mp:].reshape(kvh, t_new, nbs, hd).transpose(2, 0, 1, 3)
        new = jnp.pad(new, ((0, 0), (0, 0), (0, tpad - t_new), (0, 0)))
        return jnp.concatenate([win.transpose(0, 2, 1, 3).astype(BF16), new], axis=2).reshape(nbs, kvh * nkg, hd)

    rs = jnp.arange(t_new * swa_heads)
    t_of, h_of = rs // swa_heads, rs % swa_heads
    cs = jnp.arange(kvh * nkg)
    j = cs % nkg
    dist_s = jnp.where(j < WINDOW, WINDOW + t_of[:, None] - j[None, :], t_of[:, None] - (j[None, :] - WINDOW))
    own = (cs // nkg)[None, :] == (h_of // group)[:, None]
    ok_s = own & (dist_s >= 0) & (dist_s < WINDOW) & ((j < WINDOW + t_new)[None, :])
    bias_s = bias_table(dist_s, ok_s, own & ((j == WINDOW + t_new)[None, :]), h_of)
    q_s = q_hm[:, mp:].reshape(swa_heads, t_new, nbs, hd).transpose(2, 1, 0, 3).reshape(nbs, t_new * swa_heads, hd)
    nbi = 8
    att_s = pl.pallas_call(
        _swa_sample_kernel,
        grid=(nbs // nbi,),
        in_specs=[pl.BlockSpec(bias_s.shape, const2),
                  pl.BlockSpec((nbi, t_new * swa_heads, hd), lambda i: (i, 0, 0)),
                  pl.BlockSpec((nbi, kvh * nkg, hd), lambda i: (i, 0, 0)),
                  pl.BlockSpec((nbi, kvh * nkg, hd), lambda i: (i, 0, 0))],
        out_specs=pl.BlockSpec((nbi, t_new * swa_heads, hd), lambda i: (i, 0, 0)),
        out_shape=jax.ShapeDtypeStruct((nbs, t_new * swa_heads, hd), BF16),
        compiler_params=_params("parallel"),
        name="swa_sample_attention",
    )(bias_s, q_s, stack_keys(state_win_k, k_hm), stack_keys(state_win_v, v_hm))
    att_s = att_s.reshape(nbs, t_new, swa_heads * hd).transpose(1, 0, 2).reshape(ms, swa_heads * hd)

    y_p, y_s = _attn_out_moe(att_p, att_s, w_swa_o[0], h1, h1, sample_blk_of_all, m_rows, tab, 1, g_ffn[1],
                             w_router[1], b_router[1], w_up, b_up, w_down, b_down,
                             seq=seq, nb_prompt=nbp, n_prompt_blocks=n_prompt_blocks, split_rows=mp)

    def sample_major(x):
        return x.reshape(t_new, nbs, -1).transpose(1, 0, 2)

    def last_window(x):
        return x[:mp].reshape(nbp, seq, kvh, hd)[:, -WINDOW:]

    y_prompt = y_p.reshape(nbp, seq, d)
    y_sample = sample_major(y_s)
    lat_p = ckv_p.reshape(1, nbp, seq, kv_lora)
    krope_p = kr_p.reshape(1, nbp, seq, rope)
    lat_s = sample_major(ckv_s)[None]
    krope_s = sample_major(kr_s)[None]
    k_n = sample_major(k_all[mp:]).reshape(nbs, t_new, kvh, hd)
    v_n = sample_major(v_all[mp:]).reshape(nbs, t_new, kvh, hd)
    win_k_s = jnp.concatenate([state_win_k, k_n], axis=1)[:, -WINDOW:]
    win_v_s = jnp.concatenate([state_win_v, v_n], axis=1)[:, -WINDOW:]
    return (y_prompt, y_sample, lat_p, krope_p, lat_s, krope_s,
            last_window(k_all), last_window(v_all), win_k_s, win_v_s)
```
